```python
import math
import jax, jax.numpy as jnp
from jax import lax
import numpy as np

D_MODEL = 1024
BATCH = 8
SEQ = 4096
DEPTH = 2

N_META = 16
BLOCK_Q = 128
EPS = 1e-6
MLA_HEADS = 8
MLA_Q_RANK = 384
MLA_KV_RANK = 256
MLA_NOPE = 64
MLA_ROPE = 32
MLA_V = 64
MLA_QK = MLA_NOPE + MLA_ROPE
MLA_SCALE = MLA_QK ** -0.5
ROPE_BASE = 10000.0
DSA_HEADS = 8
DSA_HEAD_DIM = 64
DSA_WIDTH = DSA_HEADS * DSA_HEAD_DIM
DSA_SCALE = DSA_HEAD_DIM ** -0.5
IDX_HEADS = 8
IDX_DIM = 32
DSA_TOPK_MAX = 256
REL_BUCKETS = 32
REL_MAX_DIST = 128
MIX_IN_SIZES = (MLA_Q_RANK, MLA_KV_RANK, MLA_ROPE, DSA_WIDTH, DSA_WIDTH, DSA_WIDTH,
                IDX_HEADS * IDX_DIM, IDX_DIM, IDX_HEADS)
MIX_IN = (MLA_Q_RANK + MLA_KV_RANK + MLA_ROPE + 3 * DSA_WIDTH
          + IDX_HEADS * IDX_DIM + IDX_DIM + IDX_HEADS)
MIX_OUT = MLA_HEADS * MLA_V + DSA_WIDTH
CONV_WIDTH = 3
CONV_DIM = D_MODEL
D_FF = 3584
N_EXPERTS = 8
TOP_K = 2
N_EVEN = (DEPTH + 1) // 2
N_ODD = DEPTH // 2

kernel_name = "hybrid_mla_dsa_shortconv_moe_block"


def rms_norm(x, g):
    xf = x.astype(jnp.float32)
    y = xf * lax.rsqrt(jnp.mean(xf * xf, axis=-1, keepdims=True) + EPS)
    return (y * g.astype(jnp.float32)).astype(x.dtype)


def rope(x, pos):
    half = MLA_ROPE // 2
    inv = ROPE_BASE ** (-jnp.arange(half, dtype=jnp.float32) / half)
    ang = pos.astype(jnp.float32)[:, None] * inv[None, :]
    cos = jnp.cos(ang)[:, None, :]
    sin = jnp.sin(ang)[:, None, :]
    x1 = x[..., :half].astype(jnp.float32)
    x2 = x[..., half:].astype(jnp.float32)
    return jnp.concatenate([x1 * cos - x2 * sin, x1 * sin + x2 * cos], axis=-1).astype(x.dtype)


def rel_bucket(dist):
    max_exact = REL_BUCKETS // 2
    d = jnp.maximum(dist, 0)
    df = jnp.maximum(d, 1).astype(jnp.float32)
    large = max_exact + (jnp.log(df / max_exact) / math.log(REL_MAX_DIST / max_exact)
                         * (REL_BUCKETS - max_exact)).astype(jnp.int32)
    large = jnp.minimum(large, REL_BUCKETS - 1)
    return jnp.where(d < max_exact, d, large)


def to_blocks(a, nb):
    a = a.reshape(a.shape[0], nb, BLOCK_Q, *a.shape[2:])
    return jnp.moveaxis(a, 1, 0)


def from_blocks(a):
    a = jnp.moveaxis(a, 0, 1)
    return a.reshape(a.shape[0], a.shape[1] * a.shape[2], *a.shape[3:])


def mla_heads(c_q, c_kv, k_rope, g_q_lat, g_kv_lat, w_uq, w_ukv, g_qn, g_kn, pos):
    b, lp, _ = c_q.shape
    q = (rms_norm(c_q, g_q_lat) @ w_uq).reshape(b, lp, MLA_HEADS, MLA_QK)
    kv = (rms_norm(c_kv, g_kv_lat) @ w_ukv).reshape(b, lp, MLA_HEADS, MLA_NOPE + MLA_V)
    k_rope_h = jnp.broadcast_to(k_rope[:, :, None, :], (b, lp, MLA_HEADS, MLA_ROPE))
    k = jnp.concatenate([kv[..., :MLA_NOPE], k_rope_h], axis=-1)
    v = kv[..., MLA_NOPE:]
    q = rms_norm(q, g_qn)
    k = rms_norm(k, g_kn)
    q = jnp.concatenate([q[..., :MLA_NOPE], rope(q[..., MLA_NOPE:], pos)], axis=-1)
    k = jnp.concatenate([k[..., :MLA_NOPE], rope(k[..., MLA_NOPE:], pos)], axis=-1)
    nb = lp // BLOCK_Q

    def attend(blk):
        qb, qpos = blk
        s = jnp.einsum("bqhd,bkhd->bhqk", qb, k).astype(jnp.float32) * MLA_SCALE
        s = jnp.where(pos[None, :] <= qpos[:, None], s, -jnp.inf)
        p = jax.nn.softmax(s, axis=-1).astype(v.dtype)
        return jnp.einsum("bhqk,bkhd->bqhd", p, v)

    o = lax.map(attend, (to_blocks(q, nb), pos.reshape(nb, BLOCK_Q)))
    return from_blocks(o).reshape(b, lp, MLA_HEADS * MLA_V)


def dsa_heads(q, k, v, q_idx, k_idx, w_idx, g_qn, g_kn, rel_bias, pos, n_keys):
    b, lp, _ = q.shape
    top_k = min(DSA_TOPK_MAX, n_keys // 4)
    q = rms_norm(q.reshape(b, lp, DSA_HEADS, DSA_HEAD_DIM), g_qn)
    k = rms_norm(k.reshape(b, lp, DSA_HEADS, DSA_HEAD_DIM), g_kn)
    kv = jnp.concatenate([k.reshape(b, lp, DSA_WIDTH), v], axis=-1)
    q_idx = q_idx.reshape(b, lp, IDX_HEADS, IDX_DIM)
    nb = lp // BLOCK_Q

    def attend(blk):
        qb, qib, wib, qpos = blk
        act = jax.nn.relu(jnp.einsum("bqhd,bkd->bqhk", qib, k_idx).astype(jnp.float32))
        score = jnp.einsum("bqh,bqhk->bqk", wib.astype(jnp.float32), act)
        score = jnp.where(pos[None, :] <= qpos[:, None], score, -jnp.inf)
        _, sel = lax.top_k(score, top_k)
        kv_sel = jax.vmap(lambda a, i: a[i])(kv, sel.reshape(b, BLOCK_Q * top_k))
        kv_sel = kv_sel.reshape(b, BLOCK_Q, top_k, 2, DSA_HEADS, DSA_HEAD_DIM)
        k_sel = kv_sel[:, :, :, 0]
        v_sel = kv_sel[:, :, :, 1]
        dist = qpos[None, :, None] - sel
        s = jnp.einsum("bqhd,bqkhd->bhqk", qb, k_sel).astype(jnp.float32) * DSA_SCALE
        s = s + jnp.transpose(rel_bias[rel_bucket(dist)], (0, 3, 1, 2)).astype(jnp.float32)
        s = jnp.where((dist >= 0)[:, None], s, -jnp.inf)
        p = jax.nn.softmax(s, axis=-1).astype(v_sel.dtype)
        return jnp.einsum("bhqk,bqkhd->bqhd", p, v_sel)

    o = lax.map(attend, (to_blocks(q, nb), to_blocks(q_idx, nb), to_blocks(w_idx, nb),
                         pos.reshape(nb, BLOCK_Q)))
    return from_blocks(o).reshape(b, lp, DSA_WIDTH)


def mla_dsa_mixer(h, w_mix_in, g_q_lat, g_kv_lat, w_uq, w_ukv, mla_qn, mla_kn,
                  dsa_qn, dsa_kn, rel_bias, w_mix_out, pos, n_keys):
    split_at = np.cumsum(MIX_IN_SIZES)[:-1].tolist()
    c_q, c_kv, k_rope, dq, dk, dv, iq, ik, iw = jnp.split(h @ w_mix_in, split_at, axis=-1)
    o_mla = mla_heads(c_q, c_kv, k_rope, g_q_lat, g_kv_lat, w_uq, w_ukv, mla_qn, mla_kn, pos)
    o_dsa = dsa_heads(dq, dk, dv, iq, ik, iw, dsa_qn, dsa_kn, rel_bias, pos, n_keys)
    return jnp.concatenate([o_mla, o_dsa], axis=-1) @ w_mix_out


def short_conv_mixer(h, w_in, conv_w, w_out):
    b_gate, c_gate, u = jnp.split(h @ w_in, 3, axis=-1)
    z = c_gate * u
    y = lax.conv_general_dilated(z, conv_w.astype(z.dtype), window_strides=(1,),
                                 padding=[(CONV_WIDTH - 1, 0)],
                                 dimension_numbers=("NWC", "WIO", "NWC"),
                                 feature_group_count=CONV_DIM)
    return (b_gate * y) @ w_out


def swiglu(h, w1, w3, w2):
    return (jax.nn.silu(h @ w1) * (h @ w3)) @ w2


def moe_swiglu(h, w_router, w1, w3, w2):
    logits = (h @ w_router).astype(jnp.float32)
    top_val, top_idx = lax.top_k(logits, TOP_K)
    gates = jax.nn.softmax(top_val, axis=-1)
    gate_dense = jnp.sum(jax.nn.one_hot(top_idx, N_EXPERTS, dtype=jnp.float32)
                         * gates[..., None], axis=-2).astype(h.dtype)
    out = jnp.zeros_like(h)
    for e in range(N_EXPERTS):
        out = out + gate_dense[..., e:e + 1] * swiglu(h, w1[e], w3[e], w2[e])
    return out


def setup_inputs(seed: int = 0) -> dict:
    key = jax.random.key(seed)
    keys = jax.random.split(key, 40)
    ctr = [0]

    def nxt():
        k = keys[ctr[0]]
        ctr[0] += 1
        return k

    def w(shape, fan_in):
        return jax.random.normal(nxt(), shape, jnp.float32) * (fan_in ** -0.5)

    def gain(shape):
        return 1.0 + 0.02 * jax.random.normal(nxt(), shape, jnp.float32)

    NE, NO = N_EVEN, N_ODD
    return {
        "x": jax.random.normal(nxt(), (BATCH, SEQ, D_MODEL), jnp.float32),
        "meta_tokens": jax.random.normal(nxt(), (N_META, D_MODEL), jnp.float32),
        "rel_bias": 0.5 * jax.random.normal(nxt(), (REL_BUCKETS, DSA_HEADS), jnp.float32),
        "ev_norm_mix": gain((NE, D_MODEL)),
        "ev_w_mix_in": w((NE, D_MODEL, MIX_IN), D_MODEL),
        "ev_g_q_lat": gain((NE, MLA_Q_RANK)),
        "ev_g_kv_lat": gain((NE, MLA_KV_RANK)),
        "ev_w_uq": w((NE, MLA_Q_RANK, MLA_HEADS * MLA_QK), MLA_Q_RANK),
        "ev_w_ukv": w((NE, MLA_KV_RANK, MLA_HEADS * (MLA_NOPE + MLA_V)), MLA_KV_RANK),
        "ev_mla_q_norm": gain((NE, MLA_QK)),
        "ev_mla_k_norm": gain((NE, MLA_QK)),
        "ev_dsa_q_norm": gain((NE, DSA_HEAD_DIM)),
        "ev_dsa_k_norm": gain((NE, DSA_HEAD_DIM)),
        "ev_w_mix_out": w((NE, MIX_OUT, D_MODEL), MIX_OUT),
        "ev_norm_ffn": gain((NE, D_MODEL)),
        "ev_w1": w((NE, D_MODEL, D_FF), D_MODEL),
        "ev_w3": w((NE, D_MODEL, D_FF), D_MODEL),
        "ev_w2": w((NE, D_FF, D_MODEL), D_FF),
        "od_norm_mix": gain((NO, D_MODEL)),
        "od_w_in": w((NO, D_MODEL, 3 * CONV_DIM), D_MODEL),
        "od_conv_w": w((NO, CONV_WIDTH, 1, CONV_DIM), CONV_WIDTH),
        "od_w_out": w((NO, CONV_DIM, D_MODEL), CONV_DIM),
        "od_norm_ffn": gain((NO, D_MODEL)),
        "od_w_router": w((NO, D_MODEL, N_EXPERTS), D_MODEL),
        "od_w1": w((NO, N_EXPERTS, D_MODEL, D_FF), D_MODEL),
        "od_w3": w((NO, N_EXPERTS, D_MODEL, D_FF), D_MODEL),
        "od_w2": w((NO, N_EXPERTS, D_FF, D_MODEL), D_FF),
    }


def reference(x, meta_tokens, rel_bias,
              ev_norm_mix, ev_w_mix_in, ev_g_q_lat, ev_g_kv_lat, ev_w_uq, ev_w_ukv,
              ev_mla_q_norm, ev_mla_k_norm, ev_dsa_q_norm, ev_dsa_k_norm, ev_w_mix_out,
              ev_norm_ffn, ev_w1, ev_w3, ev_w2,
              od_norm_mix, od_w_in, od_conv_w, od_w_out,
              od_norm_ffn, od_w_router, od_w1, od_w3, od_w2):
    b, seq, d = x.shape
    l_tot = seq + N_META
    lp = -(-l_tot // BLOCK_Q) * BLOCK_Q
    meta = jnp.broadcast_to(meta_tokens[None].astype(x.dtype), (b, N_META, d))
    h = jnp.concatenate([meta, x, jnp.zeros((b, lp - l_tot, d), x.dtype)], axis=1)
    pos = jnp.arange(lp, dtype=jnp.int32)
    for layer in range(DEPTH):
        i = layer // 2
        if layer % 2 == 0:
            h = h + mla_dsa_mixer(rms_norm(h, ev_norm_mix[i]), ev_w_mix_in[i], ev_g_q_lat[i],
                                  ev_g_kv_lat[i], ev_w_uq[i], ev_w_ukv[i], ev_mla_q_norm[i],
                                  ev_mla_k_norm[i], ev_dsa_q_norm[i], ev_dsa_k_norm[i],
                                  rel_bias, ev_w_mix_out[i], pos, l_tot)
            h = h + swiglu(rms_norm(h, ev_norm_ffn[i]), ev_w1[i], ev_w3[i], ev_w2[i])
        else:
            h = h + short_conv_mixer(rms_norm(h, od_norm_mix[i]), od_w_in[i], od_conv_w[i],
                                     od_w_out[i])
            h = h + moe_swiglu(rms_norm(h, od_norm_ffn[i]), od_w_router[i], od_w1[i],
                               od_w3[i], od_w2[i])
    return h[:, N_META:l_tot]
```

```python
import functools
import math

import numpy as np
import jax
import jax.numpy as jnp
from jax import lax
from jax.experimental import pallas as pl
from jax.experimental.pallas import tpu as pltpu

F32 = jnp.float32
BF16 = jnp.bfloat16
I32 = jnp.int32

D_MODEL = 1024
N_META = 16
BLOCK_Q = 128
EPS = 1e-6
MLA_HEADS = 8
MLA_Q_RANK = 384
MLA_KV_RANK = 256
MLA_NOPE = 64
MLA_ROPE = 32
MLA_V = 64
MLA_QK = MLA_NOPE + MLA_ROPE
MLA_SCALE = MLA_QK ** -0.5
ROPE_BASE = 10000.0
DSA_HEADS = 8
DSA_HEAD_DIM = 64
DSA_WIDTH = DSA_HEADS * DSA_HEAD_DIM
DSA_SCALE = DSA_HEAD_DIM ** -0.5
IDX_HEADS = 8
IDX_DIM = 32
DSA_TOPK_MAX = 256
REL_BUCKETS = 32
REL_MAX_DIST = 128
MIX_IN_SIZES = (MLA_Q_RANK, MLA_KV_RANK, MLA_ROPE, DSA_WIDTH, DSA_WIDTH, DSA_WIDTH,
                IDX_HEADS * IDX_DIM, IDX_DIM, IDX_HEADS)
CONV_WIDTH = 3
N_EXPERTS = 8
TOP_K = 2

LANES = 128
VMEM_LIMIT_BYTES = 56 * 1024 * 1024

SEQ_TILE = 3 * LANES
LOG2E = math.log2(math.e)
NEG_BIG = -1e30
INT_MIN = -2 ** 31

_C_CQ = 0
_C_CKV = _C_CQ + MLA_Q_RANK
_C_KRM = _C_CKV + MLA_KV_RANK
_C_KRS = _C_KRM + LANES
_C_DQ = _C_KRS + LANES
_C_DK = _C_DQ + DSA_WIDTH
_C_DV = _C_DK + DSA_WIDTH
_C_IQ = _C_DV + DSA_WIDTH
_C_IK = _C_IQ + IDX_HEADS * IDX_DIM
_C_IW = _C_IK + LANES
_C_END = _C_IW + LANES


def _cparams(sem):
    return pltpu.CompilerParams(dimension_semantics=sem, vmem_limit_bytes=VMEM_LIMIT_BYTES)


def _row_tile(n_rows, candidates=(1024, 768, 512, 384, 256, 128)):
    for c in candidates:
        if n_rows % c == 0:
            return c
    raise ValueError(f"no row tile for {n_rows}")


def _rms(x, g):
    ms = jnp.mean(x * x, axis=-1, keepdims=True)
    return x * lax.rsqrt(ms + EPS) * g


def _dot(a, b):
    return jnp.dot(a, b, preferred_element_type=F32)


def _dot_nt(a, b):
    return lax.dot_general(a, b, (((1,), (1,)), ((), ())), preferred_element_type=F32)


def _prep_kernel(h_ref, g_ref, wext_ref, gql_ref, gkvl_ref, wqm_ref, wqs_ref, wkk_ref, wkv_ref,
                 gqm_ref, gqs_ref, gkm_ref, gks_ref, gdq_ref, gdk_ref, cos_ref, sin_ref,
                 qm_o, km_o, vm_o, qd_o, kd_o, vd_o, iq_o, ik_o, iw_o):
    xn = _rms(h_ref[...], g_ref[...]).astype(BF16)

    def proj(lo, hi):
        return _dot(xn, wext_ref[:, lo:hi])

    cos = cos_ref[...]
    sin = sin_ref[...]
    lane = lax.broadcasted_iota(I32, (xn.shape[0], LANES), 1)

    cqn = _rms(proj(_C_CQ, _C_CKV), gql_ref[...]).astype(BF16)
    q_main = _dot(cqn, wqm_ref[...])
    q_swap = _dot(cqn, wqs_ref[...])
    for hd in range(MLA_HEADS):
        sl = slice(hd * LANES, (hd + 1) * LANES)
        a = q_main[:, sl]
        r = lax.rsqrt(jnp.sum(a * a, axis=-1, keepdims=True) * (1.0 / MLA_QK) + EPS)
        out = (a * r * gqm_ref[...]) * cos + (q_swap[:, sl] * r * gqs_ref[...]) * sin
        qm_o[:, sl] = out.astype(BF16)

    ckvn = _rms(proj(_C_CKV, _C_KRM), gkvl_ref[...]).astype(BF16)
    k_nope = _dot(ckvn, wkk_ref[...])
    vm_o[...] = _dot(ckvn, wkv_ref[...]).astype(BF16)
    kr_main = proj(_C_KRM, _C_KRS)
    kr_swap = proj(_C_KRS, _C_DQ)
    for hd in range(MLA_HEADS):
        sl = slice(hd * LANES, (hd + 1) * LANES)
        a = k_nope[:, sl] + kr_main
        r = lax.rsqrt(jnp.sum(a * a, axis=-1, keepdims=True) * (1.0 / MLA_QK) + EPS)
        out = (a * r * gkm_ref[...]) * cos + (kr_swap * r * gks_ref[...]) * sin
        km_o[:, sl] = out.astype(BF16)

    first = lane < DSA_HEAD_DIM
    for (lo, g2_ref, o_ref) in ((_C_DQ, gdq_ref, qd_o), (_C_DK, gdk_ref, kd_o)):
        for pr in range(DSA_HEADS // 2):
            x = proj(lo + pr * LANES, lo + (pr + 1) * LANES)
            sq = x * x
            s0 = jnp.sum(jnp.where(first, sq, 0.0), axis=-1, keepdims=True)
            s1 = jnp.sum(jnp.where(first, 0.0, sq), axis=-1, keepdims=True)
            r0 = lax.rsqrt(s0 * (1.0 / DSA_HEAD_DIM) + EPS)
            r1 = lax.rsqrt(s1 * (1.0 / DSA_HEAD_DIM) + EPS)
            out = x * jnp.where(first, r0, r1) * g2_ref[...]
            o_ref[:, pr * LANES:(pr + 1) * LANES] = out.astype(BF16)

    vd_o[...] = proj(_C_DV, _C_IQ).astype(BF16)
    iq_o[...] = proj(_C_IQ, _C_IK).astype(BF16)
    ik_o[...] = proj(_C_IK, _C_IW).astype(BF16)
    iw_o[...] = proj(_C_IW, _C_END)


def _prep_call(h, g_mix, wext, gql, gkvl, wqm, wqs, wkk, wkv, gqm, gqs, gkm, gks, gdq, gdk,
               cos_t, sin_t, lp):
    t = h.shape[0]
    tm = SEQ_TILE
    nt = lp // tm
    row = lambda w: pl.BlockSpec((tm, w), lambda i: (i, 0))
    full = lambda a: pl.BlockSpec(a.shape, lambda i: (0, 0))
    tab = pl.BlockSpec((tm, LANES), lambda i: (i % nt, 0))
    hw = MLA_HEADS * LANES
    out_shape = [
        jax.ShapeDtypeStruct((t, hw), BF16), jax.ShapeDtypeStruct((t, hw), BF16),
        jax.ShapeDtypeStruct((t, MLA_HEADS * MLA_V), BF16),
        jax.ShapeDtypeStruct((t, DSA_WIDTH), BF16), jax.ShapeDtypeStruct((t, DSA_WIDTH), BF16),
        jax.ShapeDtypeStruct((t, DSA_WIDTH), BF16),
        jax.ShapeDtypeStruct((t, IDX_HEADS * IDX_DIM), BF16),
        jax.ShapeDtypeStruct((t, LANES), BF16), jax.ShapeDtypeStruct((t, LANES), F32),
    ]
    return pl.pallas_call(
        _prep_kernel,
        grid=(t // tm,),
        in_specs=[row(D_MODEL), full(g_mix), full(wext), full(gql), full(gkvl), full(wqm),
                  full(wqs), full(wkk), full(wkv), full(gqm), full(gqs), full(gkm), full(gks),
                  full(gdq), full(gdk), tab, tab],
        out_specs=[row(s.shape[1]) for s in out_shape],
        out_shape=out_shape,
        compiler_params=_cparams(("parallel",)),
        name="prep_mix_in",
    )(h, g_mix, wext, gql, gkvl, wqm, wqs, wkk, wkv, gqm, gqs, gkm, gks, gdq, gdk, cos_t, sin_t)


def _softmax_pv(s_scr, m_scr, l_scr, acc_scr, v_ref, v_lo, n_chunks):
    tq = s_scr.shape[1]
    tk = s_scr.shape[2]

    def fold(x, op):
        out = x[:, 0:LANES]
        for j in range(1, tk // LANES):
            out = op(out, x[:, j * LANES:(j + 1) * LANES])
        return out

    m_scr[...] = jnp.full(m_scr.shape, -jnp.inf, F32)

    def max_body(c, carry):
        m_scr[...] = jnp.maximum(m_scr[...], fold(s_scr[c], jnp.maximum))
        return carry

    lax.fori_loop(0, n_chunks, max_body, 0)
    m_row = jnp.max(m_scr[...], axis=-1, keepdims=True)

    l_scr[...] = jnp.zeros(l_scr.shape, F32)
    acc_scr[...] = jnp.zeros(acc_scr.shape, F32)

    def pv_body(c, carry):
        p = jnp.exp2(s_scr[c] - m_row)
        l_scr[...] += fold(p, jnp.add)
        vc = v_ref[pl.ds(pl.multiple_of(c * tk, tk), tk), v_lo:v_lo + LANES]
        acc_scr[...] += _dot(p.astype(BF16), vc)
        return carry

    lax.fori_loop(0, n_chunks, pv_body, 0)
    l_row = jnp.sum(l_scr[...], axis=-1, keepdims=True)
    return acc_scr[...] / l_row


def _mla_kernel(q_ref, k_ref, v_ref, o_ref, s_scr, m_scr, l_scr, acc_scr):
    qi = pl.program_id(1)
    tq = q_ref.shape[0]
    c2 = MLA_SCALE * LOG2E
    row = lax.broadcasted_iota(I32, (tq, tq), 0)
    col = lax.broadcasted_iota(I32, (tq, tq), 1)
    lane = lax.broadcasted_iota(I32, (tq, LANES), 1)

    pair_out = None
    for hd in range(MLA_HEADS):
        sl = slice(hd * LANES, (hd + 1) * LANES)
        qh = q_ref[:, sl]

        def score_body(c, carry):
            kc = k_ref[pl.ds(pl.multiple_of(c * tq, tq), tq), sl]
            s_scr[c] = _dot_nt(qh, kc) * c2
            return carry

        lax.fori_loop(0, qi, score_body, 0)
        kc = k_ref[pl.ds(pl.multiple_of(qi * tq, tq), tq), sl]
        s_scr[qi] = jnp.where(col <= row, _dot_nt(qh, kc) * c2, NEG_BIG)

        pr = hd // 2
        o = _softmax_pv(s_scr, m_scr, l_scr, acc_scr, v_ref, pr * LANES, qi + 1)
        if hd % 2 == 0:
            pair_out = o
        else:
            o_ref[:, pr * LANES:(pr + 1) * LANES] = jnp.where(
                lane < MLA_V, pair_out, o).astype(o_ref.dtype)


def _mla_call(q, k, v):
    b, lp, _ = q.shape
    tq = SEQ_TILE
    nq = lp // tq
    return pl.pallas_call(
        _mla_kernel,
        grid=(b, nq),
        in_specs=[pl.BlockSpec((None, tq, q.shape[2]), lambda bi, i: (bi, i, 0)),
                  pl.BlockSpec((None, lp, k.shape[2]), lambda bi, i: (bi, 0, 0)),
                  pl.BlockSpec((None, lp, v.shape[2]), lambda bi, i: (bi, 0, 0))],
        out_specs=pl.BlockSpec((None, tq, v.shape[2]), lambda bi, i: (bi, i, 0)),
        out_shape=jax.ShapeDtypeStruct((b, lp, v.shape[2]), BF16),
        scratch_shapes=[pltpu.VMEM((nq, tq, tq), F32), pltpu.VMEM((tq, LANES), F32),
                        pltpu.VMEM((tq, LANES), F32), pltpu.VMEM((tq, LANES), F32)],
        compiler_params=_cparams(("parallel", "arbitrary")),
        name="mla_attention",
    )(q, k, v)


def _dsa_kernel(iq_ref, iw_ref, ik_ref, q_ref, k_ref, v_ref, tz_ref, o_ref,
                key_scr, madd_scr, s_scr, wb_scr, ans_scr, jst_scr, m_scr, l_scr, acc_scr,
                *, top_k):
    qi = pl.program_id(1)
    tq = q_ref.shape[0]
    nsub = tq // LANES
    n_chunks = qi + 1
    row = lax.broadcasted_iota(I32, (tq, tq), 0)
    col = lax.broadcasted_iota(I32, (tq, tq), 1)
    lane = lax.broadcasted_iota(I32, (tq, LANES), 1)

    iw = iw_ref[...]
    lane_group = lax.shift_right_logical(lane, int(math.log2(IDX_DIM)))
    q_heads = []
    for hd in range(IDX_HEADS):
        wb_scr[hd] = jnp.broadcast_to(iw[:, hd:hd + 1], (tq, LANES))
        quad = iq_ref[:, (hd // 4) * LANES:(hd // 4 + 1) * LANES].astype(F32)
        q_heads.append(jnp.where(lane_group == hd % 4, quad, 0.0).astype(BF16))

    def index_chunk(c, causal_mask):
        ikc = ik_ref[pl.ds(pl.multiple_of(c * tq, tq), tq), :]
        sc = jnp.zeros((tq, tq), F32)
        for hd in range(IDX_HEADS):
            act = jnp.maximum(_dot_nt(q_heads[hd], ikc), 0.0)
            wb = wb_scr[hd]
            sc = sc + jnp.concatenate([wb] * nsub, axis=1) * act
        sc = jnp.where(sc == 0.0, 0.0, sc)
        if causal_mask:
            sc = jnp.where(col <= row, sc, -jnp.inf)
        bits = pltpu.bitcast(sc, I32)
        key_scr[c] = bits ^ (lax.shift_right_arithmetic(bits, 31) & 0x7FFFFFFF)

    def index_body(c, carry):
        index_chunk(c, False)
        return carry

    lax.fori_loop(0, qi, index_body, 0)
    index_chunk(qi, True)

    kf = float(top_k)
    for rb in range(nsub):
        rs = slice(rb * LANES, (rb + 1) * LANES)
        t_pos = qi * tq + rb * LANES + lax.broadcasted_iota(I32, (LANES, LANES), 0)

        def count(pred_fn):
            def body(c, acc):
                kk = key_scr[c, rs, :]
                idx0 = c * tq
                for j in range(nsub):
                    acc = acc + jnp.where(
                        pred_fn(kk[:, j * LANES:(j + 1) * LANES], idx0 + j * LANES), 1.0, 0.0)
                return acc
            acc = lax.fori_loop(0, n_chunks, body, jnp.zeros((LANES, LANES), F32))
            return jnp.broadcast_to(jnp.sum(acc, axis=-1, keepdims=True), (LANES, LANES))

        done0 = (t_pos + 1 <= top_k).astype(I32)
        ans0 = jnp.full((LANES, LANES), INT_MIN, I32)

        def bis_cond(st):
            bit, _, _, n_open = st
            return jnp.logical_and(bit >= 0, n_open > 0.0)

        def bis_body(st):
            bit, ans, done, _ = st
            cand = ans + lax.shift_left(jnp.int32(1), bit)
            cnt = count(lambda kk, _i: kk >= cand)
            open_ = done == 0
            ans = jnp.where(jnp.logical_and(open_, cnt >= kf), cand, ans)
            done = jnp.where(jnp.logical_and(open_, cnt == kf), 1, done)
            n_open = jnp.sum(jnp.where(done == 0, 1.0, 0.0))
            return bit - 1, ans, done, n_open

        n_open0 = jnp.sum(jnp.where(done0 == 0, 1.0, 0.0))
        _, ans, done, n_open = lax.while_loop(bis_cond, bis_body,
                                              (jnp.int32(31), ans0, done0, n_open0))
        ans_scr[rs, :] = ans
        jst_scr[rs, :] = jnp.full((LANES, LANES), 2 ** 31 - 1, I32)

        @pl.when(n_open > 0.0)
        def _():
            lane_b = lax.broadcasted_iota(I32, (LANES, LANES), 1)
            need = kf - count(lambda kk, _i: kk > ans)

            def tie_body(i, jst):
                cand = jst + lax.shift_left(jnp.int32(1), 13 - i)
                cnt = count(lambda kk, i0: jnp.logical_and(kk == ans, lane_b + i0 < cand))
                return jnp.where(cnt < need, cand, jst)

            jst = lax.fori_loop(0, 14, tie_body, jnp.zeros((LANES, LANES), I32))
            jst_scr[rs, :] = jnp.where(done == 0, jst, 2 ** 31 - 1)

    ans_all = ans_scr[...]
    jst_all = jst_scr[...]
    ans_w = jnp.concatenate([ans_all] * nsub, axis=1)
    jst_w = jnp.concatenate([jst_all] * nsub, axis=1)

    def mask_chunk(c, causal_mask):
        kk = key_scr[c]
        sel = jnp.logical_or(kk > ans_w,
                             jnp.logical_and(kk == ans_w, col + c * tq <= jst_w))
        if causal_mask:
            sel = jnp.logical_and(sel, col <= row)
        madd_scr[c] = jnp.where(sel, 0.0, NEG_BIG)

    def mask_body(c, carry):
        mask_chunk(c, False)
        return carry

    lax.fori_loop(0, qi, mask_body, 0)
    mask_chunk(qi, True)

    c2 = DSA_SCALE * LOG2E
    pair_out = None
    for hd in range(DSA_HEADS):
        pr = hd // 2
        sl = slice(pr * LANES, (pr + 1) * LANES)
        qp = q_ref[:, sl].astype(F32)
        own = lane < DSA_HEAD_DIM if hd % 2 == 0 else lane >= DSA_HEAD_DIM
        qh = jnp.where(own, qp, 0.0).astype(BF16)

        def score_body(c, carry):
            kc = k_ref[pl.ds(pl.multiple_of(c * tq, tq), tq), sl]
            s_scr[c] = _dot_nt(qh, kc) * c2 + madd_scr[c]
            return carry

        lax.fori_loop(0, n_chunks, score_body, 0)

        tz_near = tz_ref[hd, :, LANES:2 * LANES]
        tz_far = tz_ref[hd, :, 0:LANES]
        for a in range(nsub):
            ra = slice(a * LANES, (a + 1) * LANES)
            s_scr[qi, ra, ra] += tz_near
            if a >= 1:
                s_scr[qi, ra, (a - 1) * LANES:a * LANES] += tz_far

        @pl.when(qi >= 1)
        def _():
            s_scr[qi - 1, 0:LANES, (nsub - 1) * LANES:nsub * LANES] += tz_far

        o = _softmax_pv(s_scr, m_scr, l_scr, acc_scr, v_ref, pr * LANES, n_chunks)
        if hd % 2 == 0:
            pair_out = o
        else:
            o_ref[:, sl] = jnp.where(lane < DSA_HEAD_DIM, pair_out, o).astype(o_ref.dtype)


def _dsa_call(iq, iw, ik, q, k, v, tz, top_k):
    b, lp, _ = q.shape
    tq = SEQ_TILE
    nq = lp // tq
    qspec = lambda w: pl.BlockSpec((None, tq, w), lambda bi, i: (bi, i, 0))
    kspec = lambda w: pl.BlockSpec((None, lp, w), lambda bi, i: (bi, 0, 0))
    return pl.pallas_call(
        functools.partial(_dsa_kernel, top_k=top_k),
        grid=(b, nq),
        in_specs=[qspec(iq.shape[2]), qspec(LANES), kspec(LANES), qspec(DSA_WIDTH),
                  kspec(DSA_WIDTH), kspec(DSA_WIDTH),
                  pl.BlockSpec(tz.shape, lambda bi, i: (0, 0, 0))],
        out_specs=qspec(DSA_WIDTH),
        out_shape=jax.ShapeDtypeStruct((b, lp, DSA_WIDTH), BF16),
        scratch_shapes=[pltpu.VMEM((nq, tq, tq), I32), pltpu.VMEM((nq, tq, tq), F32),
                        pltpu.VMEM((nq, tq, tq), F32), pltpu.VMEM((IDX_HEADS, tq, LANES), F32),
                        pltpu.VMEM((tq, LANES), I32), pltpu.VMEM((tq, LANES), I32),
                        pltpu.VMEM((tq, LANES), F32), pltpu.VMEM((tq, LANES), F32),
                        pltpu.VMEM((tq, LANES), F32)],
        compiler_params=_cparams(("parallel", "arbitrary")),
        name="dsa_attention",
    )(iq, iw, ik, q, k, v, tz)


def _mix_out_kernel(h_ref, a_ref, b_ref, wa_ref, wb_ref, o_ref):
    o_ref[...] = h_ref[...] + _dot(a_ref[...], wa_ref[...]) + _dot(b_ref[...], wb_ref[...])


def _mix_out_call(h, a, b, wa, wb):
    t = h.shape[0]
    tm = _row_tile(t)
    row = lambda w: pl.BlockSpec((tm, w), lambda i: (i, 0))
    full = lambda x: pl.BlockSpec(x.shape, lambda i: (0, 0))
    return pl.pallas_call(
        _mix_out_kernel,
        grid=(t // tm,),
        in_specs=[row(D_MODEL), row(a.shape[1]), row(b.shape[1]), full(wa), full(wb)],
        out_specs=row(D_MODEL),
        out_shape=jax.ShapeDtypeStruct(h.shape, F32),
        compiler_params=_cparams(("parallel",)),
        name="mix_out",
    )(h, a, b, wa, wb)


def _ffn_kernel(h_ref, g_ref, w1_ref, w3_ref, w2_ref, o_ref, xn_scr, acc_scr):
    f = pl.program_id(1)

    @pl.when(f == 0)
    def _():
        xn_scr[...] = _rms(h_ref[...], g_ref[...]).astype(BF16)
        acc_scr[...] = jnp.zeros(acc_scr.shape, F32)

    xn = xn_scr[...]
    a = _dot(xn, w1_ref[...])
    act = (a * jax.nn.sigmoid(a)) * _dot(xn, w3_ref[...])
    acc_scr[...] += _dot(act.astype(BF16), w2_ref[...])

    @pl.when(f == pl.num_programs(1) - 1)
    def _():
        o_ref[...] = h_ref[...] + acc_scr[...]


def _ffn_call(h, g, w1, w3, w2):
    t = h.shape[0]
    d_ff = w1.shape[1]
    tm = _row_tile(t)
    tf = 512 if d_ff % 512 == 0 else d_ff
    return pl.pallas_call(
        _ffn_kernel,
        grid=(t // tm, d_ff // tf),
        in_specs=[pl.BlockSpec((tm, D_MODEL), lambda i, f: (i, 0)),
                  pl.BlockSpec(g.shape, lambda i, f: (0, 0)),
                  pl.BlockSpec((D_MODEL, tf), lambda i, f: (0, f)),
                  pl.BlockSpec((D_MODEL, tf), lambda i, f: (0, f)),
                  pl.BlockSpec((tf, D_MODEL), lambda i, f: (f, 0))],
        out_specs=pl.BlockSpec((tm, D_MODEL), lambda i, f: (i, 0)),
        out_shape=jax.ShapeDtypeStruct(h.shape, F32),
        scratch_shapes=[pltpu.VMEM((tm, D_MODEL), BF16), pltpu.VMEM((tm, D_MODEL), F32)],
        compiler_params=_cparams(("parallel", "arbitrary")),
        name="dense_swiglu",
    )(h, g, w1, w3, w2)


def _conv_kernel(h_ref, g_ref, win_ref, cw_ref, wout_ref, o_ref, z_scr):
    i = pl.program_id(1)
    tm = h_ref.shape[0]
    halo = 8
    x = h_ref[...]
    xn = _rms(x, g_ref[...]).astype(BF16)
    c_gate = _dot(xn, win_ref[:, D_MODEL:2 * D_MODEL])
    u = _dot(xn, win_ref[:, 2 * D_MODEL:3 * D_MODEL])
    z = c_gate * u

    @pl.when(i == 0)
    def _():
        z_scr[0:halo, :] = jnp.zeros((halo, D_MODEL), F32)

    @pl.when(i > 0)
    def _():
        z_scr[0:halo, :] = z_scr[tm:tm + halo, :]

    z_scr[halo:halo + tm, :] = z
    cw = cw_ref[...]
    y = (cw[0:1, :] * z_scr[halo - 2:halo - 2 + tm, :]
         + cw[1:2, :] * z_scr[halo - 1:halo - 1 + tm, :]
         + cw[2:3, :] * z)
    b_gate = _dot(xn, win_ref[:, 0:D_MODEL])
    o_ref[...] = x + _dot((b_gate * y).astype(BF16), wout_ref[...])


def _conv_call(h3, g, w_in, cw, w_out):
    b, lp, d = h3.shape
    tm = SEQ_TILE
    full = lambda a: pl.BlockSpec(a.shape, lambda bi, i: (0,) * a.ndim)
    return pl.pallas_call(
        _conv_kernel,
        grid=(b, lp // tm),
        in_specs=[pl.BlockSpec((None, tm, d), lambda bi, i: (bi, i, 0)), full(g), full(w_in),
                  full(cw), full(w_out)],
        out_specs=pl.BlockSpec((None, tm, d), lambda bi, i: (bi, i, 0)),
        out_shape=jax.ShapeDtypeStruct(h3.shape, F32),
        scratch_shapes=[pltpu.VMEM((tm + 8, d), F32)],
        compiler_params=_cparams(("arbitrary", "arbitrary")),
        name="short_conv_mixer",
    )(h3, g, w_in, cw, w_out)


def _moe_kernel(h_ref, g_ref, wr_ref, w1_ref, w3_ref, w2_ref, o_ref, xn_scr, gate_scr, acc_scr):
    e = pl.program_id(1)
    f = pl.program_id(2)
    tm = h_ref.shape[0]
    lane = lax.broadcasted_iota(I32, (tm, LANES), 1)
    lane_f = lane.astype(F32)

    @pl.when(jnp.logical_and(e == 0, f == 0))
    def _():
        xn = _rms(h_ref[...], g_ref[...]).astype(BF16)
        xn_scr[...] = xn
        acc_scr[...] = jnp.zeros(acc_scr.shape, F32)
        logits = jnp.where(lane < N_EXPERTS, _dot(xn, wr_ref[...]), -jnp.inf)
        v1 = jnp.max(logits, axis=-1, keepdims=True)
        i1 = jnp.min(jnp.where(logits == v1, lane_f, float(LANES)), axis=-1, keepdims=True)
        rest = jnp.where(lane_f == i1, -jnp.inf, logits)
        v2 = jnp.max(rest, axis=-1, keepdims=True)
        i2 = jnp.min(jnp.where(rest == v2, lane_f, float(LANES)), axis=-1, keepdims=True)
        e2 = jnp.exp(v2 - v1)
        den = 1.0 + e2
        gate_scr[...] = jnp.where(lane_f == i1, 1.0 / den,
                                  jnp.where(lane_f == i2, e2 / den, 0.0))

    xn = xn_scr[...]
    a = _dot(xn, w1_ref[...])
    act = (a * jax.nn.sigmoid(a)) * _dot(xn, w3_ref[...])
    gate_e = jnp.sum(jnp.where(lane == e, gate_scr[...], 0.0), axis=-1, keepdims=True)
    acc_scr[...] += gate_e * _dot(act.astype(BF16), w2_ref[...])

    @pl.when(jnp.logical_and(e == pl.num_programs(1) - 1, f == pl.num_programs(2) - 1))
    def _():
        o_ref[...] = h_ref[...] + acc_scr[...]


def _moe_call(h, g, wr, w1, w3, w2):
    t = h.shape[0]
    n_e, _, d_ff = w1.shape
    tm = _row_tile(t)
    tf = 512 if d_ff % 512 == 0 else d_ff
    return pl.pallas_call(
        _moe_kernel,
        grid=(t // tm, n_e, d_ff // tf),
        in_specs=[pl.BlockSpec((tm, D_MODEL), lambda i, e, f: (i, 0)),
                  pl.BlockSpec(g.shape, lambda i, e, f: (0, 0)),
                  pl.BlockSpec(wr.shape, lambda i, e, f: (0, 0)),
                  pl.BlockSpec((None, D_MODEL, tf), lambda i, e, f: (e, 0, f)),
                  pl.BlockSpec((None, D_MODEL, tf), lambda i, e, f: (e, 0, f)),
                  pl.BlockSpec((None, tf, D_MODEL), lambda i, e, f: (e, f, 0))],
        out_specs=pl.BlockSpec((tm, D_MODEL), lambda i, e, f: (i, 0)),
        out_shape=jax.ShapeDtypeStruct(h.shape, F32),
        scratch_shapes=[pltpu.VMEM((tm, D_MODEL), BF16), pltpu.VMEM((tm, LANES), F32),
                        pltpu.VMEM((tm, D_MODEL), F32)],
        compiler_params=_cparams(("parallel", "arbitrary", "arbitrary")),
        name="expert_swiglu",
    )(h, g, wr, w1, w3, w2)


def _mix_in_weights(w):
    offs = np.concatenate([[0], np.cumsum(MIX_IN_SIZES)])
    cq, ckv, kr, dq, dk, dv, iq, ik, iw = [w[:, offs[j]:offs[j + 1]] for j in range(9)]
    z = lambda n: jnp.zeros((w.shape[0], n), w.dtype)
    half = MLA_ROPE // 2
    kr_main = jnp.concatenate([z(MLA_NOPE), kr, z(LANES - MLA_QK)], axis=1)
    kr_swap = jnp.concatenate([z(MLA_NOPE), kr[:, half:], kr[:, :half], z(LANES - MLA_QK)], axis=1)
    ik4 = jnp.concatenate([ik] * (LANES // IDX_DIM), axis=1)
    iw_p = jnp.concatenate([iw, z(LANES - IDX_HEADS)], axis=1)
    return jnp.concatenate([cq, ckv, kr_main, kr_swap, dq, dk, dv, iq, ik4, iw_p],
                           axis=1).astype(BF16)


def _mla_q_weights(w_uq):
    r = w_uq.shape[0]
    w = w_uq.reshape(r, MLA_HEADS, MLA_QK)
    nope, rope = w[..., :MLA_NOPE], w[..., MLA_NOPE:]
    half = MLA_ROPE // 2
    z = lambda n: jnp.zeros((r, MLA_HEADS, n), w.dtype)
    main = jnp.concatenate([nope, rope, z(LANES - MLA_QK)], axis=-1)
    swap = jnp.concatenate([z(MLA_NOPE), rope[..., half:], rope[..., :half], z(LANES - MLA_QK)],
                           axis=-1)
    return (main.reshape(r, MLA_HEADS * LANES).astype(BF16),
            swap.reshape(r, MLA_HEADS * LANES).astype(BF16))


def _mla_kv_weights(w_ukv):
    r = w_ukv.shape[0]
    w = w_ukv.reshape(r, MLA_HEADS, MLA_NOPE + MLA_V)
    k_nope = jnp.concatenate([w[..., :MLA_NOPE], jnp.zeros((r, MLA_HEADS, LANES - MLA_NOPE), w.dtype)],
                             axis=-1)
    return (k_nope.reshape(r, MLA_HEADS * LANES).astype(BF16),
            w[..., MLA_NOPE:].reshape(r, MLA_HEADS * MLA_V).astype(BF16))


def _qk_gains(g):
    half = MLA_ROPE // 2
    z = lambda n: jnp.zeros((n,), g.dtype)
    main = jnp.concatenate([g, z(LANES - MLA_QK)])
    swap = jnp.concatenate([z(MLA_NOPE), g[MLA_NOPE + half:], g[MLA_NOPE:MLA_NOPE + half],
                            z(LANES - MLA_QK)])
    return main[None, :], swap[None, :]


def _rope_tables(lp):
    half = MLA_ROPE // 2
    inv = ROPE_BASE ** (-jnp.arange(half, dtype=F32) / half)
    ang = jnp.arange(lp, dtype=jnp.int32).astype(F32)[:, None] * inv[None, :]
    cos, sin = jnp.cos(ang), jnp.sin(ang)
    ones = jnp.ones((lp, MLA_NOPE), F32)
    pad1 = jnp.ones((lp, LANES - MLA_QK), F32)
    zeros = jnp.zeros((lp, MLA_NOPE), F32)
    pad0 = jnp.zeros((lp, LANES - MLA_QK), F32)
    return (jnp.concatenate([ones, cos, cos, pad1], axis=1),
            jnp.concatenate([zeros, -sin, sin, pad0], axis=1))


def _rel_buckets(n):
    max_exact = REL_BUCKETS // 2
    d = np.arange(n)
    df = np.maximum(d, 1).astype(np.float32)
    large = max_exact + (np.log(df / np.float32(max_exact))
                         / np.float32(math.log(REL_MAX_DIST / max_exact))
                         * np.float32(REL_BUCKETS - max_exact)).astype(np.int32)
    large = np.minimum(large, REL_BUCKETS - 1)
    return np.where(d < max_exact, d, large)


def _bias_tiles(rel_bias):
    buckets = _rel_buckets(2 * LANES)
    assert np.all(buckets[LANES - 1:] == REL_BUCKETS - 1)
    dist = np.arange(LANES)[:, None] - np.arange(2 * LANES)[None, :] + LANES
    bk = buckets[np.clip(dist, 0, 2 * LANES - 1)]
    shifted = (rel_bias - rel_bias[REL_BUCKETS - 1:REL_BUCKETS, :]) * LOG2E
    tz = jnp.transpose(shifted[bk], (2, 0, 1))
    return jnp.where(jnp.asarray(dist >= 0)[None], tz, 0.0).astype(F32)


def kernel(x, meta_tokens, rel_bias, ev_norm_mix, ev_w_mix_in, ev_g_q_lat, ev_g_kv_lat, ev_w_uq,
           ev_w_ukv, ev_mla_q_norm, ev_mla_k_norm, ev_dsa_q_norm, ev_dsa_k_norm, ev_w_mix_out,
           ev_norm_ffn, ev_w1, ev_w3, ev_w2, od_norm_mix, od_w_in, od_conv_w, od_w_out,
           od_norm_ffn, od_w_router, od_w1, od_w3, od_w2):
    b, seq, d = x.shape
    assert d == D_MODEL
    l_tot = seq + N_META
    lp = -(-l_tot // BLOCK_Q) * BLOCK_Q
    assert lp % SEQ_TILE == 0, "sequence tiling assumes the padded length is a multiple of 384"
    top_k = min(DSA_TOPK_MAX, l_tot // 4)
    depth = ev_norm_mix.shape[0] + od_norm_mix.shape[0]

    meta = jnp.broadcast_to(meta_tokens[None].astype(x.dtype), (b, N_META, d))
    h = jnp.concatenate([meta, x, jnp.zeros((b, lp - l_tot, d), x.dtype)], axis=1)
    h = h.reshape(b * lp, d)

    cos_t, sin_t = _rope_tables(lp)
    tz = _bias_tiles(rel_bias)
    row2 = lambda v: v[None, :]

    for layer in range(depth):
        i = layer // 2
        if layer % 2 == 0:
            wqm, wqs = _mla_q_weights(ev_w_uq[i])
            wkk, wkv = _mla_kv_weights(ev_w_ukv[i])
            gqm, gqs = _qk_gains(ev_mla_q_norm[i])
            gkm, gks = _qk_gains(ev_mla_k_norm[i])
            gdq = row2(jnp.concatenate([ev_dsa_q_norm[i]] * 2))
            gdk = row2(jnp.concatenate([ev_dsa_k_norm[i]] * 2))
            qm, km, vm, qd, kd, vd, iq, ik, iw = _prep_call(
                h, row2(ev_norm_mix[i]), _mix_in_weights(ev_w_mix_in[i]), row2(ev_g_q_lat[i]),
                row2(ev_g_kv_lat[i]), wqm, wqs, wkk, wkv, gqm, gqs, gkm, gks, gdq, gdk,
                cos_t, sin_t, lp)
            seq3 = lambda a: a.reshape(b, lp, a.shape[1])
            o_mla = _mla_call(seq3(qm), seq3(km), seq3(vm))
            o_dsa = _dsa_call(seq3(iq), seq3(iw), seq3(ik), seq3(qd), seq3(kd), seq3(vd), tz, top_k)
            w_o = ev_w_mix_out[i].astype(BF16)
            n_mla = MLA_HEADS * MLA_V
            h = _mix_out_call(h, o_mla.reshape(b * lp, -1), o_dsa.reshape(b * lp, -1),
                              w_o[:n_mla], w_o[n_mla:])
            h = _ffn_call(h, row2(ev_norm_ffn[i]), ev_w1[i].astype(BF16), ev_w3[i].astype(BF16),
                          ev_w2[i].astype(BF16))
        else:
            h = _conv_call(h.reshape(b, lp, d), row2(od_norm_mix[i]), od_w_in[i].astype(BF16),
                           od_conv_w[i].reshape(CONV_WIDTH, d), od_w_out[i].astype(BF16))
            h = h.reshape(b * lp, d)
            wr = jnp.concatenate(
                [od_w_router[i], jnp.zeros((d, LANES - N_EXPERTS), od_w_router.dtype)], axis=1)
            h = _moe_call(h, row2(od_norm_ffn[i]), wr.astype(BF16), od_w1[i].astype(BF16),
                          od_w3[i].astype(BF16), od_w2[i].astype(BF16))
    return h.reshape(b, lp, d)[:, N_META:l_tot]
```

```python
import functools
import math

import numpy as np
import jax
import jax.numpy as jnp
from jax import lax
from jax.experimental import pallas as pl
from jax.experimental.pallas import tpu as pltpu

F32 = jnp.float32
BF16 = jnp.bfloat16
I32 = jnp.int32

D_MODEL = 1024
N_META = 16
BLOCK_Q = 128
EPS = 1e-6
MLA_HEADS = 8
MLA_Q_RANK = 384
MLA_KV_RANK = 256
MLA_NOPE = 64
MLA_ROPE = 32
MLA_V = 64
MLA_QK = MLA_NOPE + MLA_ROPE
MLA_SCALE = MLA_QK ** -0.5
ROPE_BASE = 10000.0
DSA_HEADS = 8
DSA_HEAD_DIM = 64
DSA_WIDTH = DSA_HEADS * DSA_HEAD_DIM
DSA_SCALE = DSA_HEAD_DIM ** -0.5
IDX_HEADS = 8
IDX_DIM = 32
DSA_TOPK_MAX = 256
REL_BUCKETS = 32
REL_MAX_DIST = 128
MIX_IN_SIZES = (MLA_Q_RANK, MLA_KV_RANK, MLA_ROPE, DSA_WIDTH, DSA_WIDTH, DSA_WIDTH,
                IDX_HEADS * IDX_DIM, IDX_DIM, IDX_HEADS)
CONV_WIDTH = 3
N_EXPERTS = 8
TOP_K = 2

LANES = 128
VMEM_LIMIT_BYTES = 56 * 1024 * 1024

SEQ_TILE = 3 * LANES
MOE_TILE = 512
XG_WIDTH = D_MODEL + LANES
LOG2E = math.log2(math.e)
NEG_BIG = -1e30
INT_MIN = -2 ** 31

_C_CQ = 0
_C_CKV = _C_CQ + MLA_Q_RANK
_C_KRM = _C_CKV + MLA_KV_RANK
_C_KRS = _C_KRM + LANES
_C_DQ = _C_KRS + LANES
_C_DK = _C_DQ + DSA_WIDTH
_C_DV = _C_DK + DSA_WIDTH
_C_IQ = _C_DV + DSA_WIDTH
_C_IK = _C_IQ + IDX_HEADS * IDX_DIM
_C_IW = _C_IK + LANES
_C_END = _C_IW + LANES


def _cparams(sem):
    return pltpu.CompilerParams(dimension_semantics=sem, vmem_limit_bytes=VMEM_LIMIT_BYTES)


def _row_tile(n_rows, candidates=(1024, 768, 512, 384, 256, 128)):
    for c in candidates:
        if n_rows % c == 0:
            return c
    raise ValueError(f"no row tile for {n_rows}")


def _rms(x, g):
    ms = jnp.mean(x * x, axis=-1, keepdims=True)
    return x * lax.rsqrt(ms + EPS) * g


def _dot(a, b):
    return jnp.dot(a, b, preferred_element_type=F32)


def _dot_nt(a, b):
    return lax.dot_general(a, b, (((1,), (1,)), ((), ())), preferred_element_type=F32)


def _prep_kernel(h_ref, g_ref, wext_ref, gql_ref, gkvl_ref, wqm_ref, wqs_ref, wkk_ref, wkv_ref,
                 gqm_ref, gqs_ref, gkm_ref, gks_ref, gdq_ref, gdk_ref, cos_ref, sin_ref,
                 qm_o, km_o, vm_o, qd_o, kd_o, vd_o, iq_o, ik_o, iw_o):
    xn = _rms(h_ref[...], g_ref[...]).astype(BF16)

    def proj(lo, hi):
        return _dot(xn, wext_ref[:, lo:hi])

    cos = cos_ref[...]
    sin = sin_ref[...]
    lane = lax.broadcasted_iota(I32, (xn.shape[0], LANES), 1)

    cqn = _rms(proj(_C_CQ, _C_CKV), gql_ref[...]).astype(BF16)
    q_main = _dot(cqn, wqm_ref[...])
    q_swap = _dot(cqn, wqs_ref[...])
    for hd in range(MLA_HEADS):
        sl = slice(hd * LANES, (hd + 1) * LANES)
        a = q_main[:, sl]
        r = lax.rsqrt(jnp.sum(a * a, axis=-1, keepdims=True) * (1.0 / MLA_QK) + EPS)
        out = (a * r * gqm_ref[...]) * cos + (q_swap[:, sl] * r * gqs_ref[...]) * sin
        qm_o[:, sl] = out.astype(BF16)

    ckvn = _rms(proj(_C_CKV, _C_KRM), gkvl_ref[...]).astype(BF16)
    k_nope = _dot(ckvn, wkk_ref[...])
    vm_o[...] = _dot(ckvn, wkv_ref[...]).astype(BF16)
    kr_main = proj(_C_KRM, _C_KRS)
    kr_swap = proj(_C_KRS, _C_DQ)
    for hd in range(MLA_HEADS):
        sl = slice(hd * LANES, (hd + 1) * LANES)
        a = k_nope[:, sl] + kr_main
        r = lax.rsqrt(jnp.sum(a * a, axis=-1, keepdims=True) * (1.0 / MLA_QK) + EPS)
        out = (a * r * gkm_ref[...]) * cos + (kr_swap * r * gks_ref[...]) * sin
        km_o[:, sl] = out.astype(BF16)

    first = lane < DSA_HEAD_DIM
    for (lo, g2_ref, o_ref) in ((_C_DQ, gdq_ref, qd_o), (_C_DK, gdk_ref, kd_o)):
        for pr in range(DSA_HEADS // 2):
            x = proj(lo + pr * LANES, lo + (pr + 1) * LANES)
            sq = x * x
            s0 = jnp.sum(jnp.where(first, sq, 0.0), axis=-1, keepdims=True)
            s1 = jnp.sum(jnp.where(first, 0.0, sq), axis=-1, keepdims=True)
            r0 = lax.rsqrt(s0 * (1.0 / DSA_HEAD_DIM) + EPS)
            r1 = lax.rsqrt(s1 * (1.0 / DSA_HEAD_DIM) + EPS)
            out = x * jnp.where(first, r0, r1) * g2_ref[...]
            o_ref[:, pr * LANES:(pr + 1) * LANES] = out.astype(BF16)

    vd_o[...] = proj(_C_DV, _C_IQ).astype(BF16)
    iq_o[...] = proj(_C_IQ, _C_IK).astype(BF16)
    ik_o[...] = proj(_C_IK, _C_IW).astype(BF16)
    iw_o[...] = proj(_C_IW, _C_END)


def _prep_call(h, g_mix, wext, gql, gkvl, wqm, wqs, wkk, wkv, gqm, gqs, gkm, gks, gdq, gdk,
               cos_t, sin_t, lp):
    t = h.shape[0]
    tm = SEQ_TILE
    nt = lp // tm
    row = lambda w: pl.BlockSpec((tm, w), lambda i: (i, 0))
    full = lambda a: pl.BlockSpec(a.shape, lambda i: (0, 0))
    tab = pl.BlockSpec((tm, LANES), lambda i: (i % nt, 0))
    hw = MLA_HEADS * LANES
    out_shape = [
        jax.ShapeDtypeStruct((t, hw), BF16), jax.ShapeDtypeStruct((t, hw), BF16),
        jax.ShapeDtypeStruct((t, MLA_HEADS * MLA_V), BF16),
        jax.ShapeDtypeStruct((t, DSA_WIDTH), BF16), jax.ShapeDtypeStruct((t, DSA_WIDTH), BF16),
        jax.ShapeDtypeStruct((t, DSA_WIDTH), BF16),
        jax.ShapeDtypeStruct((t, IDX_HEADS * IDX_DIM), BF16),
        jax.ShapeDtypeStruct((t, LANES), BF16), jax.ShapeDtypeStruct((t, LANES), F32),
    ]
    return pl.pallas_call(
        _prep_kernel,
        grid=(t // tm,),
        in_specs=[row(D_MODEL), full(g_mix), full(wext), full(gql), full(gkvl), full(wqm),
                  full(wqs), full(wkk), full(wkv), full(gqm), full(gqs), full(gkm), full(gks),
                  full(gdq), full(gdk), tab, tab],
        out_specs=[row(s.shape[1]) for s in out_shape],
        out_shape=out_shape,
        compiler_params=_cparams(("parallel",)),
        name="prep_mix_in",
    )(h, g_mix, wext, gql, gkvl, wqm, wqs, wkk, wkv, gqm, gqs, gkm, gks, gdq, gdk, cos_t, sin_t)


def _softmax_pv(s_scr, m_scr, l_scr, acc_scr, v_ref, v_lo, n_chunks):
    tq = s_scr.shape[1]
    tk = s_scr.shape[2]

    def fold(x, op):
        out = x[:, 0:LANES]
        for j in range(1, tk // LANES):
            out = op(out, x[:, j * LANES:(j + 1) * LANES])
        return out

    m_scr[...] = jnp.full(m_scr.shape, -jnp.inf, F32)

    def max_body(c, carry):
        m_scr[...] = jnp.maximum(m_scr[...], fold(s_scr[c], jnp.maximum))
        return carry

    lax.fori_loop(0, n_chunks, max_body, 0)
    m_row = jnp.max(m_scr[...], axis=-1, keepdims=True)

    l_scr[...] = jnp.zeros(l_scr.shape, F32)
    acc_scr[...] = jnp.zeros(acc_scr.shape, F32)

    def pv_body(c, carry):
        p = jnp.exp2(s_scr[c] - m_row)
        l_scr[...] += fold(p, jnp.add)
        vc = v_ref[pl.ds(pl.multiple_of(c * tk, tk), tk), v_lo:v_lo + LANES]
        acc_scr[...] += _dot(p.astype(BF16), vc)
        return carry

    lax.fori_loop(0, n_chunks, pv_body, 0)
    l_row = jnp.sum(l_scr[...], axis=-1, keepdims=True)
    return acc_scr[...] / l_row


def _mla_kernel(q_ref, k_ref, v_ref, o_ref, s_scr, m_scr, l_scr, acc_scr):
    qi = pl.program_id(1)
    tq = q_ref.shape[0]
    c2 = MLA_SCALE * LOG2E
    row = lax.broadcasted_iota(I32, (tq, tq), 0)
    col = lax.broadcasted_iota(I32, (tq, tq), 1)
    lane = lax.broadcasted_iota(I32, (tq, LANES), 1)

    pair_out = None
    for hd in range(MLA_HEADS):
        sl = slice(hd * LANES, (hd + 1) * LANES)
        qh = q_ref[:, sl]

        def score_body(c, carry):
            kc = k_ref[pl.ds(pl.multiple_of(c * tq, tq), tq), sl]
            s_scr[c] = _dot_nt(qh, kc) * c2
            return carry

        lax.fori_loop(0, qi, score_body, 0)
        kc = k_ref[pl.ds(pl.multiple_of(qi * tq, tq), tq), sl]
        s_scr[qi] = jnp.where(col <= row, _dot_nt(qh, kc) * c2, NEG_BIG)

        pr = hd // 2
        o = _softmax_pv(s_scr, m_scr, l_scr, acc_scr, v_ref, pr * LANES, qi + 1)
        if hd % 2 == 0:
            pair_out = o
        else:
            o_ref[:, pr * LANES:(pr + 1) * LANES] = jnp.where(
                lane < MLA_V, pair_out, o).astype(o_ref.dtype)


def _mla_call(q, k, v):
    b, lp, _ = q.shape
    tq = SEQ_TILE
    nq = lp // tq
    return pl.pallas_call(
        _mla_kernel,
        grid=(b, nq),
        in_specs=[pl.BlockSpec((None, tq, q.shape[2]), lambda bi, i: (bi, i, 0)),
                  pl.BlockSpec((None, lp, k.shape[2]), lambda bi, i: (bi, 0, 0)),
                  pl.BlockSpec((None, lp, v.shape[2]), lambda bi, i: (bi, 0, 0))],
        out_specs=pl.BlockSpec((None, tq, v.shape[2]), lambda bi, i: (bi, i, 0)),
        out_shape=jax.ShapeDtypeStruct((b, lp, v.shape[2]), BF16),
        scratch_shapes=[pltpu.VMEM((nq, tq, tq), F32), pltpu.VMEM((tq, LANES), F32),
                        pltpu.VMEM((tq, LANES), F32), pltpu.VMEM((tq, LANES), F32)],
        compiler_params=_cparams(("parallel", "arbitrary")),
        name="mla_attention",
    )(q, k, v)


def _dsa_kernel(iq_ref, iw_ref, ik_ref, q_ref, k_ref, v_ref, tz_ref, o_ref,
                key_scr, madd_scr, s_scr, wb_scr, ans_scr, jst_scr, m_scr, l_scr, acc_scr,
                *, top_k):
    qi = pl.program_id(1)
    tq = q_ref.shape[0]
    nsub = tq // LANES
    n_chunks = qi + 1
    row = lax.broadcasted_iota(I32, (tq, tq), 0)
    col = lax.broadcasted_iota(I32, (tq, tq), 1)
    lane = lax.broadcasted_iota(I32, (tq, LANES), 1)

    iw = iw_ref[...]
    lane_group = lax.shift_right_logical(lane, int(math.log2(IDX_DIM)))
    q_heads = []
    for hd in range(IDX_HEADS):
        wb_scr[hd] = jnp.broadcast_to(iw[:, hd:hd + 1], (tq, LANES))
        quad = iq_ref[:, (hd // 4) * LANES:(hd // 4 + 1) * LANES].astype(F32)
        q_heads.append(jnp.where(lane_group == hd % 4, quad, 0.0).astype(BF16))

    def index_chunk(c, causal_mask):
        ikc = ik_ref[pl.ds(pl.multiple_of(c * tq, tq), tq), :]
        sc = jnp.zeros((tq, tq), F32)
        for hd in range(IDX_HEADS):
            act = jnp.maximum(_dot_nt(q_heads[hd], ikc), 0.0)
            wb = wb_scr[hd]
            sc = sc + jnp.concatenate([wb] * nsub, axis=1) * act
        sc = jnp.where(sc == 0.0, 0.0, sc)
        if causal_mask:
            sc = jnp.where(col <= row, sc, -jnp.inf)
        bits = pltpu.bitcast(sc, I32)
        key_scr[c] = bits ^ (lax.shift_right_arithmetic(bits, 31) & 0x7FFFFFFF)

    def index_body(c, carry):
        index_chunk(c, False)
        return carry

    lax.fori_loop(0, qi, index_body, 0)
    index_chunk(qi, True)

    kf = float(top_k)
    for rb in range(nsub):
        rs = slice(rb * LANES, (rb + 1) * LANES)
        t_pos = qi * tq + rb * LANES + lax.broadcasted_iota(I32, (LANES, LANES), 0)

        def count(pred_fn):
            def body(c, acc):
                kk = key_scr[c, rs, :]
                idx0 = c * tq
                for j in range(nsub):
                    acc = acc + jnp.where(
                        pred_fn(kk[:, j * LANES:(j + 1) * LANES], idx0 + j * LANES), 1.0, 0.0)
                return acc
            acc = lax.fori_loop(0, n_chunks, body, jnp.zeros((LANES, LANES), F32))
            return jnp.broadcast_to(jnp.sum(acc, axis=-1, keepdims=True), (LANES, LANES))

        done0 = (t_pos + 1 <= top_k).astype(I32)
        ans0 = jnp.full((LANES, LANES), INT_MIN, I32)

        def bis_cond(st):
            bit, _, _, n_open = st
            return jnp.logical_and(bit >= 0, n_open > 0.0)

        def bis_body(st):
            bit, ans, done, _ = st
            cand = ans + lax.shift_left(jnp.int32(1), bit)
            cnt = count(lambda kk, _i: kk >= cand)
            open_ = done == 0
            ans = jnp.where(jnp.logical_and(open_, cnt >= kf), cand, ans)
            done = jnp.where(jnp.logical_and(open_, cnt == kf), 1, done)
            n_open = jnp.sum(jnp.where(done == 0, 1.0, 0.0))
            return bit - 1, ans, done, n_open

        n_open0 = jnp.sum(jnp.where(done0 == 0, 1.0, 0.0))
        _, ans, done, n_open = lax.while_loop(bis_cond, bis_body,
                                              (jnp.int32(31), ans0, done0, n_open0))
        ans_scr[rs, :] = ans
        jst_scr[rs, :] = jnp.full((LANES, LANES), 2 ** 31 - 1, I32)

        @pl.when(n_open > 0.0)
        def _():
            lane_b = lax.broadcasted_iota(I32, (LANES, LANES), 1)
            need = kf - count(lambda kk, _i: kk > ans)

            def tie_body(i, jst):
                cand = jst + lax.shift_left(jnp.int32(1), 13 - i)
                cnt = count(lambda kk, i0: jnp.logical_and(kk == ans, lane_b + i0 < cand))
                return jnp.where(cnt < need, cand, jst)

            jst = lax.fori_loop(0, 14, tie_body, jnp.zeros((LANES, LANES), I32))
            jst_scr[rs, :] = jnp.where(done == 0, jst, 2 ** 31 - 1)

    ans_all = ans_scr[...]
    jst_all = jst_scr[...]
    ans_w = jnp.concatenate([ans_all] * nsub, axis=1)
    jst_w = jnp.concatenate([jst_all] * nsub, axis=1)

    def mask_chunk(c, causal_mask):
        kk = key_scr[c]
        sel = jnp.logical_or(kk > ans_w,
                             jnp.logical_and(kk == ans_w, col + c * tq <= jst_w))
        if causal_mask:
            sel = jnp.logical_and(sel, col <= row)
        madd_scr[c] = jnp.where(sel, 0.0, NEG_BIG)

    def mask_body(c, carry):
        mask_chunk(c, False)
        return carry

    lax.fori_loop(0, qi, mask_body, 0)
    mask_chunk(qi, True)

    c2 = DSA_SCALE * LOG2E
    pair_out = None
    for hd in range(DSA_HEADS):
        pr = hd // 2
        sl = slice(pr * LANES, (pr + 1) * LANES)
        qp = q_ref[:, sl].astype(F32)
        own = lane < DSA_HEAD_DIM if hd % 2 == 0 else lane >= DSA_HEAD_DIM
        qh = jnp.where(own, qp, 0.0).astype(BF16)

        def score_body(c, carry):
            kc = k_ref[pl.ds(pl.multiple_of(c * tq, tq), tq), sl]
            s_scr[c] = _dot_nt(qh, kc) * c2 + madd_scr[c]
            return carry

        lax.fori_loop(0, n_chunks, score_body, 0)

        tz_near = tz_ref[hd, :, LANES:2 * LANES]
        tz_far = tz_ref[hd, :, 0:LANES]
        for a in range(nsub):
            ra = slice(a * LANES, (a + 1) * LANES)
            s_scr[qi, ra, ra] += tz_near
            if a >= 1:
                s_scr[qi, ra, (a - 1) * LANES:a * LANES] += tz_far

        @pl.when(qi >= 1)
        def _():
            s_scr[qi - 1, 0:LANES, (nsub - 1) * LANES:nsub * LANES] += tz_far

        o = _softmax_pv(s_scr, m_scr, l_scr, acc_scr, v_ref, pr * LANES, n_chunks)
        if hd % 2 == 0:
            pair_out = o
        else:
            o_ref[:, sl] = jnp.where(lane < DSA_HEAD_DIM, pair_out, o).astype(o_ref.dtype)


def _dsa_call(iq, iw, ik, q, k, v, tz, top_k):
    b, lp, _ = q.shape
    tq = SEQ_TILE
    nq = lp // tq
    qspec = lambda w: pl.BlockSpec((None, tq, w), lambda bi, i: (bi, i, 0))
    kspec = lambda w: pl.BlockSpec((None, lp, w), lambda bi, i: (bi, 0, 0))
    return pl.pallas_call(
        functools.partial(_dsa_kernel, top_k=top_k),
        grid=(b, nq),
        in_specs=[qspec(iq.shape[2]), qspec(LANES), kspec(LANES), qspec(DSA_WIDTH),
                  kspec(DSA_WIDTH), kspec(DSA_WIDTH),
                  pl.BlockSpec(tz.shape, lambda bi, i: (0, 0, 0))],
        out_specs=qspec(DSA_WIDTH),
        out_shape=jax.ShapeDtypeStruct((b, lp, DSA_WIDTH), BF16),
        scratch_shapes=[pltpu.VMEM((nq, tq, tq), I32), pltpu.VMEM((nq, tq, tq), F32),
                        pltpu.VMEM((nq, tq, tq), F32), pltpu.VMEM((IDX_HEADS, tq, LANES), F32),
                        pltpu.VMEM((tq, LANES), I32), pltpu.VMEM((tq, LANES), I32),
                        pltpu.VMEM((tq, LANES), F32), pltpu.VMEM((tq, LANES), F32),
                        pltpu.VMEM((tq, LANES), F32)],
        compiler_params=_cparams(("parallel", "arbitrary")),
        name="dsa_attention",
    )(iq, iw, ik, q, k, v, tz)


def _mix_out_kernel(h_ref, a_ref, b_ref, wa_ref, wb_ref, o_ref):
    o_ref[...] = h_ref[...] + _dot(a_ref[...], wa_ref[...]) + _dot(b_ref[...], wb_ref[...])


def _mix_out_call(h, a, b, wa, wb):
    t = h.shape[0]
    tm = _row_tile(t)
    row = lambda w: pl.BlockSpec((tm, w), lambda i: (i, 0))
    full = lambda x: pl.BlockSpec(x.shape, lambda i: (0, 0))
    return pl.pallas_call(
        _mix_out_kernel,
        grid=(t // tm,),
        in_specs=[row(D_MODEL), row(a.shape[1]), row(b.shape[1]), full(wa), full(wb)],
        out_specs=row(D_MODEL),
        out_shape=jax.ShapeDtypeStruct(h.shape, F32),
        compiler_params=_cparams(("parallel",)),
        name="mix_out",
    )(h, a, b, wa, wb)


def _ffn_kernel(h_ref, g_ref, w1_ref, w3_ref, w2_ref, o_ref, xn_scr, acc_scr):
    f = pl.program_id(1)

    @pl.when(f == 0)
    def _():
        xn_scr[...] = _rms(h_ref[...], g_ref[...]).astype(BF16)
        acc_scr[...] = jnp.zeros(acc_scr.shape, F32)

    xn = xn_scr[...]
    a = _dot(xn, w1_ref[...])
    act = (a * jax.nn.sigmoid(a)) * _dot(xn, w3_ref[...])
    acc_scr[...] += _dot(act.astype(BF16), w2_ref[...])

    @pl.when(f == pl.num_programs(1) - 1)
    def _():
        o_ref[...] = h_ref[...] + acc_scr[...]


def _ffn_call(h, g, w1, w3, w2):
    t = h.shape[0]
    d_ff = w1.shape[1]
    tm = _row_tile(t)
    tf = 512 if d_ff % 512 == 0 else d_ff
    return pl.pallas_call(
        _ffn_kernel,
        grid=(t // tm, d_ff // tf),
        in_specs=[pl.BlockSpec((tm, D_MODEL), lambda i, f: (i, 0)),
                  pl.BlockSpec(g.shape, lambda i, f: (0, 0)),
                  pl.BlockSpec((D_MODEL, tf), lambda i, f: (0, f)),
                  pl.BlockSpec((D_MODEL, tf), lambda i, f: (0, f)),
                  pl.BlockSpec((tf, D_MODEL), lambda i, f: (f, 0))],
        out_specs=pl.BlockSpec((tm, D_MODEL), lambda i, f: (i, 0)),
        out_shape=jax.ShapeDtypeStruct(h.shape, F32),
        scratch_shapes=[pltpu.VMEM((tm, D_MODEL), BF16), pltpu.VMEM((tm, D_MODEL), F32)],
        compiler_params=_cparams(("parallel", "arbitrary")),
        name="dense_swiglu",
    )(h, g, w1, w3, w2)


def _conv_kernel(h_ref, g_ref, win_ref, cw_ref, wout_ref, o_ref, z_scr):
    i = pl.program_id(1)
    tm = h_ref.shape[0]
    halo = 8
    x = h_ref[...]
    xn = _rms(x, g_ref[...]).astype(BF16)
    c_gate = _dot(xn, win_ref[:, D_MODEL:2 * D_MODEL])
    u = _dot(xn, win_ref[:, 2 * D_MODEL:3 * D_MODEL])
    z = c_gate * u

    @pl.when(i == 0)
    def _():
        z_scr[0:halo, :] = jnp.zeros((halo, D_MODEL), F32)

    @pl.when(i > 0)
    def _():
        z_scr[0:halo, :] = z_scr[tm:tm + halo, :]

    z_scr[halo:halo + tm, :] = z
    cw = cw_ref[...]
    y = (cw[0:1, :] * z_scr[halo - 2:halo - 2 + tm, :]
         + cw[1:2, :] * z_scr[halo - 1:halo - 1 + tm, :]
         + cw[2:3, :] * z)
    b_gate = _dot(xn, win_ref[:, 0:D_MODEL])
    o_ref[...] = x + _dot((b_gate * y).astype(BF16), wout_ref[...])


def _conv_call(h3, g, w_in, cw, w_out):
    b, lp, d = h3.shape
    tm = SEQ_TILE
    full = lambda a: pl.BlockSpec(a.shape, lambda bi, i: (0,) * a.ndim)
    return pl.pallas_call(
        _conv_kernel,
        grid=(b, lp // tm),
        in_specs=[pl.BlockSpec((None, tm, d), lambda bi, i: (bi, i, 0)), full(g), full(w_in),
                  full(cw), full(w_out)],
        out_specs=pl.BlockSpec((None, tm, d), lambda bi, i: (bi, i, 0)),
        out_shape=jax.ShapeDtypeStruct(h3.shape, F32),
        scratch_shapes=[pltpu.VMEM((tm + 8, d), F32)],
        compiler_params=_cparams(("arbitrary", "arbitrary")),
        name="short_conv_mixer",
    )(h3, g, w_in, cw, w_out)


def _route_kernel(h_ref, g_ref, wr_ref, xg_ref, didx_ref, cnt_ref,
                  xs_scr, base_scr, dvm_scr, dsm_scr, sem_idx, sem_rows, *, stride):
    i = pl.program_id(0)
    tm = h_ref.shape[0]
    lane = lax.broadcasted_iota(I32, (tm, LANES), 1)
    lane_f = lane.astype(F32)

    @pl.when(i == 0)
    def _():
        base_scr[...] = jnp.zeros(base_scr.shape, F32)

    xnf = _rms(h_ref[...], g_ref[...])
    logits = jnp.where(lane < N_EXPERTS, _dot(xnf.astype(BF16), wr_ref[...]), -jnp.inf)
    v1 = jnp.max(logits, axis=-1, keepdims=True)
    i1 = jnp.min(jnp.where(logits == v1, lane_f, float(LANES)), axis=-1, keepdims=True)
    rest = jnp.where(lane_f == i1, -jnp.inf, logits)
    v2 = jnp.max(rest, axis=-1, keepdims=True)
    i2 = jnp.min(jnp.where(rest == v2, lane_f, float(LANES)), axis=-1, keepdims=True)
    e2 = jnp.exp(v2 - v1)
    den = 1.0 + e2
    gates = (1.0 / den, e2 / den)

    oh1 = jnp.where(lane_f == i1, 1.0, 0.0)
    oh2 = jnp.where(lane_f == i2, 1.0, 0.0)
    oh = oh1 + oh2
    earlier = (lax.broadcasted_iota(I32, (tm, tm), 1) < lax.broadcasted_iota(I32, (tm, tm), 0))
    prefix = _dot(jnp.where(earlier, 1.0, 0.0).astype(BF16), oh.astype(BF16))
    pos = base_scr[0:1, :] + prefix
    d1 = jnp.sum(oh1 * pos, axis=-1, keepdims=True) + i1 * float(stride)
    d2 = jnp.sum(oh2 * pos, axis=-1, keepdims=True) + i2 * float(stride)
    base_scr[...] = base_scr[...] + jnp.sum(oh, axis=0, keepdims=True)
    cnt_ref[...] = base_scr[...]

    for s in range(TOP_K):
        xs_scr[s, :, 0:D_MODEL] = xnf
        xs_scr[s, :, D_MODEL:] = jnp.broadcast_to(gates[s], (tm, LANES))

    dmat = jnp.where(lane == 0, d1, jnp.where(lane == 1, d2, 0.0))
    dvm_scr[...] = jnp.transpose(dmat)[0:8, :].astype(I32)
    didx_ref[...] = dvm_scr[...]
    cp = pltpu.make_async_copy(dvm_scr, dsm_scr, sem_idx)
    cp.start()
    cp.wait()

    def issue(r, carry):
        for s in range(TOP_K):
            pltpu.make_async_copy(xs_scr.at[s, pl.ds(r, 1)], xg_ref.at[pl.ds(dsm_scr[s, r], 1)],
                                  sem_rows).start()
        return carry

    lax.fori_loop(0, tm, issue, 0, unroll=8)
    for s in range(TOP_K):
        pltpu.make_async_copy(xs_scr.at[s], xg_ref.at[pl.ds(0, tm)], sem_rows).wait()

    @pl.when(i == pl.num_programs(0) - 1)
    def _():
        xs_scr[0] = jnp.zeros(xs_scr.shape[1:], F32)
        dvm_scr[:, 0:LANES] = base_scr[...].astype(I32)
        cp2 = pltpu.make_async_copy(dvm_scr, dsm_scr, sem_idx)
        cp2.start()
        cp2.wait()
        sublanes = 8
        for phase in ("start", "wait"):
            for e in range(N_EXPERTS):
                n_e = dsm_scr[0, e]
                n_up = lax.shift_left(lax.shift_right_logical(n_e + sublanes - 1, 3), 3)
                for k in range(sublanes - 1):
                    @pl.when(n_e + k < n_up)
                    def _():
                        row = pltpu.make_async_copy(
                            xs_scr.at[0, pl.ds(0, 1)], xg_ref.at[pl.ds(e * stride + n_e + k, 1)],
                            sem_rows)
                        row.start() if phase == "start" else row.wait()
                blk = pltpu.make_async_copy(
                    xs_scr.at[0], xg_ref.at[pl.ds(pl.multiple_of(e * stride + n_up, sublanes), tm)],
                    sem_rows)
                blk.start() if phase == "start" else blk.wait()


def _route_call(h, g, wr, stride):
    t = h.shape[0]
    tm = MOE_TILE
    return pl.pallas_call(
        functools.partial(_route_kernel, stride=stride),
        grid=(t // tm,),
        in_specs=[pl.BlockSpec((tm, D_MODEL), lambda i: (i, 0)),
                  pl.BlockSpec(g.shape, lambda i: (0, 0)),
                  pl.BlockSpec(wr.shape, lambda i: (0, 0))],
        out_specs=[pl.BlockSpec(memory_space=pl.ANY),
                   pl.BlockSpec((8, tm), lambda i: (0, i)),
                   pl.BlockSpec((8, LANES), lambda i: (0, 0))],
        out_shape=[jax.ShapeDtypeStruct((N_EXPERTS * stride, XG_WIDTH), F32),
                   jax.ShapeDtypeStruct((8, t), I32),
                   jax.ShapeDtypeStruct((8, LANES), F32)],
        scratch_shapes=[pltpu.VMEM((TOP_K, tm, XG_WIDTH), F32), pltpu.VMEM((8, LANES), F32),
                        pltpu.VMEM((8, tm), I32), pltpu.SMEM((8, tm), I32),
                        pltpu.SemaphoreType.DMA, pltpu.SemaphoreType.DMA],
        compiler_params=_cparams(("arbitrary",)),
        name="moe_route",
    )(h, g, wr)


def _expert_kernel(blk_ref, exp_ref, nu_ref, x_ref, w1_ref, w3_ref, w2_ref, o_ref, xb_scr, acc_scr):
    j = pl.program_id(0)
    f = pl.program_id(1)

    @pl.when(j < nu_ref[0])
    def _():
        @pl.when(f == 0)
        def _():
            xb_scr[...] = x_ref[:, 0:D_MODEL].astype(BF16)
            acc_scr[...] = jnp.zeros(acc_scr.shape, F32)

        xb = xb_scr[...]
        a = _dot(xb, w1_ref[...])
        act = (a * jax.nn.sigmoid(a)) * _dot(xb, w3_ref[...])
        acc_scr[...] += _dot(act.astype(BF16), w2_ref[...])

        @pl.when(f == pl.num_programs(1) - 1)
        def _():
            gate = x_ref[:, D_MODEL:]
            o_ref[...] = acc_scr[...] * jnp.concatenate([gate] * (D_MODEL // LANES), axis=1)


def _expert_call(tile_blk, tile_exp, n_used, xg, w1, w3, w2):
    d_ff = w1.shape[2]
    tm = MOE_TILE
    tf = 512 if d_ff % 512 == 0 else d_ff
    nf = d_ff // tf
    n_tiles = tile_blk.shape[0]
    fe = lambda j, f, nu: jnp.where(j < nu[0], f, nf - 1)
    grid_spec = pltpu.PrefetchScalarGridSpec(
        num_scalar_prefetch=3,
        grid=(n_tiles, nf),
        in_specs=[pl.BlockSpec((tm, XG_WIDTH), lambda j, f, blk, ex, nu: (blk[j], 0)),
                  pl.BlockSpec((None, D_MODEL, tf), lambda j, f, blk, ex, nu: (ex[j], 0, fe(j, f, nu))),
                  pl.BlockSpec((None, D_MODEL, tf), lambda j, f, blk, ex, nu: (ex[j], 0, fe(j, f, nu))),
                  pl.BlockSpec((None, tf, D_MODEL), lambda j, f, blk, ex, nu: (ex[j], fe(j, f, nu), 0))],
        out_specs=pl.BlockSpec((tm, D_MODEL), lambda j, f, blk, ex, nu: (blk[j], 0)),
        scratch_shapes=[pltpu.VMEM((tm, D_MODEL), BF16), pltpu.VMEM((tm, D_MODEL), F32)])
    return pl.pallas_call(
        _expert_kernel,
        grid_spec=grid_spec,
        out_shape=jax.ShapeDtypeStruct((xg.shape[0], D_MODEL), F32),
        compiler_params=_cparams(("arbitrary", "arbitrary")),
        name="expert_swiglu",
    )(tile_blk, tile_exp, n_used, xg, w1, w3, w2)


def _combine_kernel(h_ref, didx_ref, yg_ref, o_ref, y_scr, dsm_scr, sem_idx, sem_rows):
    i = pl.program_id(0)
    tm = h_ref.shape[0]
    cp = pltpu.make_async_copy(didx_ref.at[:, pl.ds(pl.multiple_of(i * tm, tm), tm)], dsm_scr, sem_idx)
    cp.start()
    cp.wait()

    def issue(r, carry):
        for s in range(TOP_K):
            pltpu.make_async_copy(yg_ref.at[pl.ds(dsm_scr[s, r], 1)], y_scr.at[s, pl.ds(r, 1)],
                                  sem_rows).start()
        return carry

    lax.fori_loop(0, tm, issue, 0, unroll=8)
    for s in range(TOP_K):
        pltpu.make_async_copy(yg_ref.at[pl.ds(0, tm)], y_scr.at[s], sem_rows).wait()
    o_ref[...] = h_ref[...] + y_scr[0] + y_scr[1]


def _combine_call(h, didx, yg):
    t = h.shape[0]
    tm = MOE_TILE
    return pl.pallas_call(
        _combine_kernel,
        grid=(t // tm,),
        in_specs=[pl.BlockSpec((tm, D_MODEL), lambda i: (i, 0)),
                  pl.BlockSpec(memory_space=pl.ANY), pl.BlockSpec(memory_space=pl.ANY)],
        out_specs=pl.BlockSpec((tm, D_MODEL), lambda i: (i, 0)),
        out_shape=jax.ShapeDtypeStruct(h.shape, F32),
        scratch_shapes=[pltpu.VMEM((TOP_K, tm, D_MODEL), F32), pltpu.SMEM((8, tm), I32),
                        pltpu.SemaphoreType.DMA, pltpu.SemaphoreType.DMA],
        compiler_params=_cparams(("arbitrary",)),
        name="moe_combine",
    )(h, didx, yg)


def _moe_call(h, g, wr, w1, w3, w2):
    t = h.shape[0]
    tm = MOE_TILE
    assert t % tm == 0
    stride = t + tm
    xg, didx, cnt = _route_call(h, g, wr, stride)
    counts = cnt[0, :N_EXPERTS].astype(I32)
    tiles_e = (counts + tm - 1) // tm
    cum = jnp.cumsum(tiles_e)
    n_used = cum[-1]
    n_tiles = TOP_K * t // tm + N_EXPERTS
    jj = jnp.minimum(jnp.arange(n_tiles, dtype=I32), n_used - 1)
    tile_exp = jnp.sum((jj[:, None] >= cum[None, :]).astype(I32), axis=1)
    tile_blk = tile_exp * (stride // tm) + jj - (cum - tiles_e)[tile_exp]
    yg = _expert_call(tile_blk, tile_exp, n_used[None], xg, w1, w3, w2)
    return _combine_call(h, didx, yg)


def _mix_in_weights(w):
    offs = np.concatenate([[0], np.cumsum(MIX_IN_SIZES)])
    cq, ckv, kr, dq, dk, dv, iq, ik, iw = [w[:, offs[j]:offs[j + 1]] for j in range(9)]
    z = lambda n: jnp.zeros((w.shape[0], n), w.dtype)
    half = MLA_ROPE // 2
    kr_main = jnp.concatenate([z(MLA_NOPE), kr, z(LANES - MLA_QK)], axis=1)
    kr_swap = jnp.concatenate([z(MLA_NOPE), kr[:, half:], kr[:, :half], z(LANES - MLA_QK)], axis=1)
    ik4 = jnp.concatenate([ik] * (LANES // IDX_DIM), axis=1)
    iw_p = jnp.concatenate([iw, z(LANES - IDX_HEADS)], axis=1)
    return jnp.concatenate([cq, ckv, kr_main, kr_swap, dq, dk, dv, iq, ik4, iw_p],
                           axis=1).astype(BF16)


def _mla_q_weights(w_uq):
    r = w_uq.shape[0]
    w = w_uq.reshape(r, MLA_HEADS, MLA_QK)
    nope, rope = w[..., :MLA_NOPE], w[..., MLA_NOPE:]
    half = MLA_ROPE // 2
    z = lambda n: jnp.zeros((r, MLA_HEADS, n), w.dtype)
    main = jnp.concatenate([nope, rope, z(LANES - MLA_QK)], axis=-1)
    swap = jnp.concatenate([z(MLA_NOPE), rope[..., half:], rope[..., :half], z(LANES - MLA_QK)],
                           axis=-1)
    return (main.reshape(r, MLA_HEADS * LANES).astype(BF16),
            swap.reshape(r, MLA_HEADS * LANES).astype(BF16))


def _mla_kv_weights(w_ukv):
    r = w_ukv.shape[0]
    w = w_ukv.reshape(r, MLA_HEADS, MLA_NOPE + MLA_V)
    k_nope = jnp.concatenate([w[..., :MLA_NOPE], jnp.zeros((r, MLA_HEADS, LANES - MLA_NOPE), w.dtype)],
                             axis=-1)
    return (k_nope.reshape(r, MLA_HEADS * LANES).astype(BF16),
            w[..., MLA_NOPE:].reshape(r, MLA_HEADS * MLA_V).astype(BF16))


def _qk_gains(g):
    half = MLA_ROPE // 2
    z = lambda n: jnp.zeros((n,), g.dtype)
    main = jnp.concatenate([g, z(LANES - MLA_QK)])
    swap = jnp.concatenate([z(MLA_NOPE), g[MLA_NOPE + half:], g[MLA_NOPE:MLA_NOPE + half],
                            z(LANES - MLA_QK)])
    return main[None, :], swap[None, :]


def _rope_tables(lp):
    half = MLA_ROPE // 2
    inv = ROPE_BASE ** (-jnp.arange(half, dtype=F32) / half)
    ang = jnp.arange(lp, dtype=jnp.int32).astype(F32)[:, None] * inv[None, :]
    cos, sin = jnp.cos(ang), jnp.sin(ang)
    ones = jnp.ones((lp, MLA_NOPE), F32)
    pad1 = jnp.ones((lp, LANES - MLA_QK), F32)
    zeros = jnp.zeros((lp, MLA_NOPE), F32)
    pad0 = jnp.zeros((lp, LANES - MLA_QK), F32)
    return (jnp.concatenate([ones, cos, cos, pad1], axis=1),
            jnp.concatenate([zeros, -sin, sin, pad0], axis=1))


def _rel_buckets(n):
    max_exact = REL_BUCKETS // 2
    d = np.arange(n)
    df = np.maximum(d, 1).astype(np.float32)
    large = max_exact + (np.log(df / np.float32(max_exact))
                         / np.float32(math.log(REL_MAX_DIST / max_exact))
                         * np.float32(REL_BUCKETS - max_exact)).astype(np.int32)
    large = np.minimum(large, REL_BUCKETS - 1)
    return np.where(d < max_exact, d, large)


def _bias_tiles(rel_bias):
    buckets = _rel_buckets(2 * LANES)
    assert np.all(buckets[LANES - 1:] == REL_BUCKETS - 1)
    dist = np.arange(LANES)[:, None] - np.arange(2 * LANES)[None, :] + LANES
    bk = buckets[np.clip(dist, 0, 2 * LANES - 1)]
    shifted = (rel_bias - rel_bias[REL_BUCKETS - 1:REL_BUCKETS, :]) * LOG2E
    tz = jnp.transpose(shifted[bk], (2, 0, 1))
    return jnp.where(jnp.asarray(dist >= 0)[None], tz, 0.0).astype(F32)


def kernel(x, meta_tokens, rel_bias, ev_norm_mix, ev_w_mix_in, ev_g_q_lat, ev_g_kv_lat, ev_w_uq,
           ev_w_ukv, ev_mla_q_norm, ev_mla_k_norm, ev_dsa_q_norm, ev_dsa_k_norm, ev_w_mix_out,
           ev_norm_ffn, ev_w1, ev_w3, ev_w2, od_norm_mix, od_w_in, od_conv_w, od_w_out,
           od_norm_ffn, od_w_router, od_w1, od_w3, od_w2):
    b, seq, d = x.shape
    assert d == D_MODEL
    l_tot = seq + N_META
    lp = -(-l_tot // BLOCK_Q) * BLOCK_Q
    assert lp % SEQ_TILE == 0, "sequence tiling assumes the padded length is a multiple of 384"
    top_k = min(DSA_TOPK_MAX, l_tot // 4)
    depth = ev_norm_mix.shape[0] + od_norm_mix.shape[0]

    meta = jnp.broadcast_to(meta_tokens[None].astype(x.dtype), (b, N_META, d))
    h = jnp.concatenate([meta, x, jnp.zeros((b, lp - l_tot, d), x.dtype)], axis=1)
    h = h.reshape(b * lp, d)

    cos_t, sin_t = _rope_tables(lp)
    tz = _bias_tiles(rel_bias)
    row2 = lambda v: v[None, :]

    for layer in range(depth):
        i = layer // 2
        if layer % 2 == 0:
            wqm, wqs = _mla_q_weights(ev_w_uq[i])
            wkk, wkv = _mla_kv_weights(ev_w_ukv[i])
            gqm, gqs = _qk_gains(ev_mla_q_norm[i])
            gkm, gks = _qk_gains(ev_mla_k_norm[i])
            gdq = row2(jnp.concatenate([ev_dsa_q_norm[i]] * 2))
            gdk = row2(jnp.concatenate([ev_dsa_k_norm[i]] * 2))
            qm, km, vm, qd, kd, vd, iq, ik, iw = _prep_call(
                h, row2(ev_norm_mix[i]), _mix_in_weights(ev_w_mix_in[i]), row2(ev_g_q_lat[i]),
                row2(ev_g_kv_lat[i]), wqm, wqs, wkk, wkv, gqm, gqs, gkm, gks, gdq, gdk,
                cos_t, sin_t, lp)
            seq3 = lambda a: a.reshape(b, lp, a.shape[1])
            o_mla = _mla_call(seq3(qm), seq3(km), seq3(vm))
            o_dsa = _dsa_call(seq3(iq), seq3(iw), seq3(ik), seq3(qd), seq3(kd), seq3(vd), tz, top_k)
            w_o = ev_w_mix_out[i].astype(BF16)
            n_mla = MLA_HEADS * MLA_V
            h = _mix_out_call(h, o_mla.reshape(b * lp, -1), o_dsa.reshape(b * lp, -1),
                              w_o[:n_mla], w_o[n_mla:])
            h = _ffn_call(h, row2(ev_norm_ffn[i]), ev_w1[i].astype(BF16), ev_w3[i].astype(BF16),
                          ev_w2[i].astype(BF16))
        else:
            h = _conv_call(h.reshape(b, lp, d), row2(od_norm_mix[i]), od_w_in[i].astype(BF16),
                           od_conv_w[i].reshape(CONV_WIDTH, d), od_w_out[i].astype(BF16))
            h = h.reshape(b * lp, d)
            wr = jnp.concatenate(
                [od_w_router[i], jnp.zeros((d, LANES - N_EXPERTS), od_w_router.dtype)], axis=1)
            h = _moe_call(h, row2(od_norm_ffn[i]), wr.astype(BF16), od_w1[i].astype(BF16),
                          od_w3[i].astype(BF16), od_w2[i].astype(BF16))
    return h.reshape(b, lp, d)[:, N_META:l_tot]
```

```python
import functools
import math

import numpy as np
import jax
import jax.numpy as jnp
from jax import lax
from jax.experimental import pallas as pl
from jax.experimental.pallas import tpu as pltpu

F32 = jnp.float32
BF16 = jnp.bfloat16
I32 = jnp.int32

D_MODEL = 1024
N_META = 16
BLOCK_Q = 128
EPS = 1e-6
MLA_HEADS = 8
MLA_Q_RANK = 384
MLA_KV_RANK = 256
MLA_NOPE = 64
MLA_ROPE = 32
MLA_V = 64
MLA_QK = MLA_NOPE + MLA_ROPE
MLA_SCALE = MLA_QK ** -0.5
ROPE_BASE = 10000.0
DSA_HEADS = 8
DSA_HEAD_DIM = 64
DSA_WIDTH = DSA_HEADS * DSA_HEAD_DIM
DSA_SCALE = DSA_HEAD_DIM ** -0.5
IDX_HEADS = 8
IDX_DIM = 32
DSA_TOPK_MAX = 256
REL_BUCKETS = 32
REL_MAX_DIST = 128
MIX_IN_SIZES = (MLA_Q_RANK, MLA_KV_RANK, MLA_ROPE, DSA_WIDTH, DSA_WIDTH, DSA_WIDTH,
                IDX_HEADS * IDX_DIM, IDX_DIM, IDX_HEADS)
CONV_WIDTH = 3
N_EXPERTS = 8
TOP_K = 2

LANES = 128
VMEM_LIMIT_BYTES = 56 * 1024 * 1024

SEQ_TILE = 3 * LANES
MOE_TILE = 512
XG_WIDTH = D_MODEL + LANES
LOG2E = math.log2(math.e)
NEG_BIG = -1e30
INT_MIN = -2 ** 31

_C_CQ = 0
_C_CKV = _C_CQ + MLA_Q_RANK
_C_KRM = _C_CKV + MLA_KV_RANK
_C_KRS = _C_KRM + LANES
_C_DQ = _C_KRS + LANES
_C_DK = _C_DQ + DSA_WIDTH
_C_DV = _C_DK + DSA_WIDTH
_C_IQ = _C_DV + DSA_WIDTH
_C_IK = _C_IQ + IDX_HEADS * IDX_DIM
_C_IW = _C_IK + LANES
_C_END = _C_IW + LANES


def _cparams(sem):
    return pltpu.CompilerParams(dimension_semantics=sem, vmem_limit_bytes=VMEM_LIMIT_BYTES)


def _row_tile(n_rows, candidates=(1024, 768, 512, 384, 256, 128)):
    for c in candidates:
        if n_rows % c == 0:
            return c
    raise ValueError(f"no row tile for {n_rows}")


def _rms(x, g):
    ms = jnp.mean(x * x, axis=-1, keepdims=True)
    return x * lax.rsqrt(ms + EPS) * g


def _dot(a, b):
    return jnp.dot(a, b, preferred_element_type=F32)


def _dot_nt(a, b):
    return lax.dot_general(a, b, (((1,), (1,)), ((), ())), preferred_element_type=F32)


def _prep_kernel(h_ref, g_ref, wext_ref, gql_ref, gkvl_ref, wqm_ref, wqs_ref, wkk_ref, wkv_ref,
                 gqm_ref, gqs_ref, gkm_ref, gks_ref, gdq_ref, gdk_ref, cos_ref, sin_ref,
                 qm_o, km_o, vm_o, qd_o, kd_o, vd_o, iq_o, ik_o, iw_o):
    xn = _rms(h_ref[...], g_ref[...]).astype(BF16)

    def proj(lo, hi):
        return _dot(xn, wext_ref[:, lo:hi])

    cos = cos_ref[...]
    sin = sin_ref[...]
    lane = lax.broadcasted_iota(I32, (xn.shape[0], LANES), 1)

    cqn = _rms(proj(_C_CQ, _C_CKV), gql_ref[...]).astype(BF16)
    q_main = _dot(cqn, wqm_ref[...])
    q_swap = _dot(cqn, wqs_ref[...])
    for hd in range(MLA_HEADS):
        sl = slice(hd * LANES, (hd + 1) * LANES)
        a = q_main[:, sl]
        r = lax.rsqrt(jnp.sum(a * a, axis=-1, keepdims=True) * (1.0 / MLA_QK) + EPS)
        out = (a * r * gqm_ref[...]) * cos + (q_swap[:, sl] * r * gqs_ref[...]) * sin
        qm_o[:, sl] = out.astype(BF16)

    ckvn = _rms(proj(_C_CKV, _C_KRM), gkvl_ref[...]).astype(BF16)
    k_nope = _dot(ckvn, wkk_ref[...])
    vm_o[...] = _dot(ckvn, wkv_ref[...]).astype(BF16)
    kr_main = proj(_C_KRM, _C_KRS)
    kr_swap = proj(_C_KRS, _C_DQ)
    for hd in range(MLA_HEADS):
        sl = slice(hd * LANES, (hd + 1) * LANES)
        a = k_nope[:, sl] + kr_main
        r = lax.rsqrt(jnp.sum(a * a, axis=-1, keepdims=True) * (1.0 / MLA_QK) + EPS)
        out = (a * r * gkm_ref[...]) * cos + (kr_swap * r * gks_ref[...]) * sin
        km_o[:, sl] = out.astype(BF16)

    first = lane < DSA_HEAD_DIM
    for (lo, g2_ref, o_ref) in ((_C_DQ, gdq_ref, qd_o), (_C_DK, gdk_ref, kd_o)):
        for pr in range(DSA_HEADS // 2):
            x = proj(lo + pr * LANES, lo + (pr + 1) * LANES)
            sq = x * x
            s0 = jnp.sum(jnp.where(first, sq, 0.0), axis=-1, keepdims=True)
            s1 = jnp.sum(jnp.where(first, 0.0, sq), axis=-1, keepdims=True)
            r0 = lax.rsqrt(s0 * (1.0 / DSA_HEAD_DIM) + EPS)
            r1 = lax.rsqrt(s1 * (1.0 / DSA_HEAD_DIM) + EPS)
            out = x * jnp.where(first, r0, r1) * g2_ref[...]
            o_ref[:, pr * LANES:(pr + 1) * LANES] = out.astype(BF16)

    vd_o[...] = proj(_C_DV, _C_IQ).astype(BF16)
    iq_o[...] = proj(_C_IQ, _C_IK).astype(BF16)
    ik_o[...] = proj(_C_IK, _C_IW).astype(BF16)
    iw_o[...] = proj(_C_IW, _C_END)


def _prep_call(h, g_mix, wext, gql, gkvl, wqm, wqs, wkk, wkv, gqm, gqs, gkm, gks, gdq, gdk,
               cos_t, sin_t, lp):
    t = h.shape[0]
    tm = SEQ_TILE
    nt = lp // tm
    row = lambda w: pl.BlockSpec((tm, w), lambda i: (i, 0))
    full = lambda a: pl.BlockSpec(a.shape, lambda i: (0, 0))
    tab = pl.BlockSpec((tm, LANES), lambda i: (i % nt, 0))
    hw = MLA_HEADS * LANES
    out_shape = [
        jax.ShapeDtypeStruct((t, hw), BF16), jax.ShapeDtypeStruct((t, hw), BF16),
        jax.ShapeDtypeStruct((t, MLA_HEADS * MLA_V), BF16),
        jax.ShapeDtypeStruct((t, DSA_WIDTH), BF16), jax.ShapeDtypeStruct((t, DSA_WIDTH), BF16),
        jax.ShapeDtypeStruct((t, DSA_WIDTH), BF16),
        jax.ShapeDtypeStruct((t, IDX_HEADS * IDX_DIM), BF16),
        jax.ShapeDtypeStruct((t, LANES), BF16), jax.ShapeDtypeStruct((t, LANES), F32),
    ]
    return pl.pallas_call(
        _prep_kernel,
        grid=(t // tm,),
        in_specs=[row(D_MODEL), full(g_mix), full(wext), full(gql), full(gkvl), full(wqm),
                  full(wqs), full(wkk), full(wkv), full(gqm), full(gqs), full(gkm), full(gks),
                  full(gdq), full(gdk), tab, tab],
        out_specs=[row(s.shape[1]) for s in out_shape],
        out_shape=out_shape,
        compiler_params=_cparams(("parallel",)),
        name="prep_mix_in",
    )(h, g_mix, wext, gql, gkvl, wqm, wqs, wkk, wkv, gqm, gqs, gkm, gks, gdq, gdk, cos_t, sin_t)


def _flash_init(m_scr, l_scr, acc_scr):
    m_scr[...] = jnp.full(m_scr.shape, -jnp.inf, F32)
    l_scr[...] = jnp.zeros(l_scr.shape, F32)
    acc_scr[...] = jnp.zeros(acc_scr.shape, F32)


def _lane_fold(x, op):
    return functools.reduce(op, [x[:, j * LANES:(j + 1) * LANES] for j in range(x.shape[1] // LANES)])


def _flash_update(hd, s, vc, m_scr, l_scr, acc_scr):
    m_old = m_scr[hd]
    m_new = jnp.maximum(m_old, jnp.max(_lane_fold(s, jnp.maximum), axis=-1, keepdims=True))
    alpha = jnp.exp2(m_old - m_new)
    p = jnp.exp2(s - jnp.concatenate([m_new] * (s.shape[1] // LANES), axis=1))
    m_scr[hd] = m_new
    l_scr[hd] = alpha * l_scr[hd] + _lane_fold(p, jnp.add)
    acc_scr[hd] = alpha * acc_scr[hd] + _dot(p.astype(BF16), vc)


def _flash_store(o_ref, l_scr, acc_scr, head_dim):
    lane = lax.broadcasted_iota(I32, acc_scr.shape[1:], 1)
    for pr in range(acc_scr.shape[0] // 2):
        o0, o1 = [acc_scr[hd] / jnp.sum(l_scr[hd], axis=-1, keepdims=True)
                  for hd in (2 * pr, 2 * pr + 1)]
        o_ref[:, pr * LANES:(pr + 1) * LANES] = jnp.where(lane < head_dim, o0, o1).astype(o_ref.dtype)


def _mla_kernel(q_ref, k_ref, v_ref, o_ref, m_scr, l_scr, acc_scr):
    qi = pl.program_id(1)
    tq = q_ref.shape[0]
    c2 = MLA_SCALE * LOG2E
    _flash_init(m_scr, l_scr, acc_scr)

    def chunk(c, diagonal):
        ks = pl.ds(pl.multiple_of(c * tq, tq), tq)
        for hd in range(MLA_HEADS):
            sl = slice(hd * LANES, (hd + 1) * LANES)
            s = _dot_nt(q_ref[:, sl], k_ref[ks, sl]) * c2
            if diagonal:
                row = lax.broadcasted_iota(I32, (tq, tq), 0)
                col = lax.broadcasted_iota(I32, (tq, tq), 1)
                s = jnp.where(col <= row, s, NEG_BIG)
            pr = hd // 2
            _flash_update(hd, s, v_ref[ks, pr * LANES:(pr + 1) * LANES], m_scr, l_scr, acc_scr)

    def body(c, carry):
        chunk(c, False)
        return carry

    lax.fori_loop(0, qi, body, 0)
    chunk(qi, True)
    _flash_store(o_ref, l_scr, acc_scr, MLA_V)


def _mla_call(q, k, v):
    b, lp, _ = q.shape
    tq = SEQ_TILE
    nq = lp // tq
    state = pltpu.VMEM((MLA_HEADS, tq, LANES), F32)
    return pl.pallas_call(
        _mla_kernel,
        grid=(b, nq),
        in_specs=[pl.BlockSpec((None, tq, q.shape[2]), lambda bi, i: (bi, i, 0)),
                  pl.BlockSpec((None, lp, k.shape[2]), lambda bi, i: (bi, 0, 0)),
                  pl.BlockSpec((None, lp, v.shape[2]), lambda bi, i: (bi, 0, 0))],
        out_specs=pl.BlockSpec((None, tq, v.shape[2]), lambda bi, i: (bi, i, 0)),
        out_shape=jax.ShapeDtypeStruct((b, lp, v.shape[2]), BF16),
        scratch_shapes=[state, state, state],
        compiler_params=_cparams(("parallel", "arbitrary")),
        name="mla_attention",
    )(q, k, v)


def _dsa_kernel(iq_ref, iw_ref, ik_ref, q_ref, k_ref, v_ref, tz_ref, o_ref,
                key_scr, madd_scr, iqh_scr, qh_scr, wt_scr, ans_scr, jst_scr, m_scr, l_scr, acc_scr,
                *, top_k):
    qi = pl.program_id(1)
    tq = q_ref.shape[0]
    nsub = tq // LANES
    sublanes = 8
    n_chunks = qi + 1
    lane = lax.broadcasted_iota(I32, (tq, LANES), 1)
    key_i = lax.broadcasted_iota(I32, (tq, tq), 0)
    qry_i = lax.broadcasted_iota(I32, (tq, tq), 1)
    chunk_rows = lambda c: pl.ds(pl.multiple_of(c * tq, tq), tq)

    lane_group = lax.shift_right_logical(lane, int(math.log2(IDX_DIM)))
    for hd in range(IDX_HEADS):
        quad = iq_ref[:, (hd // 4) * LANES:(hd // 4 + 1) * LANES].astype(F32)
        iqh_scr[hd] = jnp.where(lane_group == hd % 4, quad, 0.0).astype(BF16)
    for hd in range(DSA_HEADS):
        pair = q_ref[:, (hd // 2) * LANES:(hd // 2 + 1) * LANES].astype(F32)
        own = lane < DSA_HEAD_DIM if hd % 2 == 0 else lane >= DSA_HEAD_DIM
        qh_scr[hd] = jnp.where(own, pair, 0.0).astype(BF16)
    wt_scr[...] = jnp.transpose(iw_ref[...])[0:IDX_HEADS, :]

    def index_chunk(c, diagonal):
        ikc = ik_ref[chunk_rows(c), :]
        sc = jnp.zeros((tq, tq), F32)
        for hd in range(IDX_HEADS):
            sc = sc + wt_scr[hd:hd + 1, :] * jnp.maximum(_dot_nt(ikc, iqh_scr[hd]), 0.0)
        sc = jnp.where(sc == 0.0, 0.0, sc)
        if diagonal:
            sc = jnp.where(key_i <= qry_i, sc, -jnp.inf)
        bits = pltpu.bitcast(sc, I32)
        key_scr[c] = bits ^ (lax.shift_right_arithmetic(bits, 31) & 0x7FFFFFFF)

    def index_body(c, carry):
        index_chunk(c, False)
        return carry

    lax.fori_loop(0, qi, index_body, 0)
    index_chunk(qi, True)

    kf = float(top_k)
    vec = (sublanes, tq)
    sub_i = lax.broadcasted_iota(I32, vec, 0)

    def count(pred):
        def body(c, accs):
            accs = list(accs)
            kk = key_scr[c]
            for g in range(tq // sublanes):
                hit = jnp.where(pred(kk[g * sublanes:(g + 1) * sublanes, :], c * tq + g * sublanes),
                                1.0, 0.0)
                accs[g % 2] = accs[g % 2] + hit
            return tuple(accs)
        zero = jnp.zeros(vec, F32)
        a0, a1 = lax.fori_loop(0, n_chunks, body, (zero, zero))
        return jnp.broadcast_to(jnp.sum(a0 + a1, axis=0, keepdims=True), vec)

    def bisect_step(bit, ans, done):
        cand = ans + lax.shift_left(jnp.int32(1), bit)
        cnt = count(lambda kk, _i: kk >= cand)
        open_ = done == 0
        ans = jnp.where(jnp.logical_and(open_, cnt >= kf), cand, ans)
        done = jnp.where(jnp.logical_and(open_, cnt == kf), 1, done)
        return ans, done

    def n_open_of(done):
        return jnp.sum(jnp.where(done == 0, 1.0, 0.0))

    def bis_cond(st):
        bit, _, _, n_open = st
        return jnp.logical_and(bit >= 0, n_open > 0.0)

    def bis_body(st):
        bit, ans, done, _ = st
        ans, done = bisect_step(bit, ans, done)
        ans, done = bisect_step(bit - 1, ans, done)
        return bit - 2, ans, done, n_open_of(done)

    t_pos = qi * tq + lax.broadcasted_iota(I32, vec, 1)
    done0 = (t_pos + 1 <= top_k).astype(I32)
    ans0 = jnp.full(vec, INT_MIN, I32)
    _, ans, done, n_open = lax.while_loop(bis_cond, bis_body,
                                          (jnp.int32(31), ans0, done0, n_open_of(done0)))
    ans_scr[...] = ans
    jst_scr[...] = jnp.full(vec, 2 ** 31 - 1, I32)

    @pl.when(n_open > 0.0)
    def _():
        need = kf - count(lambda kk, _i: kk > ans)

        def tie_body(i, jst):
            cand = jst + lax.shift_left(jnp.int32(1), 13 - i)
            cnt = count(lambda kk, i0: jnp.logical_and(kk == ans, sub_i + i0 < cand))
            return jnp.where(cnt < need, cand, jst)

        jst = lax.fori_loop(0, 14, tie_body, jnp.zeros(vec, I32))
        jst_scr[...] = jnp.where(done == 0, jst, 2 ** 31 - 1)

    ans_row = ans_scr[0:1, :]
    jst_row = jst_scr[0:1, :]

    def mask_chunk(c, diagonal):
        kk = key_scr[c]
        sel = jnp.logical_or(kk > ans_row,
                             jnp.logical_and(kk == ans_row, key_i + c * tq <= jst_row))
        if diagonal:
            sel = jnp.logical_and(sel, key_i <= qry_i)
        madd_scr[c] = jnp.transpose(jnp.where(sel, 0.0, NEG_BIG))

    def mask_body(c, carry):
        mask_chunk(c, False)
        return carry

    lax.fori_loop(0, qi, mask_body, 0)
    mask_chunk(qi, True)

    c2 = DSA_SCALE * LOG2E
    _flash_init(m_scr, l_scr, acc_scr)

    def bias_of(hd, where_):
        near = tz_ref[hd, :, LANES:2 * LANES]
        far = tz_ref[hd, :, 0:LANES]
        z = jnp.zeros((LANES, LANES), F32)
        if where_ == "previous":
            blocks = [[far if (a == 0 and b == nsub - 1) else z for b in range(nsub)]
                      for a in range(nsub)]
        else:
            blocks = [[near if b == a else far if b == a - 1 else z for b in range(nsub)]
                      for a in range(nsub)]
        return jnp.concatenate([jnp.concatenate(r, axis=1) for r in blocks], axis=0)

    def chunk(c, where_):
        ks = chunk_rows(c)
        madd = madd_scr[c]
        for hd in range(DSA_HEADS):
            sl = slice((hd // 2) * LANES, (hd // 2 + 1) * LANES)
            s = _dot_nt(qh_scr[hd], k_ref[ks, sl]) * c2 + madd
            if where_ is not None:
                s = s + bias_of(hd, where_)
            _flash_update(hd, s, v_ref[ks, sl], m_scr, l_scr, acc_scr)

    def body(c, carry):
        chunk(c, None)
        return carry

    lax.fori_loop(0, jnp.maximum(qi - 1, 0), body, 0)

    @pl.when(qi >= 1)
    def _():
        chunk(qi - 1, "previous")

    chunk(qi, "diagonal")
    _flash_store(o_ref, l_scr, acc_scr, DSA_HEAD_DIM)


def _dsa_call(iq, iw, ik, q, k, v, tz, top_k):
    b, lp, _ = q.shape
    tq = SEQ_TILE
    nq = lp // tq
    qspec = lambda w: pl.BlockSpec((None, tq, w), lambda bi, i: (bi, i, 0))
    kspec = lambda w: pl.BlockSpec((None, lp, w), lambda bi, i: (bi, 0, 0))
    state = pltpu.VMEM((DSA_HEADS, tq, LANES), F32)
    return pl.pallas_call(
        functools.partial(_dsa_kernel, top_k=top_k),
        grid=(b, nq),
        in_specs=[qspec(iq.shape[2]), qspec(LANES), kspec(LANES), qspec(DSA_WIDTH),
                  kspec(DSA_WIDTH), kspec(DSA_WIDTH),
                  pl.BlockSpec(tz.shape, lambda bi, i: (0, 0, 0))],
        out_specs=qspec(DSA_WIDTH),
        out_shape=jax.ShapeDtypeStruct((b, lp, DSA_WIDTH), BF16),
        scratch_shapes=[pltpu.VMEM((nq, tq, tq), I32), pltpu.VMEM((nq, tq, tq), F32),
                        pltpu.VMEM((IDX_HEADS, tq, LANES), BF16),
                        pltpu.VMEM((DSA_HEADS, tq, LANES), BF16),
                        pltpu.VMEM((IDX_HEADS, tq), F32),
                        pltpu.VMEM((8, tq), I32), pltpu.VMEM((8, tq), I32),
                        state, state, state],
        compiler_params=_cparams(("parallel", "arbitrary")),
        name="dsa_attention",
    )(iq, iw, ik, q, k, v, tz)


def _mix_out_kernel(h_ref, a_ref, b_ref, wa_ref, wb_ref, o_ref):
    o_ref[...] = h_ref[...] + _dot(a_ref[...], wa_ref[...]) + _dot(b_ref[...], wb_ref[...])


def _mix_out_call(h, a, b, wa, wb):
    t = h.shape[0]
    tm = _row_tile(t)
    row = lambda w: pl.BlockSpec((tm, w), lambda i: (i, 0))
    full = lambda x: pl.BlockSpec(x.shape, lambda i: (0, 0))
    return pl.pallas_call(
        _mix_out_kernel,
        grid=(t // tm,),
        in_specs=[row(D_MODEL), row(a.shape[1]), row(b.shape[1]), full(wa), full(wb)],
        out_specs=row(D_MODEL),
        out_shape=jax.ShapeDtypeStruct(h.shape, F32),
        compiler_params=_cparams(("parallel",)),
        name="mix_out",
    )(h, a, b, wa, wb)


def _ffn_kernel(h_ref, g_ref, w1_ref, w3_ref, w2_ref, o_ref, xn_scr, acc_scr):
    f = pl.program_id(1)

    @pl.when(f == 0)
    def _():
        xn_scr[...] = _rms(h_ref[...], g_ref[...]).astype(BF16)
        acc_scr[...] = jnp.zeros(acc_scr.shape, F32)

    xn = xn_scr[...]
    a = _dot(xn, w1_ref[...])
    act = (a * jax.nn.sigmoid(a)) * _dot(xn, w3_ref[...])
    acc_scr[...] += _dot(act.astype(BF16), w2_ref[...])

    @pl.when(f == pl.num_programs(1) - 1)
    def _():
        o_ref[...] = h_ref[...] + acc_scr[...]


def _ffn_call(h, g, w1, w3, w2):
    t = h.shape[0]
    d_ff = w1.shape[1]
    tm = _row_tile(t)
    tf = 512 if d_ff % 512 == 0 else d_ff
    return pl.pallas_call(
        _ffn_kernel,
        grid=(t // tm, d_ff // tf),
        in_specs=[pl.BlockSpec((tm, D_MODEL), lambda i, f: (i, 0)),
                  pl.BlockSpec(g.shape, lambda i, f: (0, 0)),
                  pl.BlockSpec((D_MODEL, tf), lambda i, f: (0, f)),
                  pl.BlockSpec((D_MODEL, tf), lambda i, f: (0, f)),
                  pl.BlockSpec((tf, D_MODEL), lambda i, f: (f, 0))],
        out_specs=pl.BlockSpec((tm, D_MODEL), lambda i, f: (i, 0)),
        out_shape=jax.ShapeDtypeStruct(h.shape, F32),
        scratch_shapes=[pltpu.VMEM((tm, D_MODEL), BF16), pltpu.VMEM((tm, D_MODEL), F32)],
        compiler_params=_cparams(("parallel", "arbitrary")),
        name="dense_swiglu",
    )(h, g, w1, w3, w2)


def _conv_kernel(h_ref, g_ref, win_ref, cw_ref, wout_ref, o_ref, z_scr):
    i = pl.program_id(1)
    tm = h_ref.shape[0]
    halo = 8
    x = h_ref[...]
    xn = _rms(x, g_ref[...]).astype(BF16)
    c_gate = _dot(xn, win_ref[:, D_MODEL:2 * D_MODEL])
    u = _dot(xn, win_ref[:, 2 * D_MODEL:3 * D_MODEL])
    z = c_gate * u

    @pl.when(i == 0)
    def _():
        z_scr[0:halo, :] = jnp.zeros((halo, D_MODEL), F32)

    @pl.when(i > 0)
    def _():
        z_scr[0:halo, :] = z_scr[tm:tm + halo, :]

    z_scr[halo:halo + tm, :] = z
    cw = cw_ref[...]
    y = (cw[0:1, :] * z_scr[halo - 2:halo - 2 + tm, :]
         + cw[1:2, :] * z_scr[halo - 1:halo - 1 + tm, :]
         + cw[2:3, :] * z)
    b_gate = _dot(xn, win_ref[:, 0:D_MODEL])
    o_ref[...] = x + _dot((b_gate * y).astype(BF16), wout_ref[...])


def _conv_call(h3, g, w_in, cw, w_out):
    b, lp, d = h3.shape
    tm = SEQ_TILE
    full = lambda a: pl.BlockSpec(a.shape, lambda bi, i: (0,) * a.ndim)
    return pl.pallas_call(
        _conv_kernel,
        grid=(b, lp // tm),
        in_specs=[pl.BlockSpec((None, tm, d), lambda bi, i: (bi, i, 0)), full(g), full(w_in),
                  full(cw), full(w_out)],
        out_specs=pl.BlockSpec((None, tm, d), lambda bi, i: (bi, i, 0)),
        out_shape=jax.ShapeDtypeStruct(h3.shape, F32),
        scratch_shapes=[pltpu.VMEM((tm + 8, d), F32)],
        compiler_params=_cparams(("arbitrary", "arbitrary")),
        name="short_conv_mixer",
    )(h3, g, w_in, cw, w_out)


def _route_kernel(h_ref, g_ref, wr_ref, xg_ref, didx_ref, cnt_ref,
                  xs_scr, base_scr, dvm_scr, dsm_scr, sem_idx, sem_rows, *, stride):
    i = pl.program_id(0)
    tm = h_ref.shape[0]
    lane = lax.broadcasted_iota(I32, (tm, LANES), 1)
    lane_f = lane.astype(F32)

    @pl.when(i == 0)
    def _():
        base_scr[...] = jnp.zeros(base_scr.shape, F32)

    xnf = _rms(h_ref[...], g_ref[...])
    logits = jnp.where(lane < N_EXPERTS, _dot(xnf.astype(BF16), wr_ref[...]), -jnp.inf)
    v1 = jnp.max(logits, axis=-1, keepdims=True)
    i1 = jnp.min(jnp.where(logits == v1, lane_f, float(LANES)), axis=-1, keepdims=True)
    rest = jnp.where(lane_f == i1, -jnp.inf, logits)
    v2 = jnp.max(rest, axis=-1, keepdims=True)
    i2 = jnp.min(jnp.where(rest == v2, lane_f, float(LANES)), axis=-1, keepdims=True)
    e2 = jnp.exp(v2 - v1)
    den = 1.0 + e2
    gates = (1.0 / den, e2 / den)

    oh1 = jnp.where(lane_f == i1, 1.0, 0.0)
    oh2 = jnp.where(lane_f == i2, 1.0, 0.0)
    oh = oh1 + oh2
    earlier = (lax.broadcasted_iota(I32, (tm, tm), 1) < lax.broadcasted_iota(I32, (tm, tm), 0))
    prefix = _dot(jnp.where(earlier, 1.0, 0.0).astype(BF16), oh.astype(BF16))
    pos = base_scr[0:1, :] + prefix
    d1 = jnp.sum(oh1 * pos, axis=-1, keepdims=True) + i1 * float(stride)
    d2 = jnp.sum(oh2 * pos, axis=-1, keepdims=True) + i2 * float(stride)
    base_scr[...] = base_scr[...] + jnp.sum(oh, axis=0, keepdims=True)
    cnt_ref[...] = base_scr[...]

    for s in range(TOP_K):
        xs_scr[s, :, 0:D_MODEL] = xnf
        xs_scr[s, :, D_MODEL:] = jnp.broadcast_to(gates[s], (tm, LANES))

    dmat = jnp.where(lane == 0, d1, jnp.where(lane == 1, d2, 0.0))
    dvm_scr[...] = jnp.transpose(dmat)[0:8, :].astype(I32)
    didx_ref[...] = dvm_scr[...]
    cp = pltpu.make_async_copy(dvm_scr, dsm_scr, sem_idx)
    cp.start()
    cp.wait()

    def issue(r, carry):
        for s in range(TOP_K):
            pltpu.make_async_copy(xs_scr.at[s, pl.ds(r, 1)], xg_ref.at[pl.ds(dsm_scr[s, r], 1)],
                                  sem_rows).start()
        return carry

    lax.fori_loop(0, tm, issue, 0, unroll=8)
    for s in range(TOP_K):
        pltpu.make_async_copy(xs_scr.at[s], xg_ref.at[pl.ds(0, tm)], sem_rows).wait()

    @pl.when(i == pl.num_programs(0) - 1)
    def _():
        xs_scr[0] = jnp.zeros(xs_scr.shape[1:], F32)
        dvm_scr[:, 0:LANES] = base_scr[...].astype(I32)
        cp2 = pltpu.make_async_copy(dvm_scr, dsm_scr, sem_idx)
        cp2.start()
        cp2.wait()
        sublanes = 8
        for phase in ("start", "wait"):
            for e in range(N_EXPERTS):
                n_e = dsm_scr[0, e]
                n_up = lax.shift_left(lax.shift_right_logical(n_e + sublanes - 1, 3), 3)
                for k in range(sublanes - 1):
                    @pl.when(n_e + k < n_up)
                    def _():
                        row = pltpu.make_async_copy(
                            xs_scr.at[0, pl.ds(0, 1)], xg_ref.at[pl.ds(e * stride + n_e + k, 1)],
                            sem_rows)
                        row.start() if phase == "start" else row.wait()
                blk = pltpu.make_async_copy(
                    xs_scr.at[0], xg_ref.at[pl.ds(pl.multiple_of(e * stride + n_up, sublanes), tm)],
                    sem_rows)
                blk.start() if phase == "start" else blk.wait()


def _route_call(h, g, wr, stride):
    t = h.shape[0]
    tm = MOE_TILE
    return pl.pallas_call(
        functools.partial(_route_kernel, stride=stride),
        grid=(t // tm,),
        in_specs=[pl.BlockSpec((tm, D_MODEL), lambda i: (i, 0)),
                  pl.BlockSpec(g.shape, lambda i: (0, 0)),
                  pl.BlockSpec(wr.shape, lambda i: (0, 0))],
        out_specs=[pl.BlockSpec(memory_space=pl.ANY),
                   pl.BlockSpec((8, tm), lambda i: (0, i)),
                   pl.BlockSpec((8, LANES), lambda i: (0, 0))],
        out_shape=[jax.ShapeDtypeStruct((N_EXPERTS * stride, XG_WIDTH), F32),
                   jax.ShapeDtypeStruct((8, t), I32),
                   jax.ShapeDtypeStruct((8, LANES), F32)],
        scratch_shapes=[pltpu.VMEM((TOP_K, tm, XG_WIDTH), F32), pltpu.VMEM((8, LANES), F32),
                        pltpu.VMEM((8, tm), I32), pltpu.SMEM((8, tm), I32),
                        pltpu.SemaphoreType.DMA, pltpu.SemaphoreType.DMA],
        compiler_params=_cparams(("arbitrary",)),
        name="moe_route",
    )(h, g, wr)


def _expert_kernel(blk_ref, exp_ref, nu_ref, x_ref, w1_ref, w3_ref, w2_ref, o_ref, xb_scr, acc_scr):
    j = pl.program_id(0)
    f = pl.program_id(1)

    @pl.when(j < nu_ref[0])
    def _():
        @pl.when(f == 0)
        def _():
            xb_scr[...] = x_ref[:, 0:D_MODEL].astype(BF16)
            acc_scr[...] = jnp.zeros(acc_scr.shape, F32)

        xb = xb_scr[...]
        a = _dot(xb, w1_ref[...])
        act = (a * jax.nn.sigmoid(a)) * _dot(xb, w3_ref[...])
        acc_scr[...] += _dot(act.astype(BF16), w2_ref[...])

        @pl.when(f == pl.num_programs(1) - 1)
        def _():
            gate = x_ref[:, D_MODEL:]
            o_ref[...] = acc_scr[...] * jnp.concatenate([gate] * (D_MODEL // LANES), axis=1)


def _expert_call(tile_blk, tile_exp, n_used, xg, w1, w3, w2):
    d_ff = w1.shape[2]
    tm = MOE_TILE
    tf = 512 if d_ff % 512 == 0 else d_ff
    nf = d_ff // tf
    n_tiles = tile_blk.shape[0]
    fe = lambda j, f, nu: jnp.where(j < nu[0], f, nf - 1)
    grid_spec = pltpu.PrefetchScalarGridSpec(
        num_scalar_prefetch=3,
        grid=(n_tiles, nf),
        in_specs=[pl.BlockSpec((tm, XG_WIDTH), lambda j, f, blk, ex, nu: (blk[j], 0)),
                  pl.BlockSpec((None, D_MODEL, tf), lambda j, f, blk, ex, nu: (ex[j], 0, fe(j, f, nu))),
                  pl.BlockSpec((None, D_MODEL, tf), lambda j, f, blk, ex, nu: (ex[j], 0, fe(j, f, nu))),
                  pl.BlockSpec((None, tf, D_MODEL), lambda j, f, blk, ex, nu: (ex[j], fe(j, f, nu), 0))],
        out_specs=pl.BlockSpec((tm, D_MODEL), lambda j, f, blk, ex, nu: (blk[j], 0)),
        scratch_shapes=[pltpu.VMEM((tm, D_MODEL), BF16), pltpu.VMEM((tm, D_MODEL), F32)])
    return pl.pallas_call(
        _expert_kernel,
        grid_spec=grid_spec,
        out_shape=jax.ShapeDtypeStruct((xg.shape[0], D_MODEL), F32),
        compiler_params=_cparams(("arbitrary", "arbitrary")),
        name="expert_swiglu",
    )(tile_blk, tile_exp, n_used, xg, w1, w3, w2)


def _combine_kernel(h_ref, didx_ref, yg_ref, o_ref, y_scr, dsm_scr, sem_idx, sem_rows):
    i = pl.program_id(0)
    tm = h_ref.shape[0]
    cp = pltpu.make_async_copy(didx_ref.at[:, pl.ds(pl.multiple_of(i * tm, tm), tm)], dsm_scr, sem_idx)
    cp.start()
    cp.wait()

    def issue(r, carry):
        for s in range(TOP_K):
            pltpu.make_async_copy(yg_ref.at[pl.ds(dsm_scr[s, r], 1)], y_scr.at[s, pl.ds(r, 1)],
                                  sem_rows).start()
        return carry

    lax.fori_loop(0, tm, issue, 0, unroll=8)
    for s in range(TOP_K):
        pltpu.make_async_copy(yg_ref.at[pl.ds(0, tm)], y_scr.at[s], sem_rows).wait()
    o_ref[...] = h_ref[...] + y_scr[0] + y_scr[1]


def _combine_call(h, didx, yg):
    t = h.shape[0]
    tm = MOE_TILE
    return pl.pallas_call(
        _combine_kernel,
        grid=(t // tm,),
        in_specs=[pl.BlockSpec((tm, D_MODEL), lambda i: (i, 0)),
                  pl.BlockSpec(memory_space=pl.ANY), pl.BlockSpec(memory_space=pl.ANY)],
        out_specs=pl.BlockSpec((tm, D_MODEL), lambda i: (i, 0)),
        out_shape=jax.ShapeDtypeStruct(h.shape, F32),
        scratch_shapes=[pltpu.VMEM((TOP_K, tm, D_MODEL), F32), pltpu.SMEM((8, tm), I32),
                        pltpu.SemaphoreType.DMA, pltpu.SemaphoreType.DMA],
        compiler_params=_cparams(("arbitrary",)),
        name="moe_combine",
    )(h, didx, yg)


def _moe_call(h, g, wr, w1, w3, w2):
    t = h.shape[0]
    tm = MOE_TILE
    assert t % tm == 0
    stride = t + tm
    xg, didx, cnt = _route_call(h, g, wr, stride)
    counts = cnt[0, :N_EXPERTS].astype(I32)
    tiles_e = (counts + tm - 1) // tm
    cum = jnp.cumsum(tiles_e)
    n_used = cum[-1]
    n_tiles = TOP_K * t // tm + N_EXPERTS
    jj = jnp.minimum(jnp.arange(n_tiles, dtype=I32), n_used - 1)
    tile_exp = jnp.sum((jj[:, None] >= cum[None, :]).astype(I32), axis=1)
    tile_blk = tile_exp * (stride // tm) + jj - (cum - tiles_e)[tile_exp]
    yg = _expert_call(tile_blk, tile_exp, n_used[None], xg, w1, w3, w2)
    return _combine_call(h, didx, yg)


def _mix_in_weights(w):
    offs = np.concatenate([[0], np.cumsum(MIX_IN_SIZES)])
    cq, ckv, kr, dq, dk, dv, iq, ik, iw = [w[:, offs[j]:offs[j + 1]] for j in range(9)]
    z = lambda n: jnp.zeros((w.shape[0], n), w.dtype)
    half = MLA_ROPE // 2
    kr_main = jnp.concatenate([z(MLA_NOPE), kr, z(LANES - MLA_QK)], axis=1)
    kr_swap = jnp.concatenate([z(MLA_NOPE), kr[:, half:], kr[:, :half], z(LANES - MLA_QK)], axis=1)
    ik4 = jnp.concatenate([ik] * (LANES // IDX_DIM), axis=1)
    iw_p = jnp.concatenate([iw, z(LANES - IDX_HEADS)], axis=1)
    return jnp.concatenate([cq, ckv, kr_main, kr_swap, dq, dk, dv, iq, ik4, iw_p],
                           axis=1).astype(BF16)


def _mla_q_weights(w_uq):
    r = w_uq.shape[0]
    w = w_uq.reshape(r, MLA_HEADS, MLA_QK)
    nope, rope = w[..., :MLA_NOPE], w[..., MLA_NOPE:]
    half = MLA_ROPE // 2
    z = lambda n: jnp.zeros((r, MLA_HEADS, n), w.dtype)
    main = jnp.concatenate([nope, rope, z(LANES - MLA_QK)], axis=-1)
    swap = jnp.concatenate([z(MLA_NOPE), rope[..., half:], rope[..., :half], z(LANES - MLA_QK)],
                           axis=-1)
    return (main.reshape(r, MLA_HEADS * LANES).astype(BF16),
            swap.reshape(r, MLA_HEADS * LANES).astype(BF16))


def _mla_kv_weights(w_ukv):
    r = w_ukv.shape[0]
    w = w_ukv.reshape(r, MLA_HEADS, MLA_NOPE + MLA_V)
    k_nope = jnp.concatenate([w[..., :MLA_NOPE], jnp.zeros((r, MLA_HEADS, LANES - MLA_NOPE), w.dtype)],
                             axis=-1)
    return (k_nope.reshape(r, MLA_HEADS * LANES).astype(BF16),
            w[..., MLA_NOPE:].reshape(r, MLA_HEADS * MLA_V).astype(BF16))


def _qk_gains(g):
    half = MLA_ROPE // 2
    z = lambda n: jnp.zeros((n,), g.dtype)
    main = jnp.concatenate([g, z(LANES - MLA_QK)])
    swap = jnp.concatenate([z(MLA_NOPE), g[MLA_NOPE + half:], g[MLA_NOPE:MLA_NOPE + half],
                            z(LANES - MLA_QK)])
    return main[None, :], swap[None, :]


def _rope_tables(lp):
    half = MLA_ROPE // 2
    inv = ROPE_BASE ** (-jnp.arange(half, dtype=F32) / half)
    ang = jnp.arange(lp, dtype=jnp.int32).astype(F32)[:, None] * inv[None, :]
    cos, sin = jnp.cos(ang), jnp.sin(ang)
    ones = jnp.ones((lp, MLA_NOPE), F32)
    pad1 = jnp.ones((lp, LANES - MLA_QK), F32)
    zeros = jnp.zeros((lp, MLA_NOPE), F32)
    pad0 = jnp.zeros((lp, LANES - MLA_QK), F32)
    return (jnp.concatenate([ones, cos, cos, pad1], axis=1),
            jnp.concatenate([zeros, -sin, sin, pad0], axis=1))


def _rel_buckets(n):
    max_exact = REL_BUCKETS // 2
    d = np.arange(n)
    df = np.maximum(d, 1).astype(np.float32)
    large = max_exact + (np.log(df / np.float32(max_exact))
                         / np.float32(math.log(REL_MAX_DIST / max_exact))
                         * np.float32(REL_BUCKETS - max_exact)).astype(np.int32)
    large = np.minimum(large, REL_BUCKETS - 1)
    return np.where(d < max_exact, d, large)


def _bias_tiles(rel_bias):
    buckets = _rel_buckets(2 * LANES)
    assert np.all(buckets[LANES - 1:] == REL_BUCKETS - 1)
    dist = np.arange(LANES)[:, None] - np.arange(2 * LANES)[None, :] + LANES
    bk = buckets[np.clip(dist, 0, 2 * LANES - 1)]
    shifted = (rel_bias - rel_bias[REL_BUCKETS - 1:REL_BUCKETS, :]) * LOG2E
    tz = jnp.transpose(shifted[bk], (2, 0, 1))
    return jnp.where(jnp.asarray(dist >= 0)[None], tz, 0.0).astype(F32)


def kernel(x, meta_tokens, rel_bias, ev_norm_mix, ev_w_mix_in, ev_g_q_lat, ev_g_kv_lat, ev_w_uq,
           ev_w_ukv, ev_mla_q_norm, ev_mla_k_norm, ev_dsa_q_norm, ev_dsa_k_norm, ev_w_mix_out,
           ev_norm_ffn, ev_w1, ev_w3, ev_w2, od_norm_mix, od_w_in, od_conv_w, od_w_out,
           od_norm_ffn, od_w_router, od_w1, od_w3, od_w2):
    b, seq, d = x.shape
    assert d == D_MODEL
    l_tot = seq + N_META
    lp = -(-l_tot // BLOCK_Q) * BLOCK_Q
    assert lp % SEQ_TILE == 0, "sequence tiling assumes the padded length is a multiple of 384"
    top_k = min(DSA_TOPK_MAX, l_tot // 4)
    depth = ev_norm_mix.shape[0] + od_norm_mix.shape[0]

    meta = jnp.broadcast_to(meta_tokens[None].astype(x.dtype), (b, N_META, d))
    h = jnp.concatenate([meta, x, jnp.zeros((b, lp - l_tot, d), x.dtype)], axis=1)
    h = h.reshape(b * lp, d)

    cos_t, sin_t = _rope_tables(lp)
    tz = _bias_tiles(rel_bias)
    row2 = lambda v: v[None, :]

    for layer in range(depth):
        i = layer // 2
        if layer % 2 == 0:
            wqm, wqs = _mla_q_weights(ev_w_uq[i])
            wkk, wkv = _mla_kv_weights(ev_w_ukv[i])
            gqm, gqs = _qk_gains(ev_mla_q_norm[i])
            gkm, gks = _qk_gains(ev_mla_k_norm[i])
            gdq = row2(jnp.concatenate([ev_dsa_q_norm[i]] * 2))
            gdk = row2(jnp.concatenate([ev_dsa_k_norm[i]] * 2))
            qm, km, vm, qd, kd, vd, iq, ik, iw = _prep_call(
                h, row2(ev_norm_mix[i]), _mix_in_weights(ev_w_mix_in[i]), row2(ev_g_q_lat[i]),
                row2(ev_g_kv_lat[i]), wqm, wqs, wkk, wkv, gqm, gqs, gkm, gks, gdq, gdk,
                cos_t, sin_t, lp)
            seq3 = lambda a: a.reshape(b, lp, a.shape[1])
            o_mla = _mla_call(seq3(qm), seq3(km), seq3(vm))
            o_dsa = _dsa_call(seq3(iq), seq3(iw), seq3(ik), seq3(qd), seq3(kd), seq3(vd), tz, top_k)
            w_o = ev_w_mix_out[i].astype(BF16)
            n_mla = MLA_HEADS * MLA_V
            h = _mix_out_call(h, o_mla.reshape(b * lp, -1), o_dsa.reshape(b * lp, -1),
                              w_o[:n_mla], w_o[n_mla:])
            h = _ffn_call(h, row2(ev_norm_ffn[i]), ev_w1[i].astype(BF16), ev_w3[i].astype(BF16),
                          ev_w2[i].astype(BF16))
        else:
            h = _conv_call(h.reshape(b, lp, d), row2(od_norm_mix[i]), od_w_in[i].astype(BF16),
                           od_conv_w[i].reshape(CONV_WIDTH, d), od_w_out[i].astype(BF16))
            h = h.reshape(b * lp, d)
            wr = jnp.concatenate(
                [od_w_router[i], jnp.zeros((d, LANES - N_EXPERTS), od_w_router.dtype)], axis=1)
            h = _moe_call(h, row2(od_norm_ffn[i]), wr.astype(BF16), od_w1[i].astype(BF16),
                          od_w3[i].astype(BF16), od_w2[i].astype(BF16))
    return h.reshape(b, lp, d)[:, N_META:l_tot]
```

```python
import functools
import math

import numpy as np
import jax
import jax.numpy as jnp
from jax import lax
from jax.experimental import pallas as pl
from jax.experimental.pallas import tpu as pltpu

F32 = jnp.float32
BF16 = jnp.bfloat16
I32 = jnp.int32

D_MODEL = 1024
N_META = 16
BLOCK_Q = 128
EPS = 1e-6
MLA_HEADS = 8
MLA_Q_RANK = 384
MLA_KV_RANK = 256
MLA_NOPE = 64
MLA_ROPE = 32
MLA_V = 64
MLA_QK = MLA_NOPE + MLA_ROPE
MLA_SCALE = MLA_QK ** -0.5
ROPE_BASE = 10000.0
DSA_HEADS = 8
DSA_HEAD_DIM = 64
DSA_WIDTH = DSA_HEADS * DSA_HEAD_DIM
DSA_SCALE = DSA_HEAD_DIM ** -0.5
IDX_HEADS = 8
IDX_DIM = 32
DSA_TOPK_MAX = 256
REL_BUCKETS = 32
REL_MAX_DIST = 128
MIX_IN_SIZES = (MLA_Q_RANK, MLA_KV_RANK, MLA_ROPE, DSA_WIDTH, DSA_WIDTH, DSA_WIDTH,
                IDX_HEADS * IDX_DIM, IDX_DIM, IDX_HEADS)
CONV_WIDTH = 3
N_EXPERTS = 8
TOP_K = 2

LANES = 128
VMEM_LIMIT_BYTES = 56 * 1024 * 1024

SEQ_TILE = 3 * LANES
MOE_TILE = 512
EXPERT_TILE = 2 * MOE_TILE
XG_WIDTH = D_MODEL + LANES
LOG2E = math.log2(math.e)
NEG_BIG = -1e30
INT_MIN = -2 ** 31

_C_CQ = 0
_C_CKV = _C_CQ + MLA_Q_RANK
_C_KRM = _C_CKV + MLA_KV_RANK
_C_KRS = _C_KRM + LANES
_C_DQ = _C_KRS + LANES
_C_DK = _C_DQ + DSA_WIDTH
_C_DV = _C_DK + DSA_WIDTH
_C_IQ = _C_DV + DSA_WIDTH
_C_IK = _C_IQ + IDX_HEADS * IDX_DIM
_C_IW = _C_IK + LANES
_C_END = _C_IW + LANES


def _cparams(sem):
    return pltpu.CompilerParams(dimension_semantics=sem, vmem_limit_bytes=VMEM_LIMIT_BYTES)


def _row_tile(n_rows, candidates=(1024, 768, 512, 384, 256, 128)):
    for c in candidates:
        if n_rows % c == 0:
            return c
    raise ValueError(f"no row tile for {n_rows}")


def _rms(x, g):
    ms = jnp.mean(x * x, axis=-1, keepdims=True)
    return x * lax.rsqrt(ms + EPS) * g


def _dot(a, b):
    return jnp.dot(a, b, preferred_element_type=F32)


def _dot_nt(a, b):
    return lax.dot_general(a, b, (((1,), (1,)), ((), ())), preferred_element_type=F32)


def _prep_kernel(h_ref, g_ref, wext_ref, gql_ref, gkvl_ref, wqm_ref, wqs_ref, wkk_ref, wkv_ref,
                 gqm_ref, gqs_ref, gkm_ref, gks_ref, gdq_ref, gdk_ref, cos_ref, sin_ref,
                 qm_o, km_o, vm_o, qd_o, kd_o, vd_o, iq_o, ik_o, iw_o):
    xn = _rms(h_ref[...], g_ref[...]).astype(BF16)

    def proj(lo, hi):
        return _dot(xn, wext_ref[:, lo:hi])

    cos = cos_ref[...]
    sin = sin_ref[...]
    lane = lax.broadcasted_iota(I32, (xn.shape[0], LANES), 1)

    cqn = _rms(proj(_C_CQ, _C_CKV), gql_ref[...]).astype(BF16)
    q_main = _dot(cqn, wqm_ref[...])
    q_swap = _dot(cqn, wqs_ref[...])
    for hd in range(MLA_HEADS):
        sl = slice(hd * LANES, (hd + 1) * LANES)
        a = q_main[:, sl]
        r = lax.rsqrt(jnp.sum(a * a, axis=-1, keepdims=True) * (1.0 / MLA_QK) + EPS)
        out = (a * r * gqm_ref[...]) * cos + (q_swap[:, sl] * r * gqs_ref[...]) * sin
        qm_o[:, sl] = (out * (MLA_SCALE * LOG2E)).astype(BF16)

    ckvn = _rms(proj(_C_CKV, _C_KRM), gkvl_ref[...]).astype(BF16)
    k_nope = _dot(ckvn, wkk_ref[...])
    vm_o[...] = _dot(ckvn, wkv_ref[...]).astype(BF16)
    kr_main = proj(_C_KRM, _C_KRS)
    kr_swap = proj(_C_KRS, _C_DQ)
    for hd in range(MLA_HEADS):
        sl = slice(hd * LANES, (hd + 1) * LANES)
        a = k_nope[:, sl] + kr_main
        r = lax.rsqrt(jnp.sum(a * a, axis=-1, keepdims=True) * (1.0 / MLA_QK) + EPS)
        out = (a * r * gkm_ref[...]) * cos + (kr_swap * r * gks_ref[...]) * sin
        km_o[:, sl] = out.astype(BF16)

    first = lane < DSA_HEAD_DIM
    for (lo, g2_ref, o_ref, post) in ((_C_DQ, gdq_ref, qd_o, DSA_SCALE * LOG2E),
                                      (_C_DK, gdk_ref, kd_o, None)):
        for pr in range(DSA_HEADS // 2):
            x = proj(lo + pr * LANES, lo + (pr + 1) * LANES)
            sq = x * x
            s0 = jnp.sum(jnp.where(first, sq, 0.0), axis=-1, keepdims=True)
            s1 = jnp.sum(jnp.where(first, 0.0, sq), axis=-1, keepdims=True)
            r0 = lax.rsqrt(s0 * (1.0 / DSA_HEAD_DIM) + EPS)
            r1 = lax.rsqrt(s1 * (1.0 / DSA_HEAD_DIM) + EPS)
            out = x * jnp.where(first, r0, r1) * g2_ref[...]
            if post is not None:
                out = out * post
            o_ref[:, pr * LANES:(pr + 1) * LANES] = out.astype(BF16)

    vd_o[...] = proj(_C_DV, _C_IQ).astype(BF16)
    iq_o[...] = proj(_C_IQ, _C_IK).astype(BF16)
    ik_o[...] = proj(_C_IK, _C_IW).astype(BF16)
    iw_o[...] = proj(_C_IW, _C_END)


def _prep_call(h, g_mix, wext, gql, gkvl, wqm, wqs, wkk, wkv, gqm, gqs, gkm, gks, gdq, gdk,
               cos_t, sin_t, lp):
    t = h.shape[0]
    tm = SEQ_TILE
    nt = lp // tm
    row = lambda w: pl.BlockSpec((tm, w), lambda i: (i, 0))
    full = lambda a: pl.BlockSpec(a.shape, lambda i: (0, 0))
    tab = pl.BlockSpec((tm, LANES), lambda i: (i % nt, 0))
    hw = MLA_HEADS * LANES
    out_shape = [
        jax.ShapeDtypeStruct((t, hw), BF16), jax.ShapeDtypeStruct((t, hw), BF16),
        jax.ShapeDtypeStruct((t, MLA_HEADS * MLA_V), BF16),
        jax.ShapeDtypeStruct((t, DSA_WIDTH), BF16), jax.ShapeDtypeStruct((t, DSA_WIDTH), BF16),
        jax.ShapeDtypeStruct((t, DSA_WIDTH), BF16),
        jax.ShapeDtypeStruct((t, IDX_HEADS * IDX_DIM), BF16),
        jax.ShapeDtypeStruct((t, LANES), BF16), jax.ShapeDtypeStruct((t, LANES), F32),
    ]
    return pl.pallas_call(
        _prep_kernel,
        grid=(t // tm,),
        in_specs=[row(D_MODEL), full(g_mix), full(wext), full(gql), full(gkvl), full(wqm),
                  full(wqs), full(wkk), full(wkv), full(gqm), full(gqs), full(gkm), full(gks),
                  full(gdq), full(gdk), tab, tab],
        out_specs=[row(s.shape[1]) for s in out_shape],
        out_shape=out_shape,
        compiler_params=_cparams(("parallel",)),
        name="prep_mix_in",
    )(h, g_mix, wext, gql, gkvl, wqm, wqs, wkk, wkv, gqm, gqs, gkm, gks, gdq, gdk, cos_t, sin_t)


def _flash_init(m_scr, l_scr, acc_scr):
    m_scr[...] = jnp.full(m_scr.shape, -jnp.inf, F32)
    l_scr[...] = jnp.zeros(l_scr.shape, F32)
    acc_scr[...] = jnp.zeros(acc_scr.shape, F32)


def _lane_fold(x, op):
    return functools.reduce(op, [x[:, j * LANES:(j + 1) * LANES] for j in range(x.shape[1] // LANES)])


def _flash_update(hd, s, vc, m_scr, l_scr, acc_scr):
    m_old = m_scr[hd]
    m_new = jnp.maximum(m_old, jnp.max(_lane_fold(s, jnp.maximum), axis=-1, keepdims=True))
    alpha = jnp.exp2(m_old - m_new)
    p = jnp.exp2(s - jnp.concatenate([m_new] * (s.shape[1] // LANES), axis=1))
    m_scr[hd] = m_new
    l_scr[hd] = alpha * l_scr[hd] + _lane_fold(p, jnp.add)
    acc_scr[hd] = alpha * acc_scr[hd] + _dot(p.astype(BF16), vc)


def _flash_store(o_ref, l_scr, acc_scr, head_dim):
    lane = lax.broadcasted_iota(I32, acc_scr.shape[1:], 1)
    for pr in range(acc_scr.shape[0] // 2):
        o0, o1 = [acc_scr[hd] / jnp.sum(l_scr[hd], axis=-1, keepdims=True)
                  for hd in (2 * pr, 2 * pr + 1)]
        o_ref[:, pr * LANES:(pr + 1) * LANES] = jnp.where(lane < head_dim, o0, o1).astype(o_ref.dtype)


def _mla_kernel(q_ref, k_ref, v_ref, o_ref, m_scr, l_scr, acc_scr):
    qi = pl.program_id(1)
    tq = q_ref.shape[0]
    _flash_init(m_scr, l_scr, acc_scr)

    def chunk(c, diagonal):
        ks = pl.ds(pl.multiple_of(c * tq, tq), tq)
        for hd in range(MLA_HEADS):
            sl = slice(hd * LANES, (hd + 1) * LANES)
            s = _dot_nt(q_ref[:, sl], k_ref[ks, sl])
            if diagonal:
                row = lax.broadcasted_iota(I32, (tq, tq), 0)
                col = lax.broadcasted_iota(I32, (tq, tq), 1)
                s = jnp.where(col <= row, s, NEG_BIG)
            pr = hd // 2
            _flash_update(hd, s, v_ref[ks, pr * LANES:(pr + 1) * LANES], m_scr, l_scr, acc_scr)

    def body(c, carry):
        chunk(c, False)
        return carry

    lax.fori_loop(0, qi, body, 0)
    chunk(qi, True)
    _flash_store(o_ref, l_scr, acc_scr, MLA_V)


def _mla_call(q, k, v):
    b, lp, _ = q.shape
    tq = SEQ_TILE
    nq = lp // tq
    state = pltpu.VMEM((MLA_HEADS, tq, LANES), F32)
    return pl.pallas_call(
        _mla_kernel,
        grid=(b, nq),
        in_specs=[pl.BlockSpec((None, tq, q.shape[2]), lambda bi, i: (bi, i, 0)),
                  pl.BlockSpec((None, lp, k.shape[2]), lambda bi, i: (bi, 0, 0)),
                  pl.BlockSpec((None, lp, v.shape[2]), lambda bi, i: (bi, 0, 0))],
        out_specs=pl.BlockSpec((None, tq, v.shape[2]), lambda bi, i: (bi, i, 0)),
        out_shape=jax.ShapeDtypeStruct((b, lp, v.shape[2]), BF16),
        scratch_shapes=[state, state, state],
        compiler_params=_cparams(("parallel", "arbitrary")),
        name="mla_attention",
    )(q, k, v)


def _dsa_kernel(iq_ref, iw_ref, ik_ref, q_ref, k_ref, v_ref, tz_ref, o_ref,
                key_scr, madd_scr, iqh_scr, qh_scr, wt_scr, ans_scr, jst_scr, m_scr, l_scr, acc_scr,
                *, top_k):
    qi = pl.program_id(1)
    tq = q_ref.shape[0]
    nsub = tq // LANES
    sublanes = 8
    n_chunks = qi + 1
    lane = lax.broadcasted_iota(I32, (tq, LANES), 1)
    key_i = lax.broadcasted_iota(I32, (tq, tq), 0)
    qry_i = lax.broadcasted_iota(I32, (tq, tq), 1)
    chunk_rows = lambda c: pl.ds(pl.multiple_of(c * tq, tq), tq)

    lane_group = lax.shift_right_logical(lane, int(math.log2(IDX_DIM)))
    for hd in range(IDX_HEADS):
        quad = iq_ref[:, (hd // 4) * LANES:(hd // 4 + 1) * LANES].astype(F32)
        iqh_scr[hd] = jnp.where(lane_group == hd % 4, quad, 0.0).astype(BF16)
    for hd in range(DSA_HEADS):
        pair = q_ref[:, (hd // 2) * LANES:(hd // 2 + 1) * LANES].astype(F32)
        own = lane < DSA_HEAD_DIM if hd % 2 == 0 else lane >= DSA_HEAD_DIM
        qh_scr[hd] = jnp.where(own, pair, 0.0).astype(BF16)
    wt_scr[...] = jnp.transpose(iw_ref[...])[0:IDX_HEADS, :]

    def index_chunk(c, diagonal):
        ikc = ik_ref[chunk_rows(c), :]
        sc = jnp.zeros((tq, tq), F32)
        for hd in range(IDX_HEADS):
            sc = sc + wt_scr[hd:hd + 1, :] * jnp.maximum(_dot_nt(ikc, iqh_scr[hd]), 0.0)
        sc = jnp.where(sc == 0.0, 0.0, sc)
        if diagonal:
            sc = jnp.where(key_i <= qry_i, sc, -jnp.inf)
        bits = pltpu.bitcast(sc, I32)
        key_scr[c] = bits ^ (lax.shift_right_arithmetic(bits, 31) & 0x7FFFFFFF)

    def index_body(c, carry):
        index_chunk(c, False)
        return carry

    lax.fori_loop(0, qi, index_body, 0)
    index_chunk(qi, True)

    kf = float(top_k)
    vec = (sublanes, tq)
    sub_i = lax.broadcasted_iota(I32, vec, 0)

    def count(pred):
        def body(c, accs):
            accs = list(accs)
            kk = key_scr[c]
            for g in range(tq // sublanes):
                hit = jnp.where(pred(kk[g * sublanes:(g + 1) * sublanes, :], c * tq + g * sublanes),
                                1.0, 0.0)
                accs[g % 2] = accs[g % 2] + hit
            return tuple(accs)
        zero = jnp.zeros(vec, F32)
        a0, a1 = lax.fori_loop(0, n_chunks, body, (zero, zero))
        return jnp.broadcast_to(jnp.sum(a0 + a1, axis=0, keepdims=True), vec)

    def bisect_step(bit, ans, done):
        cand = ans + lax.shift_left(jnp.int32(1), bit)
        cnt = count(lambda kk, _i: kk >= cand)
        open_ = done == 0
        ans = jnp.where(jnp.logical_and(open_, cnt >= kf), cand, ans)
        done = jnp.where(jnp.logical_and(open_, cnt == kf), 1, done)
        return ans, done

    def n_open_of(done):
        return jnp.sum(jnp.where(done == 0, 1.0, 0.0))

    def bis_cond(st):
        bit, _, _, n_open = st
        return jnp.logical_and(bit >= 0, n_open > 0.0)

    def bis_body(st):
        bit, ans, done, _ = st
        ans, done = bisect_step(bit, ans, done)
        ans, done = bisect_step(bit - 1, ans, done)
        return bit - 2, ans, done, n_open_of(done)

    t_pos = qi * tq + lax.broadcasted_iota(I32, vec, 1)
    done0 = (t_pos + 1 <= top_k).astype(I32)
    ans0 = jnp.full(vec, INT_MIN, I32)
    _, ans, done, n_open = lax.while_loop(bis_cond, bis_body,
                                          (jnp.int32(31), ans0, done0, n_open_of(done0)))
    ans_scr[...] = ans
    jst_scr[...] = jnp.full(vec, 2 ** 31 - 1, I32)

    @pl.when(n_open > 0.0)
    def _():
        need = kf - count(lambda kk, _i: kk > ans)

        def tie_body(i, jst):
            cand = jst + lax.shift_left(jnp.int32(1), 13 - i)
            cnt = count(lambda kk, i0: jnp.logical_and(kk == ans, sub_i + i0 < cand))
            return jnp.where(cnt < need, cand, jst)

        jst = lax.fori_loop(0, 14, tie_body, jnp.zeros(vec, I32))
        jst_scr[...] = jnp.where(done == 0, jst, 2 ** 31 - 1)

    ans_row = ans_scr[0:1, :]
    jst_row = jst_scr[0:1, :]

    def mask_chunk(c, diagonal):
        kk = key_scr[c]
        sel = jnp.logical_or(kk > ans_row,
                             jnp.logical_and(kk == ans_row, key_i + c * tq <= jst_row))
        if diagonal:
            sel = jnp.logical_and(sel, key_i <= qry_i)
        madd_scr[c] = jnp.transpose(jnp.where(sel, 0.0, NEG_BIG))

    def mask_body(c, carry):
        mask_chunk(c, False)
        return carry

    lax.fori_loop(0, qi, mask_body, 0)
    mask_chunk(qi, True)

    _flash_init(m_scr, l_scr, acc_scr)

    def bias_of(hd, where_):
        near = tz_ref[hd, :, LANES:2 * LANES]
        far = tz_ref[hd, :, 0:LANES]
        z = jnp.zeros((LANES, LANES), F32)
        if where_ == "previous":
            blocks = [[far if (a == 0 and b == nsub - 1) else z for b in range(nsub)]
                      for a in range(nsub)]
        else:
            blocks = [[near if b == a else far if b == a - 1 else z for b in range(nsub)]
                      for a in range(nsub)]
        return jnp.concatenate([jnp.concatenate(r, axis=1) for r in blocks], axis=0)

    def chunk(c, where_):
        ks = chunk_rows(c)
        madd = madd_scr[c]
        for hd in range(DSA_HEADS):
            sl = slice((hd // 2) * LANES, (hd // 2 + 1) * LANES)
            s = _dot_nt(qh_scr[hd], k_ref[ks, sl]) + madd
            if where_ is not None:
                s = s + bias_of(hd, where_)
            _flash_update(hd, s, v_ref[ks, sl], m_scr, l_scr, acc_scr)

    def body(c, carry):
        chunk(c, None)
        return carry

    lax.fori_loop(0, jnp.maximum(qi - 1, 0), body, 0)

    @pl.when(qi >= 1)
    def _():
        chunk(qi - 1, "previous")

    chunk(qi, "diagonal")
    _flash_store(o_ref, l_scr, acc_scr, DSA_HEAD_DIM)


def _dsa_call(iq, iw, ik, q, k, v, tz, top_k):
    b, lp, _ = q.shape
    tq = SEQ_TILE
    nq = lp // tq
    qspec = lambda w: pl.BlockSpec((None, tq, w), lambda bi, i: (bi, i, 0))
    kspec = lambda w: pl.BlockSpec((None, lp, w), lambda bi, i: (bi, 0, 0))
    state = pltpu.VMEM((DSA_HEADS, tq, LANES), F32)
    return pl.pallas_call(
        functools.partial(_dsa_kernel, top_k=top_k),
        grid=(b, nq),
        in_specs=[qspec(iq.shape[2]), qspec(LANES), kspec(LANES), qspec(DSA_WIDTH),
                  kspec(DSA_WIDTH), kspec(DSA_WIDTH),
                  pl.BlockSpec(tz.shape, lambda bi, i: (0, 0, 0))],
        out_specs=qspec(DSA_WIDTH),
        out_shape=jax.ShapeDtypeStruct((b, lp, DSA_WIDTH), BF16),
        scratch_shapes=[pltpu.VMEM((nq, tq, tq), I32), pltpu.VMEM((nq, tq, tq), F32),
                        pltpu.VMEM((IDX_HEADS, tq, LANES), BF16),
                        pltpu.VMEM((DSA_HEADS, tq, LANES), BF16),
                        pltpu.VMEM((IDX_HEADS, tq), F32),
                        pltpu.VMEM((8, tq), I32), pltpu.VMEM((8, tq), I32),
                        state, state, state],
        compiler_params=_cparams(("parallel", "arbitrary")),
        name="dsa_attention",
    )(iq, iw, ik, q, k, v, tz)


def _mix_out_kernel(h_ref, a_ref, b_ref, wa_ref, wb_ref, o_ref):
    o_ref[...] = h_ref[...] + _dot(a_ref[...], wa_ref[...]) + _dot(b_ref[...], wb_ref[...])


def _mix_out_call(h, a, b, wa, wb):
    t = h.shape[0]
    tm = _row_tile(t)
    row = lambda w: pl.BlockSpec((tm, w), lambda i: (i, 0))
    full = lambda x: pl.BlockSpec(x.shape, lambda i: (0, 0))
    return pl.pallas_call(
        _mix_out_kernel,
        grid=(t // tm,),
        in_specs=[row(D_MODEL), row(a.shape[1]), row(b.shape[1]), full(wa), full(wb)],
        out_specs=row(D_MODEL),
        out_shape=jax.ShapeDtypeStruct(h.shape, F32),
        compiler_params=_cparams(("parallel",)),
        name="mix_out",
    )(h, a, b, wa, wb)


def _swiglu_chunk(xb, w1_ref, w3_ref, w2_ref):
    a = _dot(xb, w1_ref[...].astype(BF16))
    act = (a * jax.nn.sigmoid(a)) * _dot(xb, w3_ref[...].astype(BF16))
    return _dot(act.astype(BF16), w2_ref[...].astype(BF16))


def _ffn_kernel(h_ref, g_ref, w1_ref, w3_ref, w2_ref, o_ref, xn_scr, acc_scr):
    f = pl.program_id(1)

    @pl.when(f == 0)
    def _():
        xn_scr[...] = _rms(h_ref[...], g_ref[...]).astype(BF16)
        acc_scr[...] = jnp.zeros(acc_scr.shape, F32)

    acc_scr[...] += _swiglu_chunk(xn_scr[...], w1_ref, w3_ref, w2_ref)

    @pl.when(f == pl.num_programs(1) - 1)
    def _():
        o_ref[...] = h_ref[...] + acc_scr[...]


def _ffn_call(h, g, w1, w3, w2):
    t = h.shape[0]
    d_ff = w1.shape[1]
    tm = _row_tile(t)
    tf = 512 if d_ff % 512 == 0 else d_ff
    return pl.pallas_call(
        _ffn_kernel,
        grid=(t // tm, d_ff // tf),
        in_specs=[pl.BlockSpec((tm, D_MODEL), lambda i, f: (i, 0)),
                  pl.BlockSpec(g.shape, lambda i, f: (0, 0)),
                  pl.BlockSpec((D_MODEL, tf), lambda i, f: (0, f)),
                  pl.BlockSpec((D_MODEL, tf), lambda i, f: (0, f)),
                  pl.BlockSpec((tf, D_MODEL), lambda i, f: (f, 0))],
        out_specs=pl.BlockSpec((tm, D_MODEL), lambda i, f: (i, 0)),
        out_shape=jax.ShapeDtypeStruct(h.shape, F32),
        scratch_shapes=[pltpu.VMEM((tm, D_MODEL), BF16), pltpu.VMEM((tm, D_MODEL), F32)],
        compiler_params=_cparams(("parallel", "arbitrary")),
        name="dense_swiglu",
    )(h, g, w1, w3, w2)


def _conv_kernel(h_ref, g_ref, win_ref, cw_ref, wout_ref, o_ref, z_scr):
    i = pl.program_id(1)
    tm = h_ref.shape[0]
    halo = 8
    x = h_ref[...]
    xn = _rms(x, g_ref[...]).astype(BF16)
    c_gate = _dot(xn, win_ref[:, D_MODEL:2 * D_MODEL])
    u = _dot(xn, win_ref[:, 2 * D_MODEL:3 * D_MODEL])
    z = c_gate * u

    @pl.when(i == 0)
    def _():
        z_scr[0:halo, :] = jnp.zeros((halo, D_MODEL), F32)

    @pl.when(i > 0)
    def _():
        z_scr[0:halo, :] = z_scr[tm:tm + halo, :]

    z_scr[halo:halo + tm, :] = z
    cw = cw_ref[...]
    y = (cw[0:1, :] * z_scr[halo - 2:halo - 2 + tm, :]
         + cw[1:2, :] * z_scr[halo - 1:halo - 1 + tm, :]
         + cw[2:3, :] * z)
    b_gate = _dot(xn, win_ref[:, 0:D_MODEL])
    o_ref[...] = x + _dot((b_gate * y).astype(BF16), wout_ref[...])


def _conv_call(h3, g, w_in, cw, w_out):
    b, lp, d = h3.shape
    tm = SEQ_TILE
    full = lambda a: pl.BlockSpec(a.shape, lambda bi, i: (0,) * a.ndim)
    return pl.pallas_call(
        _conv_kernel,
        grid=(b, lp // tm),
        in_specs=[pl.BlockSpec((None, tm, d), lambda bi, i: (bi, i, 0)), full(g), full(w_in),
                  full(cw), full(w_out)],
        out_specs=pl.BlockSpec((None, tm, d), lambda bi, i: (bi, i, 0)),
        out_shape=jax.ShapeDtypeStruct(h3.shape, F32),
        scratch_shapes=[pltpu.VMEM((tm + 8, d), F32)],
        compiler_params=_cparams(("arbitrary", "arbitrary")),
        name="short_conv_mixer",
    )(h3, g, w_in, cw, w_out)


def _route_kernel(h_ref, g_ref, wr_ref, xg_ref, didx_ref, cnt_ref,
                  xs_scr, base_scr, dvm_scr, dsm_scr, sem_idx, sem_rows, *, stride):
    i = pl.program_id(0)
    tm = h_ref.shape[0]
    lane = lax.broadcasted_iota(I32, (tm, LANES), 1)
    lane_f = lane.astype(F32)

    @pl.when(i == 0)
    def _():
        base_scr[...] = jnp.zeros(base_scr.shape, F32)

    xnf = _rms(h_ref[...], g_ref[...])
    logits = jnp.where(lane < N_EXPERTS, _dot(xnf.astype(BF16), wr_ref[...]), -jnp.inf)
    v1 = jnp.max(logits, axis=-1, keepdims=True)
    i1 = jnp.min(jnp.where(logits == v1, lane_f, float(LANES)), axis=-1, keepdims=True)
    rest = jnp.where(lane_f == i1, -jnp.inf, logits)
    v2 = jnp.max(rest, axis=-1, keepdims=True)
    i2 = jnp.min(jnp.where(rest == v2, lane_f, float(LANES)), axis=-1, keepdims=True)
    e2 = jnp.exp(v2 - v1)
    den = 1.0 + e2
    gates = (1.0 / den, e2 / den)

    oh1 = jnp.where(lane_f == i1, 1.0, 0.0)
    oh2 = jnp.where(lane_f == i2, 1.0, 0.0)
    oh = oh1 + oh2
    earlier = (lax.broadcasted_iota(I32, (tm, tm), 1) < lax.broadcasted_iota(I32, (tm, tm), 0))
    prefix = _dot(jnp.where(earlier, 1.0, 0.0).astype(BF16), oh.astype(BF16))
    pos = base_scr[0:1, :] + prefix
    d1 = jnp.sum(oh1 * pos, axis=-1, keepdims=True) + i1 * float(stride)
    d2 = jnp.sum(oh2 * pos, axis=-1, keepdims=True) + i2 * float(stride)
    base_scr[...] = base_scr[...] + jnp.sum(oh, axis=0, keepdims=True)
    cnt_ref[...] = base_scr[...]

    slot = lax.rem(i, 2)
    for s in range(TOP_K):
        xs_scr[slot, s, :, 0:D_MODEL] = xnf
        xs_scr[slot, s, :, D_MODEL:] = jnp.broadcast_to(gates[s], (tm, LANES))

    dmat = jnp.where(lane == 0, d1, jnp.where(lane == 1, d2, 0.0))
    dvm_scr[...] = jnp.transpose(dmat)[0:8, :].astype(I32)
    didx_ref[...] = dvm_scr[...]
    cp = pltpu.make_async_copy(dvm_scr, dsm_scr, sem_idx)
    cp.start()
    cp.wait()

    for r in range(tm):
        for s in range(TOP_K):
            pltpu.make_async_copy(xs_scr.at[slot, s, pl.ds(r, 1)],
                                  xg_ref.at[pl.ds(dsm_scr[s, r], 1)],
                                  sem_rows.at[slot]).start(priority=s)

    def wait_rows(sl):
        for s in range(TOP_K):
            pltpu.make_async_copy(xs_scr.at[sl, s], xg_ref.at[pl.ds(0, tm)], sem_rows.at[sl]).wait()

    @pl.when(i > 0)
    def _():
        wait_rows(1 - slot)

    @pl.when(i == pl.num_programs(0) - 1)
    def _():
        wait_rows(slot)
        xs_scr[0, 0] = jnp.zeros(xs_scr.shape[2:], F32)
        dvm_scr[:, 0:LANES] = base_scr[...].astype(I32)
        cp2 = pltpu.make_async_copy(dvm_scr, dsm_scr, sem_idx)
        cp2.start()
        cp2.wait()
        sublanes = 8
        for phase in ("start", "wait"):
            for e in range(N_EXPERTS):
                n_e = dsm_scr[0, e]
                n_up = lax.shift_left(lax.shift_right_logical(n_e + sublanes - 1, 3), 3)
                for k in range(sublanes - 1):
                    @pl.when(n_e + k < n_up)
                    def _():
                        row = pltpu.make_async_copy(
                            xs_scr.at[0, 0, pl.ds(0, 1)],
                            xg_ref.at[pl.ds(e * stride + n_e + k, 1)], sem_rows.at[0])
                        row.start() if phase == "start" else row.wait()
                for k in range(EXPERT_TILE // tm):
                    blk = pltpu.make_async_copy(
                        xs_scr.at[0, 0],
                        xg_ref.at[pl.ds(pl.multiple_of(e * stride + n_up + k * tm, sublanes), tm)],
                        sem_rows.at[0])
                    blk.start() if phase == "start" else blk.wait()


def _route_call(h, g, wr, stride):
    t = h.shape[0]
    tm = MOE_TILE
    return pl.pallas_call(
        functools.partial(_route_kernel, stride=stride),
        grid=(t // tm,),
        in_specs=[pl.BlockSpec((tm, D_MODEL), lambda i: (i, 0)),
                  pl.BlockSpec(g.shape, lambda i: (0, 0)),
                  pl.BlockSpec(wr.shape, lambda i: (0, 0))],
        out_specs=[pl.BlockSpec(memory_space=pl.ANY),
                   pl.BlockSpec((8, tm), lambda i: (0, i)),
                   pl.BlockSpec((8, LANES), lambda i: (0, 0))],
        out_shape=[jax.ShapeDtypeStruct((N_EXPERTS * stride, XG_WIDTH), F32),
                   jax.ShapeDtypeStruct((8, t), I32),
                   jax.ShapeDtypeStruct((8, LANES), F32)],
        scratch_shapes=[pltpu.VMEM((2, TOP_K, tm, XG_WIDTH), F32), pltpu.VMEM((8, LANES), F32),
                        pltpu.VMEM((8, tm), I32), pltpu.SMEM((8, tm), I32),
                        pltpu.SemaphoreType.DMA, pltpu.SemaphoreType.DMA((2,))],
        compiler_params=_cparams(("arbitrary",)),
        name="moe_route",
    )(h, g, wr)


def _expert_kernel(blk_ref, exp_ref, nu_ref, x_ref, w1_ref, w3_ref, w2_ref, o_ref, xb_scr, acc_scr):
    j = pl.program_id(0)
    f = pl.program_id(1)

    @pl.when(j < nu_ref[0])
    def _():
        @pl.when(f == 0)
        def _():
            xb_scr[...] = x_ref[:, 0:D_MODEL].astype(BF16)
            acc_scr[...] = jnp.zeros(acc_scr.shape, F32)

        acc_scr[...] += _swiglu_chunk(xb_scr[...], w1_ref, w3_ref, w2_ref)

        @pl.when(f == pl.num_programs(1) - 1)
        def _():
            gate = x_ref[:, D_MODEL:]
            o_ref[...] = acc_scr[...] * jnp.concatenate([gate] * (D_MODEL // LANES), axis=1)


def _expert_call(tile_blk, tile_exp, n_used, xg, w1, w3, w2):
    d_ff = w1.shape[2]
    tm = EXPERT_TILE
    tf = 512 if d_ff % 512 == 0 else d_ff
    nf = d_ff // tf
    n_tiles = tile_blk.shape[0]
    fe = lambda j, f, nu: jnp.where(j < nu[0], f, nf - 1)
    grid_spec = pltpu.PrefetchScalarGridSpec(
        num_scalar_prefetch=3,
        grid=(n_tiles, nf),
        in_specs=[pl.BlockSpec((tm, XG_WIDTH), lambda j, f, blk, ex, nu: (blk[j], 0)),
                  pl.BlockSpec((None, D_MODEL, tf), lambda j, f, blk, ex, nu: (ex[j], 0, fe(j, f, nu))),
                  pl.BlockSpec((None, D_MODEL, tf), lambda j, f, blk, ex, nu: (ex[j], 0, fe(j, f, nu))),
                  pl.BlockSpec((None, tf, D_MODEL), lambda j, f, blk, ex, nu: (ex[j], fe(j, f, nu), 0))],
        out_specs=pl.BlockSpec((tm, D_MODEL), lambda j, f, blk, ex, nu: (blk[j], 0)),
        scratch_shapes=[pltpu.VMEM((tm, D_MODEL), BF16), pltpu.VMEM((tm, D_MODEL), F32)])
    return pl.pallas_call(
        _expert_kernel,
        grid_spec=grid_spec,
        out_shape=jax.ShapeDtypeStruct((xg.shape[0], D_MODEL), F32),
        compiler_params=_cparams(("arbitrary", "arbitrary")),
        name="expert_swiglu",
    )(tile_blk, tile_exp, n_used, xg, w1, w3, w2)


def _combine_kernel(h_ref, didx_ref, yg_ref, o_ref, y_scr, dsm_scr, sem_idx, sem_rows):
    i = pl.program_id(0)
    tm = h_ref.shape[0]
    slot = lax.rem(i, 2)

    def fetch(j, sl):
        cp = pltpu.make_async_copy(didx_ref.at[:, pl.ds(pl.multiple_of(j * tm, tm), tm)], dsm_scr,
                                   sem_idx)
        cp.start()
        cp.wait()
        for r in range(tm):
            for s in range(TOP_K):
                pltpu.make_async_copy(yg_ref.at[pl.ds(dsm_scr[s, r], 1)],
                                      y_scr.at[sl, s, pl.ds(r, 1)],
                                      sem_rows.at[sl]).start(priority=s)

    @pl.when(i == 0)
    def _():
        fetch(0, 0)

    @pl.when(i + 1 < pl.num_programs(0))
    def _():
        fetch(i + 1, 1 - slot)

    for s in range(TOP_K):
        pltpu.make_async_copy(yg_ref.at[pl.ds(0, tm)], y_scr.at[slot, s], sem_rows.at[slot]).wait()
    o_ref[...] = h_ref[...] + y_scr[slot, 0] + y_scr[slot, 1]


def _combine_call(h, didx, yg):
    t = h.shape[0]
    tm = MOE_TILE
    return pl.pallas_call(
        _combine_kernel,
        grid=(t // tm,),
        in_specs=[pl.BlockSpec((tm, D_MODEL), lambda i: (i, 0)),
                  pl.BlockSpec(memory_space=pl.ANY), pl.BlockSpec(memory_space=pl.ANY)],
        out_specs=pl.BlockSpec((tm, D_MODEL), lambda i: (i, 0)),
        out_shape=jax.ShapeDtypeStruct(h.shape, F32),
        scratch_shapes=[pltpu.VMEM((2, TOP_K, tm, D_MODEL), F32), pltpu.SMEM((8, tm), I32),
                        pltpu.SemaphoreType.DMA, pltpu.SemaphoreType.DMA((2,))],
        compiler_params=_cparams(("arbitrary",)),
        name="moe_combine",
    )(h, didx, yg)


def _moe_call(h, g, wr, w1, w3, w2):
    t = h.shape[0]
    tm = EXPERT_TILE
    assert t % MOE_TILE == 0 and tm % MOE_TILE == 0
    stride = -(-t // tm) * tm + tm
    xg, didx, cnt = _route_call(h, g, wr, stride)
    counts = cnt[0, :N_EXPERTS].astype(I32)
    tiles_e = (counts + tm - 1) // tm
    cum = jnp.cumsum(tiles_e)
    n_used = cum[-1]
    n_tiles = -(-TOP_K * t // tm) + N_EXPERTS
    jj = jnp.minimum(jnp.arange(n_tiles, dtype=I32), n_used - 1)
    tile_exp = jnp.sum((jj[:, None] >= cum[None, :]).astype(I32), axis=1)
    tile_blk = tile_exp * (stride // tm) + jj - (cum - tiles_e)[tile_exp]
    yg = _expert_call(tile_blk, tile_exp, n_used[None], xg, w1, w3, w2)
    return _combine_call(h, didx, yg)


def _mix_in_weights(w):
    offs = np.concatenate([[0], np.cumsum(MIX_IN_SIZES)])
    cq, ckv, kr, dq, dk, dv, iq, ik, iw = [w[:, offs[j]:offs[j + 1]] for j in range(9)]
    z = lambda n: jnp.zeros((w.shape[0], n), w.dtype)
    half = MLA_ROPE // 2
    kr_main = jnp.concatenate([z(MLA_NOPE), kr, z(LANES - MLA_QK)], axis=1)
    kr_swap = jnp.concatenate([z(MLA_NOPE), kr[:, half:], kr[:, :half], z(LANES - MLA_QK)], axis=1)
    ik4 = jnp.concatenate([ik] * (LANES // IDX_DIM), axis=1)
    iw_p = jnp.concatenate([iw, z(LANES - IDX_HEADS)], axis=1)
    return jnp.concatenate([cq, ckv, kr_main, kr_swap, dq, dk, dv, iq, ik4, iw_p],
                           axis=1).astype(BF16)


def _mla_q_weights(w_uq):
    r = w_uq.shape[0]
    w = w_uq.reshape(r, MLA_HEADS, MLA_QK)
    nope, rope = w[..., :MLA_NOPE], w[..., MLA_NOPE:]
    half = MLA_ROPE // 2
    z = lambda n: jnp.zeros((r, MLA_HEADS, n), w.dtype)
    main = jnp.concatenate([nope, rope, z(LANES - MLA_QK)], axis=-1)
    swap = jnp.concatenate([z(MLA_NOPE), rope[..., half:], rope[..., :half], z(LANES - MLA_QK)],
                           axis=-1)
    return (main.reshape(r, MLA_HEADS * LANES).astype(BF16),
            swap.reshape(r, MLA_HEADS * LANES).astype(BF16))


def _mla_kv_weights(w_ukv):
    r = w_ukv.shape[0]
    w = w_ukv.reshape(r, MLA_HEADS, MLA_NOPE + MLA_V)
    k_nope = jnp.concatenate([w[..., :MLA_NOPE], jnp.zeros((r, MLA_HEADS, LANES - MLA_NOPE), w.dtype)],
                             axis=-1)
    return (k_nope.reshape(r, MLA_HEADS * LANES).astype(BF16),
            w[..., MLA_NOPE:].reshape(r, MLA_HEADS * MLA_V).astype(BF16))


def _qk_gains(g):
    half = MLA_ROPE // 2
    z = lambda n: jnp.zeros((n,), g.dtype)
    main = jnp.concatenate([g, z(LANES - MLA_QK)])
    swap = jnp.concatenate([z(MLA_NOPE), g[MLA_NOPE + half:], g[MLA_NOPE:MLA_NOPE + half],
                            z(LANES - MLA_QK)])
    return main[None, :], swap[None, :]


def _rope_tables(lp):
    half = MLA_ROPE // 2
    inv = ROPE_BASE ** (-jnp.arange(half, dtype=F32) / half)
    ang = jnp.arange(lp, dtype=jnp.int32).astype(F32)[:, None] * inv[None, :]
    cos, sin = jnp.cos(ang), jnp.sin(ang)
    ones = jnp.ones((lp, MLA_NOPE), F32)
    pad1 = jnp.ones((lp, LANES - MLA_QK), F32)
    zeros = jnp.zeros((lp, MLA_NOPE), F32)
    pad0 = jnp.zeros((lp, LANES - MLA_QK), F32)
    return (jnp.concatenate([ones, cos, cos, pad1], axis=1),
            jnp.concatenate([zeros, -sin, sin, pad0], axis=1))


def _rel_buckets(n):
    max_exact = REL_BUCKETS // 2
    d = np.arange(n)
    df = np.maximum(d, 1).astype(np.float32)
    large = max_exact + (np.log(df / np.float32(max_exact))
                         / np.float32(math.log(REL_MAX_DIST / max_exact))
                         * np.float32(REL_BUCKETS - max_exact)).astype(np.int32)
    large = np.minimum(large, REL_BUCKETS - 1)
    return np.where(d < max_exact, d, large)


def _bias_tiles(rel_bias):
    buckets = _rel_buckets(2 * LANES)
    assert np.all(buckets[LANES - 1:] == REL_BUCKETS - 1)
    n_heads = rel_bias.shape[1]
    shifted = (rel_bias - rel_bias[REL_BUCKETS - 1:REL_BUCKETS, :]) * LOG2E
    period = 3 * LANES
    w = jnp.concatenate([jnp.take(shifted, jnp.asarray(buckets), axis=0),
                         jnp.zeros((period - 2 * LANES, n_heads), F32)], axis=0)
    u = jnp.roll(jnp.flip(w, axis=0), -(2 * LANES - 1), axis=0).T
    flat = jnp.tile(u, (1, LANES))[:, :LANES * (period - 1)]
    return flat.reshape(n_heads, LANES, period - 1)[:, :, :2 * LANES].astype(F32)


def kernel(x, meta_tokens, rel_bias, ev_norm_mix, ev_w_mix_in, ev_g_q_lat, ev_g_kv_lat, ev_w_uq,
           ev_w_ukv, ev_mla_q_norm, ev_mla_k_norm, ev_dsa_q_norm, ev_dsa_k_norm, ev_w_mix_out,
           ev_norm_ffn, ev_w1, ev_w3, ev_w2, od_norm_mix, od_w_in, od_conv_w, od_w_out,
           od_norm_ffn, od_w_router, od_w1, od_w3, od_w2):
    b, seq, d = x.shape
    assert d == D_MODEL
    l_tot = seq + N_META
    lp = -(-l_tot // BLOCK_Q) * BLOCK_Q
    assert lp % SEQ_TILE == 0, "sequence tiling assumes the padded length is a multiple of 384"
    top_k = min(DSA_TOPK_MAX, l_tot // 4)
    depth = ev_norm_mix.shape[0] + od_norm_mix.shape[0]

    meta = jnp.broadcast_to(meta_tokens[None].astype(x.dtype), (b, N_META, d))
    h = jnp.concatenate([meta, x, jnp.zeros((b, lp - l_tot, d), x.dtype)], axis=1)
    h = h.reshape(b * lp, d)

    cos_t, sin_t = _rope_tables(lp)
    tz = _bias_tiles(rel_bias)
    row2 = lambda v: v[None, :]

    for layer in range(depth):
        i = layer // 2
        if layer % 2 == 0:
            wqm, wqs = _mla_q_weights(ev_w_uq[i])
            wkk, wkv = _mla_kv_weights(ev_w_ukv[i])
            gqm, gqs = _qk_gains(ev_mla_q_norm[i])
            gkm, gks = _qk_gains(ev_mla_k_norm[i])
            gdq = row2(jnp.concatenate([ev_dsa_q_norm[i]] * 2))
            gdk = row2(jnp.concatenate([ev_dsa_k_norm[i]] * 2))
            qm, km, vm, qd, kd, vd, iq, ik, iw = _prep_call(
                h, row2(ev_norm_mix[i]), _mix_in_weights(ev_w_mix_in[i]), row2(ev_g_q_lat[i]),
                row2(ev_g_kv_lat[i]), wqm, wqs, wkk, wkv, gqm, gqs, gkm, gks, gdq, gdk,
                cos_t, sin_t, lp)
            seq3 = lambda a: a.reshape(b, lp, a.shape[1])
            o_mla = _mla_call(seq3(qm), seq3(km), seq3(vm))
            o_dsa = _dsa_call(seq3(iq), seq3(iw), seq3(ik), seq3(qd), seq3(kd), seq3(vd), tz, top_k)
            w_o = ev_w_mix_out[i].astype(BF16)
            n_mla = MLA_HEADS * MLA_V
            h = _mix_out_call(h, o_mla.reshape(b * lp, -1), o_dsa.reshape(b * lp, -1),
                              w_o[:n_mla], w_o[n_mla:])
            h = _ffn_call(h, row2(ev_norm_ffn[i]), ev_w1[i], ev_w3[i], ev_w2[i])
        else:
            h = _conv_call(h.reshape(b, lp, d), row2(od_norm_mix[i]), od_w_in[i].astype(BF16),
                           od_conv_w[i].reshape(CONV_WIDTH, d), od_w_out[i].astype(BF16))
            h = h.reshape(b * lp, d)
            wr = jnp.concatenate(
                [od_w_router[i], jnp.zeros((d, LANES - N_EXPERTS), od_w_router.dtype)], axis=1)
            h = _moe_call(h, row2(od_norm_ffn[i]), wr.astype(BF16), od_w1[i], od_w3[i], od_w2[i])
    return h.reshape(b, lp, d)[:, N_META:l_tot]
```

```python
import functools
import math

import numpy as np
import jax
import jax.numpy as jnp
from jax import lax
from jax.experimental import pallas as pl
from jax.experimental.pallas import tpu as pltpu

F32 = jnp.float32
BF16 = jnp.bfloat16
I32 = jnp.int32

D_MODEL = 1024
N_META = 16
BLOCK_Q = 128
EPS = 1e-6
MLA_HEADS = 8
MLA_Q_RANK = 384
MLA_KV_RANK = 256
MLA_NOPE = 64
MLA_ROPE = 32
MLA_V = 64
MLA_QK = MLA_NOPE + MLA_ROPE
MLA_SCALE = MLA_QK ** -0.5
ROPE_BASE = 10000.0
DSA_HEADS = 8
DSA_HEAD_DIM = 64
DSA_WIDTH = DSA_HEADS * DSA_HEAD_DIM
DSA_SCALE = DSA_HEAD_DIM ** -0.5
IDX_HEADS = 8
IDX_DIM = 32
DSA_TOPK_MAX = 256
REL_BUCKETS = 32
REL_MAX_DIST = 128
MIX_IN_SIZES = (MLA_Q_RANK, MLA_KV_RANK, MLA_ROPE, DSA_WIDTH, DSA_WIDTH, DSA_WIDTH,
                IDX_HEADS * IDX_DIM, IDX_DIM, IDX_HEADS)
CONV_WIDTH = 3
N_EXPERTS = 8
TOP_K = 2

LANES = 128
VMEM_LIMIT_BYTES = 56 * 1024 * 1024

SEQ_TILE = 3 * LANES
MOE_TILE = 512
EXPERT_TILE = 2 * MOE_TILE
QK_LOOKAHEAD = 2
XG_WIDTH = D_MODEL + LANES
LOG2E = math.log2(math.e)
NEG_BIG = -1e30
INT_MIN = -2 ** 31

_C_CQ = 0
_C_CKV = _C_CQ + MLA_Q_RANK
_C_KRM = _C_CKV + MLA_KV_RANK
_C_KRS = _C_KRM + LANES
_C_DQ = _C_KRS + LANES
_C_DK = _C_DQ + DSA_WIDTH
_C_DV = _C_DK + DSA_WIDTH
_C_IQ = _C_DV + DSA_WIDTH
_C_IK = _C_IQ + IDX_HEADS * IDX_DIM
_C_IW = _C_IK + LANES
_C_END = _C_IW + LANES


def _cparams(sem):
    return pltpu.CompilerParams(dimension_semantics=sem, vmem_limit_bytes=VMEM_LIMIT_BYTES)


def _row_tile(n_rows, candidates=(1024, 768, 512, 384, 256, 128)):
    for c in candidates:
        if n_rows % c == 0:
            return c
    raise ValueError(f"no row tile for {n_rows}")


def _rms(x, g):
    ms = jnp.mean(x * x, axis=-1, keepdims=True)
    return x * lax.rsqrt(ms + EPS) * g


def _dot(a, b):
    return jnp.dot(a, b, preferred_element_type=F32)


def _dot_nt(a, b):
    return lax.dot_general(a, b, (((1,), (1,)), ((), ())), preferred_element_type=F32)


def _prep_kernel(h_ref, g_ref, wext_ref, gql_ref, gkvl_ref, wqm_ref, wqs_ref, wkk_ref, wkv_ref,
                 gqm_ref, gqs_ref, gkm_ref, gks_ref, gdq_ref, gdk_ref, cos_ref, sin_ref,
                 qm_o, km_o, vm_o, qd_o, kd_o, vd_o, iq_o, ik_o, iw_o):
    xn = _rms(h_ref[...], g_ref[...]).astype(BF16)

    def proj(lo, hi):
        return _dot(xn, wext_ref[:, lo:hi])

    cos = cos_ref[...]
    sin = sin_ref[...]
    lane = lax.broadcasted_iota(I32, (xn.shape[0], LANES), 1)

    cqn = _rms(proj(_C_CQ, _C_CKV), gql_ref[...]).astype(BF16)
    q_main = _dot(cqn, wqm_ref[...])
    q_swap = _dot(cqn, wqs_ref[...])
    for hd in range(MLA_HEADS):
        sl = slice(hd * LANES, (hd + 1) * LANES)
        a = q_main[:, sl]
        r = lax.rsqrt(jnp.sum(a * a, axis=-1, keepdims=True) * (1.0 / MLA_QK) + EPS)
        out = (a * r * gqm_ref[...]) * cos + (q_swap[:, sl] * r * gqs_ref[...]) * sin
        qm_o[:, sl] = (out * (MLA_SCALE * LOG2E)).astype(BF16)

    ckvn = _rms(proj(_C_CKV, _C_KRM), gkvl_ref[...]).astype(BF16)
    k_nope = _dot(ckvn, wkk_ref[...])
    vm_o[...] = _dot(ckvn, wkv_ref[...]).astype(BF16)
    kr_main = proj(_C_KRM, _C_KRS)
    kr_swap = proj(_C_KRS, _C_DQ)
    for hd in range(MLA_HEADS):
        sl = slice(hd * LANES, (hd + 1) * LANES)
        a = k_nope[:, sl] + kr_main
        r = lax.rsqrt(jnp.sum(a * a, axis=-1, keepdims=True) * (1.0 / MLA_QK) + EPS)
        out = (a * r * gkm_ref[...]) * cos + (kr_swap * r * gks_ref[...]) * sin
        km_o[:, sl] = out.astype(BF16)

    first = lane < DSA_HEAD_DIM
    for (lo, g2_ref, o_ref, post) in ((_C_DQ, gdq_ref, qd_o, DSA_SCALE * LOG2E),
                                      (_C_DK, gdk_ref, kd_o, None)):
        for pr in range(DSA_HEADS // 2):
            x = proj(lo + pr * LANES, lo + (pr + 1) * LANES)
            sq = x * x
            s0 = jnp.sum(jnp.where(first, sq, 0.0), axis=-1, keepdims=True)
            s1 = jnp.sum(jnp.where(first, 0.0, sq), axis=-1, keepdims=True)
            r0 = lax.rsqrt(s0 * (1.0 / DSA_HEAD_DIM) + EPS)
            r1 = lax.rsqrt(s1 * (1.0 / DSA_HEAD_DIM) + EPS)
            out = x * jnp.where(first, r0, r1) * g2_ref[...]
            if post is None:
                o_ref[:, pr * LANES:(pr + 1) * LANES] = out.astype(BF16)
            else:
                out = out * post
                o_ref[:, (2 * pr) * LANES:(2 * pr + 1) * LANES] = jnp.where(first, out, 0.0).astype(BF16)
                o_ref[:, (2 * pr + 1) * LANES:(2 * pr + 2) * LANES] = jnp.where(first, 0.0, out).astype(BF16)

    vd_o[...] = proj(_C_DV, _C_IQ).astype(BF16)
    lane_group = lax.shift_right_logical(lane, int(math.log2(IDX_DIM)))
    for quad in range(IDX_HEADS * IDX_DIM // LANES):
        x = proj(_C_IQ + quad * LANES, _C_IQ + (quad + 1) * LANES)
        for j in range(LANES // IDX_DIM):
            hd = quad * (LANES // IDX_DIM) + j
            iq_o[:, hd * LANES:(hd + 1) * LANES] = jnp.where(lane_group == j, x, 0.0).astype(BF16)
    ik_o[...] = proj(_C_IK, _C_IW).astype(BF16)
    iw_o[...] = proj(_C_IW, _C_END)


def _prep_call(h, g_mix, wext, gql, gkvl, wqm, wqs, wkk, wkv, gqm, gqs, gkm, gks, gdq, gdk,
               cos_t, sin_t, lp):
    t = h.shape[0]
    tm = SEQ_TILE
    nt = lp // tm
    row = lambda w: pl.BlockSpec((tm, w), lambda i: (i, 0))
    full = lambda a: pl.BlockSpec(a.shape, lambda i: (0, 0))
    tab = pl.BlockSpec((tm, LANES), lambda i: (i % nt, 0))
    hw = MLA_HEADS * LANES
    out_shape = [
        jax.ShapeDtypeStruct((t, hw), BF16), jax.ShapeDtypeStruct((t, hw), BF16),
        jax.ShapeDtypeStruct((t, MLA_HEADS * MLA_V), BF16),
        jax.ShapeDtypeStruct((t, DSA_HEADS * LANES), BF16), jax.ShapeDtypeStruct((t, DSA_WIDTH), BF16),
        jax.ShapeDtypeStruct((t, DSA_WIDTH), BF16),
        jax.ShapeDtypeStruct((t, IDX_HEADS * LANES), BF16),
        jax.ShapeDtypeStruct((t, LANES), BF16), jax.ShapeDtypeStruct((t, LANES), F32),
    ]
    return pl.pallas_call(
        _prep_kernel,
        grid=(t // tm,),
        in_specs=[row(D_MODEL), full(g_mix), full(wext), full(gql), full(gkvl), full(wqm),
                  full(wqs), full(wkk), full(wkv), full(gqm), full(gqs), full(gkm), full(gks),
                  full(gdq), full(gdk), tab, tab],
        out_specs=[row(s.shape[1]) for s in out_shape],
        out_shape=out_shape,
        compiler_params=_cparams(("parallel",)),
        name="prep_mix_in",
    )(h, g_mix, wext, gql, gkvl, wqm, wqs, wkk, wkv, gqm, gqs, gkm, gks, gdq, gdk, cos_t, sin_t)


def _flash_init(m_scr, l_scr, acc_scr):
    m_scr[...] = jnp.full(m_scr.shape, -jnp.inf, F32)
    l_scr[...] = jnp.zeros(l_scr.shape, F32)
    acc_scr[...] = jnp.zeros(acc_scr.shape, F32)


def _lane_fold(x, op):
    return functools.reduce(op, [x[:, j * LANES:(j + 1) * LANES] for j in range(x.shape[1] // LANES)])


def _flash_update(hd, s, vc, m_scr, l_scr, acc_scr):
    m_old = m_scr[hd]
    m_new = jnp.maximum(m_old, jnp.max(_lane_fold(s, jnp.maximum), axis=-1, keepdims=True))
    alpha = jnp.exp2(m_old - m_new)
    p = jnp.exp2(s - jnp.concatenate([m_new] * (s.shape[1] // LANES), axis=1))
    m_scr[hd] = m_new
    l_scr[hd] = alpha * l_scr[hd] + _lane_fold(p, jnp.add)
    acc_scr[hd] = alpha * acc_scr[hd] + _dot(p.astype(BF16), vc)


def _flash_store(o_ref, l_scr, acc_scr, head_dim):
    lane = lax.broadcasted_iota(I32, acc_scr.shape[1:], 1)
    for pr in range(acc_scr.shape[0] // 2):
        o0, o1 = [acc_scr[hd] / jnp.sum(l_scr[hd], axis=-1, keepdims=True)
                  for hd in (2 * pr, 2 * pr + 1)]
        o_ref[:, pr * LANES:(pr + 1) * LANES] = jnp.where(lane < head_dim, o0, o1).astype(o_ref.dtype)


def _mla_kernel(q_ref, k_ref, v_ref, o_ref, m_scr, l_scr, acc_scr):
    qi = pl.program_id(1)
    tq = q_ref.shape[0]
    _flash_init(m_scr, l_scr, acc_scr)

    def chunk(c, diagonal):
        ks = pl.ds(pl.multiple_of(c * tq, tq), tq)

        def scores(hd):
            sl = slice(hd * LANES, (hd + 1) * LANES)
            return _dot_nt(q_ref[:, sl], k_ref[ks, sl])

        pending = [scores(hd) for hd in range(QK_LOOKAHEAD)]
        for hd in range(MLA_HEADS):
            if hd + QK_LOOKAHEAD < MLA_HEADS:
                pending.append(scores(hd + QK_LOOKAHEAD))
            s = pending.pop(0)
            if diagonal:
                row = lax.broadcasted_iota(I32, (tq, tq), 0)
                col = lax.broadcasted_iota(I32, (tq, tq), 1)
                s = jnp.where(col <= row, s, NEG_BIG)
            pr = hd // 2
            _flash_update(hd, s, v_ref[ks, pr * LANES:(pr + 1) * LANES], m_scr, l_scr, acc_scr)

    def body(c, carry):
        chunk(c, False)
        return carry

    lax.fori_loop(0, qi, body, 0)
    chunk(qi, True)
    _flash_store(o_ref, l_scr, acc_scr, MLA_V)


def _mla_call(q, k, v):
    b, lp, _ = q.shape
    tq = SEQ_TILE
    nq = lp // tq
    state = pltpu.VMEM((MLA_HEADS, tq, LANES), F32)
    return pl.pallas_call(
        _mla_kernel,
        grid=(b, nq),
        in_specs=[pl.BlockSpec((None, tq, q.shape[2]), lambda bi, i: (bi, i, 0)),
                  pl.BlockSpec((None, lp, k.shape[2]), lambda bi, i: (bi, 0, 0)),
                  pl.BlockSpec((None, lp, v.shape[2]), lambda bi, i: (bi, 0, 0))],
        out_specs=pl.BlockSpec((None, tq, v.shape[2]), lambda bi, i: (bi, i, 0)),
        out_shape=jax.ShapeDtypeStruct((b, lp, v.shape[2]), BF16),
        scratch_shapes=[state, state, state],
        compiler_params=_cparams(("parallel", "arbitrary")),
        name="mla_attention",
    )(q, k, v)


def _dsa_kernel(iq_ref, iw_ref, ik_ref, q_ref, k_ref, v_ref, tz_ref, o_ref,
                key_scr, cap_scr, wt_scr, ans_scr, jst_scr, m_scr, l_scr, acc_scr,
                *, top_k):
    qi = pl.program_id(1)
    tq = q_ref.shape[0]
    nsub = tq // LANES
    sublanes = 8
    n_chunks = qi + 1
    key_i = lax.broadcasted_iota(I32, (tq, tq), 0)
    qry_i = lax.broadcasted_iota(I32, (tq, tq), 1)
    chunk_rows = lambda c: pl.ds(pl.multiple_of(c * tq, tq), tq)
    head_slot = lambda hd: slice(hd * LANES, (hd + 1) * LANES)
    wt_scr[...] = jnp.transpose(iw_ref[...])[0:IDX_HEADS, :]

    def index_chunk(c, diagonal):
        ikc = ik_ref[chunk_rows(c), :]
        sc = jnp.zeros((tq, tq), F32)
        for hd in range(IDX_HEADS):
            act = jnp.maximum(_dot_nt(ikc, iq_ref[:, head_slot(hd)]), 0.0)
            sc = sc + wt_scr[hd:hd + 1, :] * act
        sc = jnp.where(sc == 0.0, 0.0, sc)
        if diagonal:
            sc = jnp.where(key_i <= qry_i, sc, -jnp.inf)
        bits = pltpu.bitcast(sc, I32)
        key_scr[c] = bits ^ (lax.shift_right_arithmetic(bits, 31) & 0x7FFFFFFF)

    def index_body(c, carry):
        index_chunk(c, False)
        return carry

    lax.fori_loop(0, qi, index_body, 0)
    index_chunk(qi, True)

    kf = float(top_k)
    vec = (sublanes, tq)
    sub_i = lax.broadcasted_iota(I32, vec, 0)

    def count(pred):
        def body(c, accs):
            accs = list(accs)
            kk = key_scr[c]
            for g in range(tq // sublanes):
                hit = jnp.where(pred(kk[g * sublanes:(g + 1) * sublanes, :], c * tq + g * sublanes),
                                1.0, 0.0)
                accs[g % 2] = accs[g % 2] + hit
            return tuple(accs)
        zero = jnp.zeros(vec, F32)
        a0, a1 = lax.fori_loop(0, n_chunks, body, (zero, zero))
        return jnp.broadcast_to(jnp.sum(a0 + a1, axis=0, keepdims=True), vec)

    def bisect_step(bit, ans, done):
        cand = ans + lax.shift_left(jnp.int32(1), bit)
        cnt = count(lambda kk, _i: kk >= cand)
        open_ = done == 0
        ans = jnp.where(jnp.logical_and(open_, cnt >= kf), cand, ans)
        done = jnp.where(jnp.logical_and(open_, cnt == kf), 1, done)
        return ans, done

    def n_open_of(done):
        return jnp.sum(jnp.where(done == 0, 1.0, 0.0))

    def bis_cond(st):
        bit, _, _, n_open = st
        return jnp.logical_and(bit >= 0, n_open > 0.0)

    def bis_body(st):
        bit, ans, done, _ = st
        ans, done = bisect_step(bit, ans, done)
        ans, done = bisect_step(bit - 1, ans, done)
        return bit - 2, ans, done, n_open_of(done)

    t_pos = qi * tq + lax.broadcasted_iota(I32, vec, 1)
    done0 = (t_pos + 1 <= top_k).astype(I32)
    ans0 = jnp.full(vec, INT_MIN, I32)
    _, ans, done, n_open = lax.while_loop(bis_cond, bis_body,
                                          (jnp.int32(31), ans0, done0, n_open_of(done0)))
    ans_scr[...] = ans
    jst_scr[...] = jnp.full(vec, 2 ** 31 - 1, I32)

    @pl.when(n_open > 0.0)
    def _():
        need = kf - count(lambda kk, _i: kk > ans)

        def tie_body(i, jst):
            cand = jst + lax.shift_left(jnp.int32(1), 13 - i)
            cnt = count(lambda kk, i0: jnp.logical_and(kk == ans, sub_i + i0 < cand))
            return jnp.where(cnt < need, cand, jst)

        jst = lax.fori_loop(0, 14, tie_body, jnp.zeros(vec, I32))
        jst_scr[...] = jnp.where(done == 0, jst, 2 ** 31 - 1)

    ans_row = ans_scr[0:1, :]
    jst_row = jst_scr[0:1, :]

    def mask_chunk(c, diagonal):
        kk = key_scr[c]
        sel = jnp.logical_or(kk > ans_row,
                             jnp.logical_and(kk == ans_row, key_i + c * tq <= jst_row))
        if diagonal:
            sel = jnp.logical_and(sel, key_i <= qry_i)
        cap_scr[c] = jnp.transpose(jnp.where(sel, -NEG_BIG, NEG_BIG))

    def mask_body(c, carry):
        mask_chunk(c, False)
        return carry

    lax.fori_loop(0, qi, mask_body, 0)
    mask_chunk(qi, True)

    _flash_init(m_scr, l_scr, acc_scr)

    def bias_of(hd, where_):
        near = tz_ref[hd, :, LANES:2 * LANES]
        far = tz_ref[hd, :, 0:LANES]
        z = jnp.zeros((LANES, LANES), F32)
        if where_ == "previous":
            blocks = [[far if (a == 0 and b == nsub - 1) else z for b in range(nsub)]
                      for a in range(nsub)]
        else:
            blocks = [[near if b == a else far if b == a - 1 else z for b in range(nsub)]
                      for a in range(nsub)]
        return jnp.concatenate([jnp.concatenate(r, axis=1) for r in blocks], axis=0)

    def chunk(c, where_):
        ks = chunk_rows(c)
        cap = cap_scr[c]
        pair = lambda hd: slice((hd // 2) * LANES, (hd // 2 + 1) * LANES)
        scores = lambda hd: _dot_nt(q_ref[:, head_slot(hd)], k_ref[ks, pair(hd)])

        pending = [scores(hd) for hd in range(QK_LOOKAHEAD)]
        for hd in range(DSA_HEADS):
            if hd + QK_LOOKAHEAD < DSA_HEADS:
                pending.append(scores(hd + QK_LOOKAHEAD))
            s = jnp.minimum(pending.pop(0), cap)
            if where_ is not None:
                s = s + bias_of(hd, where_)
            _flash_update(hd, s, v_ref[ks, pair(hd)], m_scr, l_scr, acc_scr)

    def body(c, carry):
        chunk(c, None)
        return carry

    lax.fori_loop(0, jnp.maximum(qi - 1, 0), body, 0)

    @pl.when(qi >= 1)
    def _():
        chunk(qi - 1, "previous")

    chunk(qi, "diagonal")
    _flash_store(o_ref, l_scr, acc_scr, DSA_HEAD_DIM)


def _dsa_call(iq, iw, ik, q, k, v, tz, top_k):
    b, lp, _ = q.shape
    tq = SEQ_TILE
    nq = lp // tq
    qspec = lambda w: pl.BlockSpec((None, tq, w), lambda bi, i: (bi, i, 0))
    kspec = lambda w: pl.BlockSpec((None, lp, w), lambda bi, i: (bi, 0, 0))
    state = pltpu.VMEM((DSA_HEADS, tq, LANES), F32)
    return pl.pallas_call(
        functools.partial(_dsa_kernel, top_k=top_k),
        grid=(b, nq),
        in_specs=[qspec(iq.shape[2]), qspec(LANES), kspec(LANES), qspec(q.shape[2]),
                  kspec(DSA_WIDTH), kspec(DSA_WIDTH),
                  pl.BlockSpec(tz.shape, lambda bi, i: (0, 0, 0))],
        out_specs=qspec(DSA_WIDTH),
        out_shape=jax.ShapeDtypeStruct((b, lp, DSA_WIDTH), BF16),
        scratch_shapes=[pltpu.VMEM((nq, tq, tq), I32), pltpu.VMEM((nq, tq, tq), F32),
                        pltpu.VMEM((IDX_HEADS, tq), F32),
                        pltpu.VMEM((8, tq), I32), pltpu.VMEM((8, tq), I32),
                        state, state, state],
        compiler_params=_cparams(("parallel", "arbitrary")),
        name="dsa_attention",
    )(iq, iw, ik, q, k, v, tz)


def _mix_out_kernel(h_ref, a_ref, b_ref, wa_ref, wb_ref, o_ref):
    o_ref[...] = h_ref[...] + _dot(a_ref[...], wa_ref[...]) + _dot(b_ref[...], wb_ref[...])


def _mix_out_call(h, a, b, wa, wb):
    t = h.shape[0]
    tm = _row_tile(t)
    row = lambda w: pl.BlockSpec((tm, w), lambda i: (i, 0))
    full = lambda x: pl.BlockSpec(x.shape, lambda i: (0, 0))
    return pl.pallas_call(
        _mix_out_kernel,
        grid=(t // tm,),
        in_specs=[row(D_MODEL), row(a.shape[1]), row(b.shape[1]), full(wa), full(wb)],
        out_specs=row(D_MODEL),
        out_shape=jax.ShapeDtypeStruct(h.shape, F32),
        compiler_params=_cparams(("parallel",)),
        name="mix_out",
    )(h, a, b, wa, wb)


def _swiglu_chunk(xb, w1_ref, w3_ref, w2_ref):
    a = _dot(xb, w1_ref[...].astype(BF16))
    act = (a * jax.nn.sigmoid(a)) * _dot(xb, w3_ref[...].astype(BF16))
    return _dot(act.astype(BF16), w2_ref[...].astype(BF16))


def _ffn_kernel(h_ref, g_ref, w1_ref, w3_ref, w2_ref, o_ref, xn_scr, acc_scr):
    f = pl.program_id(1)

    @pl.when(f == 0)
    def _():
        xn_scr[...] = _rms(h_ref[...], g_ref[...]).astype(BF16)
        acc_scr[...] = jnp.zeros(acc_scr.shape, F32)

    acc_scr[...] += _swiglu_chunk(xn_scr[...], w1_ref, w3_ref, w2_ref)

    @pl.when(f == pl.num_programs(1) - 1)
    def _():
        o_ref[...] = h_ref[...] + acc_scr[...]


def _ffn_call(h, g, w1, w3, w2):
    t = h.shape[0]
    d_ff = w1.shape[1]
    tm = _row_tile(t)
    tf = 512 if d_ff % 512 == 0 else d_ff
    return pl.pallas_call(
        _ffn_kernel,
        grid=(t // tm, d_ff // tf),
        in_specs=[pl.BlockSpec((tm, D_MODEL), lambda i, f: (i, 0)),
                  pl.BlockSpec(g.shape, lambda i, f: (0, 0)),
                  pl.BlockSpec((D_MODEL, tf), lambda i, f: (0, f)),
                  pl.BlockSpec((D_MODEL, tf), lambda i, f: (0, f)),
                  pl.BlockSpec((tf, D_MODEL), lambda i, f: (f, 0))],
        out_specs=pl.BlockSpec((tm, D_MODEL), lambda i, f: (i, 0)),
        out_shape=jax.ShapeDtypeStruct(h.shape, F32),
        scratch_shapes=[pltpu.VMEM((tm, D_MODEL), BF16), pltpu.VMEM((tm, D_MODEL), F32)],
        compiler_params=_cparams(("parallel", "arbitrary")),
        name="dense_swiglu",
    )(h, g, w1, w3, w2)


def _conv_kernel(h_ref, g_ref, win_ref, cw_ref, wout_ref, o_ref, z_scr):
    i = pl.program_id(1)
    tm = h_ref.shape[0]
    halo = 8
    x = h_ref[...]
    xn = _rms(x, g_ref[...]).astype(BF16)
    c_gate = _dot(xn, win_ref[:, D_MODEL:2 * D_MODEL])
    u = _dot(xn, win_ref[:, 2 * D_MODEL:3 * D_MODEL])
    z = c_gate * u

    @pl.when(i == 0)
    def _():
        z_scr[0:halo, :] = jnp.zeros((halo, D_MODEL), F32)

    @pl.when(i > 0)
    def _():
        z_scr[0:halo, :] = z_scr[tm:tm + halo, :]

    z_scr[halo:halo + tm, :] = z
    cw = cw_ref[...]
    y = (cw[0:1, :] * z_scr[halo - 2:halo - 2 + tm, :]
         + cw[1:2, :] * z_scr[halo - 1:halo - 1 + tm, :]
         + cw[2:3, :] * z)
    b_gate = _dot(xn, win_ref[:, 0:D_MODEL])
    o_ref[...] = x + _dot((b_gate * y).astype(BF16), wout_ref[...])


def _conv_call(h3, g, w_in, cw, w_out):
    b, lp, d = h3.shape
    tm = SEQ_TILE
    full = lambda a: pl.BlockSpec(a.shape, lambda bi, i: (0,) * a.ndim)
    return pl.pallas_call(
        _conv_kernel,
        grid=(b, lp // tm),
        in_specs=[pl.BlockSpec((None, tm, d), lambda bi, i: (bi, i, 0)), full(g), full(w_in),
                  full(cw), full(w_out)],
        out_specs=pl.BlockSpec((None, tm, d), lambda bi, i: (bi, i, 0)),
        out_shape=jax.ShapeDtypeStruct(h3.shape, F32),
        scratch_shapes=[pltpu.VMEM((tm + 8, d), F32)],
        compiler_params=_cparams(("arbitrary", "arbitrary")),
        name="short_conv_mixer",
    )(h3, g, w_in, cw, w_out)


def _route_kernel(h_ref, g_ref, wr_ref, xg_ref, didx_ref, cnt_ref,
                  xs_scr, base_scr, dvm_scr, dsm_scr, sem_idx, sem_rows, *, stride):
    i = pl.program_id(0)
    tm = h_ref.shape[0]
    lane = lax.broadcasted_iota(I32, (tm, LANES), 1)
    lane_f = lane.astype(F32)

    @pl.when(i == 0)
    def _():
        base_scr[...] = jnp.zeros(base_scr.shape, F32)

    xnf = _rms(h_ref[...], g_ref[...])
    logits = jnp.where(lane < N_EXPERTS, _dot(xnf.astype(BF16), wr_ref[...]), -jnp.inf)
    v1 = jnp.max(logits, axis=-1, keepdims=True)
    i1 = jnp.min(jnp.where(logits == v1, lane_f, float(LANES)), axis=-1, keepdims=True)
    rest = jnp.where(lane_f == i1, -jnp.inf, logits)
    v2 = jnp.max(rest, axis=-1, keepdims=True)
    i2 = jnp.min(jnp.where(rest == v2, lane_f, float(LANES)), axis=-1, keepdims=True)
    e2 = jnp.exp(v2 - v1)
    den = 1.0 + e2
    gates = (1.0 / den, e2 / den)

    oh1 = jnp.where(lane_f == i1, 1.0, 0.0)
    oh2 = jnp.where(lane_f == i2, 1.0, 0.0)
    oh = oh1 + oh2
    earlier = (lax.broadcasted_iota(I32, (tm, tm), 1) < lax.broadcasted_iota(I32, (tm, tm), 0))
    prefix = _dot(jnp.where(earlier, 1.0, 0.0).astype(BF16), oh.astype(BF16))
    pos = base_scr[0:1, :] + prefix
    d1 = jnp.sum(oh1 * pos, axis=-1, keepdims=True) + i1 * float(stride)
    d2 = jnp.sum(oh2 * pos, axis=-1, keepdims=True) + i2 * float(stride)
    base_scr[...] = base_scr[...] + jnp.sum(oh, axis=0, keepdims=True)
    cnt_ref[...] = base_scr[...]

    slot = lax.rem(i, 2)
    for s in range(TOP_K):
        xs_scr[slot, s, :, 0:D_MODEL] = xnf
        xs_scr[slot, s, :, D_MODEL:] = jnp.broadcast_to(gates[s], (tm, LANES))

    dmat = jnp.where(lane == 0, d1, jnp.where(lane == 1, d2, 0.0))
    dvm_scr[...] = jnp.transpose(dmat)[0:8, :].astype(I32)
    didx_ref[...] = dvm_scr[...]
    cp = pltpu.make_async_copy(dvm_scr, dsm_scr, sem_idx)
    cp.start()
    cp.wait()

    for r in range(tm):
        for s in range(TOP_K):
            pltpu.make_async_copy(xs_scr.at[slot, s, pl.ds(r, 1)],
                                  xg_ref.at[pl.ds(dsm_scr[s, r], 1)],
                                  sem_rows.at[slot]).start(priority=s)

    def wait_rows(sl):
        for s in range(TOP_K):
            pltpu.make_async_copy(xs_scr.at[sl, s], xg_ref.at[pl.ds(0, tm)], sem_rows.at[sl]).wait()

    @pl.when(i > 0)
    def _():
        wait_rows(1 - slot)

    @pl.when(i == pl.num_programs(0) - 1)
    def _():
        wait_rows(slot)
        xs_scr[0, 0] = jnp.zeros(xs_scr.shape[2:], F32)
        dvm_scr[:, 0:LANES] = base_scr[...].astype(I32)
        cp2 = pltpu.make_async_copy(dvm_scr, dsm_scr, sem_idx)
        cp2.start()
        cp2.wait()
        sublanes = 8
        for phase in ("start", "wait"):
            for e in range(N_EXPERTS):
                n_e = dsm_scr[0, e]
                n_up = lax.shift_left(lax.shift_right_logical(n_e + sublanes - 1, 3), 3)
                for k in range(sublanes - 1):
                    @pl.when(n_e + k < n_up)
                    def _():
                        row = pltpu.make_async_copy(
                            xs_scr.at[0, 0, pl.ds(0, 1)],
                            xg_ref.at[pl.ds(e * stride + n_e + k, 1)], sem_rows.at[0])
                        row.start() if phase == "start" else row.wait()
                for k in range(EXPERT_TILE // tm):
                    blk = pltpu.make_async_copy(
                        xs_scr.at[0, 0],
                        xg_ref.at[pl.ds(pl.multiple_of(e * stride + n_up + k * tm, sublanes), tm)],
                        sem_rows.at[0])
                    blk.start() if phase == "start" else blk.wait()


def _route_call(h, g, wr, stride):
    t = h.shape[0]
    tm = MOE_TILE
    return pl.pallas_call(
        functools.partial(_route_kernel, stride=stride),
        grid=(t // tm,),
        in_specs=[pl.BlockSpec((tm, D_MODEL), lambda i: (i, 0)),
                  pl.BlockSpec(g.shape, lambda i: (0, 0)),
                  pl.BlockSpec(wr.shape, lambda i: (0, 0))],
        out_specs=[pl.BlockSpec(memory_space=pl.ANY),
                   pl.BlockSpec((8, tm), lambda i: (0, i)),
                   pl.BlockSpec((8, LANES), lambda i: (0, 0))],
        out_shape=[jax.ShapeDtypeStruct((N_EXPERTS * stride, XG_WIDTH), F32),
                   jax.ShapeDtypeStruct((8, t), I32),
                   jax.ShapeDtypeStruct((8, LANES), F32)],
        scratch_shapes=[pltpu.VMEM((2, TOP_K, tm, XG_WIDTH), F32), pltpu.VMEM((8, LANES), F32),
                        pltpu.VMEM((8, tm), I32), pltpu.SMEM((8, tm), I32),
                        pltpu.SemaphoreType.DMA, pltpu.SemaphoreType.DMA((2,))],
        compiler_params=_cparams(("arbitrary",)),
        name="moe_route",
    )(h, g, wr)


def _expert_kernel(blk_ref, exp_ref, nu_ref, x_ref, w1_ref, w3_ref, w2_ref, o_ref, xb_scr, acc_scr):
    j = pl.program_id(0)
    f = pl.program_id(1)

    @pl.when(j < nu_ref[0])
    def _():
        @pl.when(f == 0)
        def _():
            xb_scr[...] = x_ref[:, 0:D_MODEL].astype(BF16)
            acc_scr[...] = jnp.zeros(acc_scr.shape, F32)

        acc_scr[...] += _swiglu_chunk(xb_scr[...], w1_ref, w3_ref, w2_ref)

        @pl.when(f == pl.num_programs(1) - 1)
        def _():
            gate = x_ref[:, D_MODEL:]
            o_ref[...] = acc_scr[...] * jnp.concatenate([gate] * (D_MODEL // LANES), axis=1)


def _expert_call(tile_blk, tile_exp, n_used, xg, w1, w3, w2):
    d_ff = w1.shape[2]
    tm = EXPERT_TILE
    tf = 512 if d_ff % 512 == 0 else d_ff
    nf = d_ff // tf
    n_tiles = tile_blk.shape[0]
    fe = lambda j, f, nu: jnp.where(j < nu[0], f, nf - 1)
    grid_spec = pltpu.PrefetchScalarGridSpec(
        num_scalar_prefetch=3,
        grid=(n_tiles, nf),
        in_specs=[pl.BlockSpec((tm, XG_WIDTH), lambda j, f, blk, ex, nu: (blk[j], 0)),
                  pl.BlockSpec((None, D_MODEL, tf), lambda j, f, blk, ex, nu: (ex[j], 0, fe(j, f, nu))),
                  pl.BlockSpec((None, D_MODEL, tf), lambda j, f, blk, ex, nu: (ex[j], 0, fe(j, f, nu))),
                  pl.BlockSpec((None, tf, D_MODEL), lambda j, f, blk, ex, nu: (ex[j], fe(j, f, nu), 0))],
        out_specs=pl.BlockSpec((tm, D_MODEL), lambda j, f, blk, ex, nu: (blk[j], 0)),
        scratch_shapes=[pltpu.VMEM((tm, D_MODEL), BF16), pltpu.VMEM((tm, D_MODEL), F32)])
    return pl.pallas_call(
        _expert_kernel,
        grid_spec=grid_spec,
        out_shape=jax.ShapeDtypeStruct((xg.shape[0], D_MODEL), F32),
        compiler_params=_cparams(("arbitrary", "arbitrary")),
        name="expert_swiglu",
    )(tile_blk, tile_exp, n_used, xg, w1, w3, w2)


def _combine_kernel(h_ref, didx_ref, yg_ref, o_ref, y_scr, dsm_scr, sem_idx, sem_rows):
    i = pl.program_id(0)
    tm = h_ref.shape[0]
    slot = lax.rem(i, 2)

    def fetch(j, sl):
        cp = pltpu.make_async_copy(didx_ref.at[:, pl.ds(pl.multiple_of(j * tm, tm), tm)], dsm_scr,
                                   sem_idx)
        cp.start()
        cp.wait()
        for r in range(tm):
            for s in range(TOP_K):
                pltpu.make_async_copy(yg_ref.at[pl.ds(dsm_scr[s, r], 1)],
                                      y_scr.at[sl, s, pl.ds(r, 1)],
                                      sem_rows.at[sl]).start(priority=s)

    @pl.when(i == 0)
    def _():
        fetch(0, 0)

    @pl.when(i + 1 < pl.num_programs(0))
    def _():
        fetch(i + 1, 1 - slot)

    for s in range(TOP_K):
        pltpu.make_async_copy(yg_ref.at[pl.ds(0, tm)], y_scr.at[slot, s], sem_rows.at[slot]).wait()
    o_ref[...] = h_ref[...] + y_scr[slot, 0] + y_scr[slot, 1]


def _combine_call(h, didx, yg):
    t = h.shape[0]
    tm = MOE_TILE
    return pl.pallas_call(
        _combine_kernel,
        grid=(t // tm,),
        in_specs=[pl.BlockSpec((tm, D_MODEL), lambda i: (i, 0)),
                  pl.BlockSpec(memory_space=pl.ANY), pl.BlockSpec(memory_space=pl.ANY)],
        out_specs=pl.BlockSpec((tm, D_MODEL), lambda i: (i, 0)),
        out_shape=jax.ShapeDtypeStruct(h.shape, F32),
        scratch_shapes=[pltpu.VMEM((2, TOP_K, tm, D_MODEL), F32), pltpu.SMEM((8, tm), I32),
                        pltpu.SemaphoreType.DMA, pltpu.SemaphoreType.DMA((2,))],
        compiler_params=_cparams(("arbitrary",)),
        name="moe_combine",
    )(h, didx, yg)


def _moe_call(h, g, wr, w1, w3, w2):
    t = h.shape[0]
    tm = EXPERT_TILE
    assert t % MOE_TILE == 0 and tm % MOE_TILE == 0
    stride = -(-t // tm) * tm + tm
    xg, didx, cnt = _route_call(h, g, wr, stride)
    counts = cnt[0, :N_EXPERTS].astype(I32)
    tiles_e = (counts + tm - 1) // tm
    cum = jnp.cumsum(tiles_e)
    n_used = cum[-1]
    n_tiles = -(-TOP_K * t // tm) + N_EXPERTS
    jj = jnp.minimum(jnp.arange(n_tiles, dtype=I32), n_used - 1)
    tile_exp = jnp.sum((jj[:, None] >= cum[None, :]).astype(I32), axis=1)
    tile_blk = tile_exp * (stride // tm) + jj - (cum - tiles_e)[tile_exp]
    yg = _expert_call(tile_blk, tile_exp, n_used[None], xg, w1, w3, w2)
    return _combine_call(h, didx, yg)


def _mix_in_weights(w):
    offs = np.concatenate([[0], np.cumsum(MIX_IN_SIZES)])
    cq, ckv, kr, dq, dk, dv, iq, ik, iw = [w[:, offs[j]:offs[j + 1]] for j in range(9)]
    z = lambda n: jnp.zeros((w.shape[0], n), w.dtype)
    half = MLA_ROPE // 2
    kr_main = jnp.concatenate([z(MLA_NOPE), kr, z(LANES - MLA_QK)], axis=1)
    kr_swap = jnp.concatenate([z(MLA_NOPE), kr[:, half:], kr[:, :half], z(LANES - MLA_QK)], axis=1)
    ik4 = jnp.concatenate([ik] * (LANES // IDX_DIM), axis=1)
    iw_p = jnp.concatenate([iw, z(LANES - IDX_HEADS)], axis=1)
    return jnp.concatenate([cq, ckv, kr_main, kr_swap, dq, dk, dv, iq, ik4, iw_p],
                           axis=1).astype(BF16)


def _mla_q_weights(w_uq):
    r = w_uq.shape[0]
    w = w_uq.reshape(r, MLA_HEADS, MLA_QK)
    nope, rope = w[..., :MLA_NOPE], w[..., MLA_NOPE:]
    half = MLA_ROPE // 2
    z = lambda n: jnp.zeros((r, MLA_HEADS, n), w.dtype)
    main = jnp.concatenate([nope, rope, z(LANES - MLA_QK)], axis=-1)
    swap = jnp.concatenate([z(MLA_NOPE), rope[..., half:], rope[..., :half], z(LANES - MLA_QK)],
                           axis=-1)
    return (main.reshape(r, MLA_HEADS * LANES).astype(BF16),
            swap.reshape(r, MLA_HEADS * LANES).astype(BF16))


def _mla_kv_weights(w_ukv):
    r = w_ukv.shape[0]
    w = w_ukv.reshape(r, MLA_HEADS, MLA_NOPE + MLA_V)
    k_nope = jnp.concatenate([w[..., :MLA_NOPE], jnp.zeros((r, MLA_HEADS, LANES - MLA_NOPE), w.dtype)],
                             axis=-1)
    return (k_nope.reshape(r, MLA_HEADS * LANES).astype(BF16),
            w[..., MLA_NOPE:].reshape(r, MLA_HEADS * MLA_V).astype(BF16))


def _qk_gains(g):
    half = MLA_ROPE // 2
    z = lambda n: jnp.zeros((n,), g.dtype)
    main = jnp.concatenate([g, z(LANES - MLA_QK)])
    swap = jnp.concatenate([z(MLA_NOPE), g[MLA_NOPE + half:], g[MLA_NOPE:MLA_NOPE + half],
                            z(LANES - MLA_QK)])
    return main[None, :], swap[None, :]


def _rope_tables(lp):
    half = MLA_ROPE // 2
    inv = ROPE_BASE ** (-jnp.arange(half, dtype=F32) / half)
    ang = jnp.arange(lp, dtype=jnp.int32).astype(F32)[:, None] * inv[None, :]
    cos, sin = jnp.cos(ang), jnp.sin(ang)
    ones = jnp.ones((lp, MLA_NOPE), F32)
    pad1 = jnp.ones((lp, LANES - MLA_QK), F32)
    zeros = jnp.zeros((lp, MLA_NOPE), F32)
    pad0 = jnp.zeros((lp, LANES - MLA_QK), F32)
    return (jnp.concatenate([ones, cos, cos, pad1], axis=1),
            jnp.concatenate([zeros, -sin, sin, pad0], axis=1))


def _rel_buckets(n):
    max_exact = REL_BUCKETS // 2
    d = np.arange(n)
    df = np.maximum(d, 1).astype(np.float32)
    large = max_exact + (np.log(df / np.float32(max_exact))
                         / np.float32(math.log(REL_MAX_DIST / max_exact))
                         * np.float32(REL_BUCKETS - max_exact)).astype(np.int32)
    large = np.minimum(large, REL_BUCKETS - 1)
    return np.where(d < max_exact, d, large)


def _bias_tiles(rel_bias):
    buckets = _rel_buckets(2 * LANES)
    assert np.all(buckets[LANES - 1:] == REL_BUCKETS - 1)
    n_heads = rel_bias.shape[1]
    shifted = (rel_bias - rel_bias[REL_BUCKETS - 1:REL_BUCKETS, :]) * LOG2E
    period = 3 * LANES
    w = jnp.concatenate([jnp.take(shifted, jnp.asarray(buckets), axis=0),
                         jnp.zeros((period - 2 * LANES, n_heads), F32)], axis=0)
    u = jnp.roll(jnp.flip(w, axis=0), -(2 * LANES - 1), axis=0).T
    flat = jnp.tile(u, (1, LANES))[:, :LANES * (period - 1)]
    return flat.reshape(n_heads, LANES, period - 1)[:, :, :2 * LANES].astype(F32)


def kernel(x, meta_tokens, rel_bias, ev_norm_mix, ev_w_mix_in, ev_g_q_lat, ev_g_kv_lat, ev_w_uq,
           ev_w_ukv, ev_mla_q_norm, ev_mla_k_norm, ev_dsa_q_norm, ev_dsa_k_norm, ev_w_mix_out,
           ev_norm_ffn, ev_w1, ev_w3, ev_w2, od_norm_mix, od_w_in, od_conv_w, od_w_out,
           od_norm_ffn, od_w_router, od_w1, od_w3, od_w2):
    b, seq, d = x.shape
    assert d == D_MODEL
    l_tot = seq + N_META
    lp = -(-l_tot // BLOCK_Q) * BLOCK_Q
    assert lp % SEQ_TILE == 0, "sequence tiling assumes the padded length is a multiple of 384"
    top_k = min(DSA_TOPK_MAX, l_tot // 4)
    depth = ev_norm_mix.shape[0] + od_norm_mix.shape[0]

    meta = jnp.broadcast_to(meta_tokens[None].astype(x.dtype), (b, N_META, d))
    h = jnp.concatenate([meta, x, jnp.zeros((b, lp - l_tot, d), x.dtype)], axis=1)
    h = h.reshape(b * lp, d)

    cos_t, sin_t = _rope_tables(lp)
    tz = _bias_tiles(rel_bias)
    row2 = lambda v: v[None, :]

    for layer in range(depth):
        i = layer // 2
        if layer % 2 == 0:
            wqm, wqs = _mla_q_weights(ev_w_uq[i])
            wkk, wkv = _mla_kv_weights(ev_w_ukv[i])
            gqm, gqs = _qk_gains(ev_mla_q_norm[i])
            gkm, gks = _qk_gains(ev_mla_k_norm[i])
            gdq = row2(jnp.concatenate([ev_dsa_q_norm[i]] * 2))
            gdk = row2(jnp.concatenate([ev_dsa_k_norm[i]] * 2))
            qm, km, vm, qd, kd, vd, iq, ik, iw = _prep_call(
                h, row2(ev_norm_mix[i]), _mix_in_weights(ev_w_mix_in[i]), row2(ev_g_q_lat[i]),
                row2(ev_g_kv_lat[i]), wqm, wqs, wkk, wkv, gqm, gqs, gkm, gks, gdq, gdk,
                cos_t, sin_t, lp)
            seq3 = lambda a: a.reshape(b, lp, a.shape[1])
            o_mla = _mla_call(seq3(qm), seq3(km), seq3(vm))
            o_dsa = _dsa_call(seq3(iq), seq3(iw), seq3(ik), seq3(qd), seq3(kd), seq3(vd), tz, top_k)
            w_o = ev_w_mix_out[i].astype(BF16)
            n_mla = MLA_HEADS * MLA_V
            h = _mix_out_call(h, o_mla.reshape(b * lp, -1), o_dsa.reshape(b * lp, -1),
                              w_o[:n_mla], w_o[n_mla:])
            h = _ffn_call(h, row2(ev_norm_ffn[i]), ev_w1[i], ev_w3[i], ev_w2[i])
        else:
            h = _conv_call(h.reshape(b, lp, d), row2(od_norm_mix[i]), od_w_in[i].astype(BF16),
                           od_conv_w[i].reshape(CONV_WIDTH, d), od_w_out[i].astype(BF16))
            h = h.reshape(b * lp, d)
            wr = jnp.concatenate(
                [od_w_router[i], jnp.zeros((d, LANES - N_EXPERTS), od_w_router.dtype)], axis=1)
            h = _moe_call(h, row2(od_norm_ffn[i]), wr.astype(BF16), od_w1[i], od_w3[i], od_w2[i])
    return h.reshape(b, lp, d)[:, N_META:l_tot]
```

```python
import functools
import math

import numpy as np
import jax
import jax.numpy as jnp
from jax import lax
from jax.experimental import pallas as pl
from jax.experimental.pallas import tpu as pltpu

F32 = jnp.float32
BF16 = jnp.bfloat16
I32 = jnp.int32

D_MODEL = 1024
N_META = 16
BLOCK_Q = 128
EPS = 1e-6
MLA_HEADS = 8
MLA_Q_RANK = 384
MLA_KV_RANK = 256
MLA_NOPE = 64
MLA_ROPE = 32
MLA_V = 64
MLA_QK = MLA_NOPE + MLA_ROPE
MLA_SCALE = MLA_QK ** -0.5
ROPE_BASE = 10000.0
DSA_HEADS = 8
DSA_HEAD_DIM = 64
DSA_WIDTH = DSA_HEADS * DSA_HEAD_DIM
DSA_SCALE = DSA_HEAD_DIM ** -0.5
IDX_HEADS = 8
IDX_DIM = 32
DSA_TOPK_MAX = 256
REL_BUCKETS = 32
REL_MAX_DIST = 128
MIX_IN_SIZES = (MLA_Q_RANK, MLA_KV_RANK, MLA_ROPE, DSA_WIDTH, DSA_WIDTH, DSA_WIDTH,
                IDX_HEADS * IDX_DIM, IDX_DIM, IDX_HEADS)
CONV_WIDTH = 3
N_EXPERTS = 8
TOP_K = 2

LANES = 128
VMEM_LIMIT_BYTES = 56 * 1024 * 1024

SEQ_TILE = 3 * LANES
PREP_TILE = 2 * SEQ_TILE
MOE_TILE = 512
EXPERT_TILE = 2 * MOE_TILE
QK_LOOKAHEAD = 2
LOG2E = math.log2(math.e)
NEG_BIG = -1e30
INT_MIN = -2 ** 31

_C_CQ = 0
_C_CKV = _C_CQ + MLA_Q_RANK
_C_KRM = _C_CKV + MLA_KV_RANK
_C_KRS = _C_KRM + LANES
_C_DQ = _C_KRS + LANES
_C_DK = _C_DQ + DSA_WIDTH
_C_DV = _C_DK + DSA_WIDTH
_C_IQ = _C_DV + DSA_WIDTH
_C_IK = _C_IQ + IDX_HEADS * IDX_DIM
_C_IW = _C_IK + LANES
_C_END = _C_IW + LANES


def _cparams(sem):
    return pltpu.CompilerParams(dimension_semantics=sem, vmem_limit_bytes=VMEM_LIMIT_BYTES)


def _row_tile(n_rows, candidates=(1024, 768, 512, 384, 256, 128)):
    for c in candidates:
        if n_rows % c == 0:
            return c
    raise ValueError(f"no row tile for {n_rows}")


def _rms(x, g):
    ms = jnp.mean(x * x, axis=-1, keepdims=True)
    return x * lax.rsqrt(ms + EPS) * g


def _dot(a, b):
    return jnp.dot(a, b, preferred_element_type=F32)


def _dot_nt(a, b):
    return lax.dot_general(a, b, (((1,), (1,)), ((), ())), preferred_element_type=F32)


def _prep_kernel(h_ref, g_ref, wext_ref, gql_ref, gkvl_ref, wqm_ref, wqs_ref, wkk_ref, wkv_ref,
                 gqm_ref, gqs_ref, gkm_ref, gks_ref, gdq_ref, gdk_ref, cos_ref, sin_ref,
                 qm_o, km_o, vm_o, qd_o, kd_o, vd_o, iq_o, ik_o, iw_o):
    xn = _rms(h_ref[...], g_ref[...]).astype(BF16)

    def proj(lo, hi):
        return _dot(xn, wext_ref[:, lo:hi])

    cos = cos_ref[...]
    sin = sin_ref[...]
    lane = lax.broadcasted_iota(I32, (xn.shape[0], LANES), 1)

    cqn = _rms(proj(_C_CQ, _C_CKV), gql_ref[...]).astype(BF16)
    q_main = _dot(cqn, wqm_ref[...])
    q_swap = _dot(cqn, wqs_ref[...])
    for hd in range(MLA_HEADS):
        sl = slice(hd * LANES, (hd + 1) * LANES)
        a = q_main[:, sl]
        r = lax.rsqrt(jnp.sum(a * a, axis=-1, keepdims=True) * (1.0 / MLA_QK) + EPS)
        out = (a * r * gqm_ref[...]) * cos + (q_swap[:, sl] * r * gqs_ref[...]) * sin
        qm_o[:, sl] = (out * (MLA_SCALE * LOG2E)).astype(BF16)

    ckvn = _rms(proj(_C_CKV, _C_KRM), gkvl_ref[...]).astype(BF16)
    k_nope = _dot(ckvn, wkk_ref[...])
    vm_o[...] = _dot(ckvn, wkv_ref[...]).astype(BF16)
    kr_main = proj(_C_KRM, _C_KRS)
    kr_swap = proj(_C_KRS, _C_DQ)
    for hd in range(MLA_HEADS):
        sl = slice(hd * LANES, (hd + 1) * LANES)
        a = k_nope[:, sl] + kr_main
        r = lax.rsqrt(jnp.sum(a * a, axis=-1, keepdims=True) * (1.0 / MLA_QK) + EPS)
        out = (a * r * gkm_ref[...]) * cos + (kr_swap * r * gks_ref[...]) * sin
        km_o[:, sl] = out.astype(BF16)

    first = lane < DSA_HEAD_DIM
    for (lo, g2_ref, o_ref, post) in ((_C_DQ, gdq_ref, qd_o, DSA_SCALE * LOG2E),
                                      (_C_DK, gdk_ref, kd_o, None)):
        for pr in range(DSA_HEADS // 2):
            x = proj(lo + pr * LANES, lo + (pr + 1) * LANES)
            sq = x * x
            s0 = jnp.sum(jnp.where(first, sq, 0.0), axis=-1, keepdims=True)
            s1 = jnp.sum(jnp.where(first, 0.0, sq), axis=-1, keepdims=True)
            r0 = lax.rsqrt(s0 * (1.0 / DSA_HEAD_DIM) + EPS)
            r1 = lax.rsqrt(s1 * (1.0 / DSA_HEAD_DIM) + EPS)
            out = x * jnp.where(first, r0, r1) * g2_ref[...]
            if post is None:
                o_ref[:, pr * LANES:(pr + 1) * LANES] = out.astype(BF16)
            else:
                out = out * post
                o_ref[:, (2 * pr) * LANES:(2 * pr + 1) * LANES] = jnp.where(first, out, 0.0).astype(BF16)
                o_ref[:, (2 * pr + 1) * LANES:(2 * pr + 2) * LANES] = jnp.where(first, 0.0, out).astype(BF16)

    vd_o[...] = proj(_C_DV, _C_IQ).astype(BF16)
    lane_group = lax.shift_right_logical(lane, int(math.log2(IDX_DIM)))
    for quad in range(IDX_HEADS * IDX_DIM // LANES):
        x = proj(_C_IQ + quad * LANES, _C_IQ + (quad + 1) * LANES)
        for j in range(LANES // IDX_DIM):
            hd = quad * (LANES // IDX_DIM) + j
            iq_o[:, hd * LANES:(hd + 1) * LANES] = jnp.where(lane_group == j, x, 0.0).astype(BF16)
    ik_o[...] = proj(_C_IK, _C_IW).astype(BF16)
    iw_o[...] = proj(_C_IW, _C_END)


def _prep_call(h, g_mix, wext, gql, gkvl, wqm, wqs, wkk, wkv, gqm, gqs, gkm, gks, gdq, gdk,
               cos_t, sin_t, lp):
    t = h.shape[0]
    tm = PREP_TILE
    assert t % tm == 0
    reps = tm // math.gcd(lp, tm)
    nt = reps * lp // tm
    cos_t, sin_t = jnp.tile(cos_t, (reps, 1)), jnp.tile(sin_t, (reps, 1))
    row = lambda w: pl.BlockSpec((tm, w), lambda i: (i, 0))
    full = lambda a: pl.BlockSpec(a.shape, lambda i: (0, 0), pipeline_mode=pl.Buffered(1))
    tab = pl.BlockSpec((tm, LANES), lambda i: (i % nt, 0))
    hw = MLA_HEADS * LANES
    out_shape = [
        jax.ShapeDtypeStruct((t, hw), BF16), jax.ShapeDtypeStruct((t, hw), BF16),
        jax.ShapeDtypeStruct((t, MLA_HEADS * MLA_V), BF16),
        jax.ShapeDtypeStruct((t, DSA_HEADS * LANES), BF16), jax.ShapeDtypeStruct((t, DSA_WIDTH), BF16),
        jax.ShapeDtypeStruct((t, DSA_WIDTH), BF16),
        jax.ShapeDtypeStruct((t, IDX_HEADS * LANES), BF16),
        jax.ShapeDtypeStruct((t, LANES), BF16), jax.ShapeDtypeStruct((t, LANES), F32),
    ]
    return pl.pallas_call(
        _prep_kernel,
        grid=(t // tm,),
        in_specs=[row(D_MODEL), full(g_mix), full(wext), full(gql), full(gkvl), full(wqm),
                  full(wqs), full(wkk), full(wkv), full(gqm), full(gqs), full(gkm), full(gks),
                  full(gdq), full(gdk), tab, tab],
        out_specs=[row(s.shape[1]) for s in out_shape],
        out_shape=out_shape,
        compiler_params=_cparams(("parallel",)),
        name="prep_mix_in",
    )(h, g_mix, wext, gql, gkvl, wqm, wqs, wkk, wkv, gqm, gqs, gkm, gks, gdq, gdk, cos_t, sin_t)


def _flash_init(m_scr, l_scr, acc_scr):
    m_scr[...] = jnp.full(m_scr.shape, -jnp.inf, F32)
    l_scr[...] = jnp.zeros(l_scr.shape, F32)
    acc_scr[...] = jnp.zeros(acc_scr.shape, F32)


def _lane_fold(x, op):
    return functools.reduce(op, [x[:, j * LANES:(j + 1) * LANES] for j in range(x.shape[1] // LANES)])


def _flash_update(hd, s, vc, m_scr, l_scr, acc_scr):
    m_old = m_scr[hd]
    m_new = jnp.maximum(m_old, jnp.max(_lane_fold(s, jnp.maximum), axis=-1, keepdims=True))
    alpha = jnp.exp2(m_old - m_new)
    p = jnp.exp2(s - jnp.concatenate([m_new] * (s.shape[1] // LANES), axis=1))
    m_scr[hd] = m_new
    l_scr[hd] = alpha * l_scr[hd] + _lane_fold(p, jnp.add)
    acc_scr[hd] = alpha * acc_scr[hd] + _dot(p.astype(BF16), vc)


def _flash_store(o_ref, l_scr, acc_scr, head_dim):
    lane = lax.broadcasted_iota(I32, acc_scr.shape[1:], 1)
    for pr in range(acc_scr.shape[0] // 2):
        o0, o1 = [acc_scr[hd] / jnp.sum(l_scr[hd], axis=-1, keepdims=True)
                  for hd in (2 * pr, 2 * pr + 1)]
        o_ref[:, pr * LANES:(pr + 1) * LANES] = jnp.where(lane < head_dim, o0, o1).astype(o_ref.dtype)


def _mla_kernel(q_ref, k_ref, v_ref, o_ref, m_scr, l_scr, acc_scr):
    qi = pl.program_id(1)
    tq = q_ref.shape[0]
    _flash_init(m_scr, l_scr, acc_scr)

    def chunk(c, diagonal):
        ks = pl.ds(pl.multiple_of(c * tq, tq), tq)

        def scores(hd):
            sl = slice(hd * LANES, (hd + 1) * LANES)
            return _dot_nt(q_ref[:, sl], k_ref[ks, sl])

        pending = [scores(hd) for hd in range(QK_LOOKAHEAD)]
        for hd in range(MLA_HEADS):
            if hd + QK_LOOKAHEAD < MLA_HEADS:
                pending.append(scores(hd + QK_LOOKAHEAD))
            s = pending.pop(0)
            if diagonal:
                row = lax.broadcasted_iota(I32, (tq, tq), 0)
                col = lax.broadcasted_iota(I32, (tq, tq), 1)
                s = jnp.where(col <= row, s, NEG_BIG)
            pr = hd // 2
            _flash_update(hd, s, v_ref[ks, pr * LANES:(pr + 1) * LANES], m_scr, l_scr, acc_scr)

    def body(c, carry):
        chunk(c, False)
        return carry

    lax.fori_loop(0, qi, body, 0)
    chunk(qi, True)
    _flash_store(o_ref, l_scr, acc_scr, MLA_V)


def _mla_call(q, k, v):
    b, lp, _ = q.shape
    tq = SEQ_TILE
    nq = lp // tq
    state = pltpu.VMEM((MLA_HEADS, tq, LANES), F32)
    return pl.pallas_call(
        _mla_kernel,
        grid=(b, nq),
        in_specs=[pl.BlockSpec((None, tq, q.shape[2]), lambda bi, i: (bi, i, 0)),
                  pl.BlockSpec((None, lp, k.shape[2]), lambda bi, i: (bi, 0, 0)),
                  pl.BlockSpec((None, lp, v.shape[2]), lambda bi, i: (bi, 0, 0))],
        out_specs=pl.BlockSpec((None, tq, v.shape[2]), lambda bi, i: (bi, i, 0)),
        out_shape=jax.ShapeDtypeStruct((b, lp, v.shape[2]), BF16),
        scratch_shapes=[state, state, state],
        compiler_params=_cparams(("parallel", "arbitrary")),
        name="mla_attention",
    )(q, k, v)


def _dsa_kernel(iq_ref, iw_ref, ik_ref, q_ref, k_ref, v_ref, tz_ref, o_ref,
                key_scr, cap_scr, wt_scr, ans_scr, jst_scr, m_scr, l_scr, acc_scr,
                *, top_k):
    qi = pl.program_id(1)
    tq = q_ref.shape[0]
    nsub = tq // LANES
    sublanes = 8
    n_chunks = qi + 1
    key_i = lax.broadcasted_iota(I32, (tq, tq), 0)
    qry_i = lax.broadcasted_iota(I32, (tq, tq), 1)
    chunk_rows = lambda c: pl.ds(pl.multiple_of(c * tq, tq), tq)
    head_slot = lambda hd: slice(hd * LANES, (hd + 1) * LANES)
    wt_scr[...] = jnp.transpose(iw_ref[...])[0:IDX_HEADS, :]

    def index_chunk(c, diagonal):
        ikc = ik_ref[chunk_rows(c), :]
        sc = jnp.zeros((tq, tq), F32)
        for hd in range(IDX_HEADS):
            act = jnp.maximum(_dot_nt(ikc, iq_ref[:, head_slot(hd)]), 0.0)
            sc = sc + wt_scr[hd:hd + 1, :] * act
        sc = jnp.where(sc == 0.0, 0.0, sc)
        if diagonal:
            sc = jnp.where(key_i <= qry_i, sc, -jnp.inf)
        bits = pltpu.bitcast(sc, I32)
        key_scr[c] = bits ^ (lax.shift_right_arithmetic(bits, 31) & 0x7FFFFFFF)

    def index_body(c, carry):
        index_chunk(c, False)
        return carry

    lax.fori_loop(0, qi, index_body, 0)
    index_chunk(qi, True)

    kf = float(top_k)
    vec = (sublanes, tq)
    sub_i = lax.broadcasted_iota(I32, vec, 0)

    def count(pred):
        def body(c, accs):
            accs = list(accs)
            kk = key_scr[c]
            for g in range(tq // sublanes):
                hit = jnp.where(pred(kk[g * sublanes:(g + 1) * sublanes, :], c * tq + g * sublanes),
                                1.0, 0.0)
                accs[g % 2] = accs[g % 2] + hit
            return tuple(accs)
        zero = jnp.zeros(vec, F32)
        a0, a1 = lax.fori_loop(0, n_chunks, body, (zero, zero))
        return jnp.broadcast_to(jnp.sum(a0 + a1, axis=0, keepdims=True), vec)

    def bisect_step(bit, ans, done):
        cand = ans + lax.shift_left(jnp.int32(1), bit)
        cnt = count(lambda kk, _i: kk >= cand)
        open_ = done == 0
        ans = jnp.where(jnp.logical_and(open_, cnt >= kf), cand, ans)
        done = jnp.where(jnp.logical_and(open_, cnt == kf), 1, done)
        return ans, done

    def n_open_of(done):
        return jnp.sum(jnp.where(done == 0, 1.0, 0.0))

    def bis_cond(st):
        bit, _, _, n_open = st
        return jnp.logical_and(bit >= 0, n_open > 0.0)

    def bis_body(st):
        bit, ans, done, _ = st
        ans, done = bisect_step(bit, ans, done)
        ans, done = bisect_step(bit - 1, ans, done)
        return bit - 2, ans, done, n_open_of(done)

    t_pos = qi * tq + lax.broadcasted_iota(I32, vec, 1)
    done0 = (t_pos + 1 <= top_k).astype(I32)
    ans0 = jnp.full(vec, INT_MIN, I32)
    _, ans, done, n_open = lax.while_loop(bis_cond, bis_body,
                                          (jnp.int32(31), ans0, done0, n_open_of(done0)))
    ans_scr[...] = ans
    jst_scr[...] = jnp.full(vec, 2 ** 31 - 1, I32)

    @pl.when(n_open > 0.0)
    def _():
        need = kf - count(lambda kk, _i: kk > ans)

        def tie_body(i, jst):
            cand = jst + lax.shift_left(jnp.int32(1), 13 - i)
            cnt = count(lambda kk, i0: jnp.logical_and(kk == ans, sub_i + i0 < cand))
            return jnp.where(cnt < need, cand, jst)

        jst = lax.fori_loop(0, 14, tie_body, jnp.zeros(vec, I32))
        jst_scr[...] = jnp.where(done == 0, jst, 2 ** 31 - 1)

    ans_row = ans_scr[0:1, :]
    jst_row = jst_scr[0:1, :]

    def mask_chunk(c, diagonal):
        kk = key_scr[c]
        sel = jnp.logical_or(kk > ans_row,
                             jnp.logical_and(kk == ans_row, key_i + c * tq <= jst_row))
        if diagonal:
            sel = jnp.logical_and(sel, key_i <= qry_i)
        cap_scr[c] = jnp.transpose(jnp.where(sel, -NEG_BIG, NEG_BIG))

    def mask_body(c, carry):
        mask_chunk(c, False)
        return carry

    lax.fori_loop(0, qi, mask_body, 0)
    mask_chunk(qi, True)

    _flash_init(m_scr, l_scr, acc_scr)

    def bias_of(hd, where_):
        near = tz_ref[hd, :, LANES:2 * LANES]
        far = tz_ref[hd, :, 0:LANES]
        z = jnp.zeros((LANES, LANES), F32)
        if where_ == "previous":
            blocks = [[far if (a == 0 and b == nsub - 1) else z for b in range(nsub)]
                      for a in range(nsub)]
        else:
            blocks = [[near if b == a else far if b == a - 1 else z for b in range(nsub)]
                      for a in range(nsub)]
        return jnp.concatenate([jnp.concatenate(r, axis=1) for r in blocks], axis=0)

    def chunk(c, where_):
        ks = chunk_rows(c)
        cap = cap_scr[c]
        pair = lambda hd: slice((hd // 2) * LANES, (hd // 2 + 1) * LANES)
        scores = lambda hd: _dot_nt(q_ref[:, head_slot(hd)], k_ref[ks, pair(hd)])

        pending = [scores(hd) for hd in range(QK_LOOKAHEAD)]
        for hd in range(DSA_HEADS):
            if hd + QK_LOOKAHEAD < DSA_HEADS:
                pending.append(scores(hd + QK_LOOKAHEAD))
            s = jnp.minimum(pending.pop(0), cap)
            if where_ is not None:
                s = s + bias_of(hd, where_)
            _flash_update(hd, s, v_ref[ks, pair(hd)], m_scr, l_scr, acc_scr)

    def body(c, carry):
        chunk(c, None)
        return carry

    lax.fori_loop(0, jnp.maximum(qi - 1, 0), body, 0)

    @pl.when(qi >= 1)
    def _():
        chunk(qi - 1, "previous")

    chunk(qi, "diagonal")
    _flash_store(o_ref, l_scr, acc_scr, DSA_HEAD_DIM)


def _dsa_call(iq, iw, ik, q, k, v, tz, top_k):
    b, lp, _ = q.shape
    tq = SEQ_TILE
    nq = lp // tq
    qspec = lambda w: pl.BlockSpec((None, tq, w), lambda bi, i: (bi, i, 0))
    kspec = lambda w: pl.BlockSpec((None, lp, w), lambda bi, i: (bi, 0, 0))
    state = pltpu.VMEM((DSA_HEADS, tq, LANES), F32)
    return pl.pallas_call(
        functools.partial(_dsa_kernel, top_k=top_k),
        grid=(b, nq),
        in_specs=[qspec(iq.shape[2]), qspec(LANES), kspec(LANES), qspec(q.shape[2]),
                  kspec(DSA_WIDTH), kspec(DSA_WIDTH),
                  pl.BlockSpec(tz.shape, lambda bi, i: (0, 0, 0))],
        out_specs=qspec(DSA_WIDTH),
        out_shape=jax.ShapeDtypeStruct((b, lp, DSA_WIDTH), BF16),
        scratch_shapes=[pltpu.VMEM((nq, tq, tq), I32), pltpu.VMEM((nq, tq, tq), F32),
                        pltpu.VMEM((IDX_HEADS, tq), F32),
                        pltpu.VMEM((8, tq), I32), pltpu.VMEM((8, tq), I32),
                        state, state, state],
        compiler_params=_cparams(("parallel", "arbitrary")),
        name="dsa_attention",
    )(iq, iw, ik, q, k, v, tz)


def _mix_out_kernel(h_ref, a_ref, b_ref, wa_ref, wb_ref, o_ref):
    o_ref[...] = h_ref[...] + _dot(a_ref[...], wa_ref[...]) + _dot(b_ref[...], wb_ref[...])


def _mix_out_call(h, a, b, wa, wb):
    t = h.shape[0]
    tm = _row_tile(t)
    row = lambda w: pl.BlockSpec((tm, w), lambda i: (i, 0))
    full = lambda x: pl.BlockSpec(x.shape, lambda i: (0, 0))
    return pl.pallas_call(
        _mix_out_kernel,
        grid=(t // tm,),
        in_specs=[row(D_MODEL), row(a.shape[1]), row(b.shape[1]), full(wa), full(wb)],
        out_specs=row(D_MODEL),
        out_shape=jax.ShapeDtypeStruct(h.shape, F32),
        compiler_params=_cparams(("parallel",)),
        name="mix_out",
    )(h, a, b, wa, wb)


def _swiglu_chunk(xb, w1_ref, w3_ref, w2_ref):
    a = _dot(xb, w1_ref[...].astype(BF16))
    act = (a * jax.nn.sigmoid(a)) * _dot(xb, w3_ref[...].astype(BF16))
    return _dot(act.astype(BF16), w2_ref[...].astype(BF16))


def _ffn_kernel(h_ref, g_ref, w1_ref, w3_ref, w2_ref, o_ref, xn_scr, acc_scr):
    f = pl.program_id(1)

    @pl.when(f == 0)
    def _():
        xn_scr[...] = _rms(h_ref[...], g_ref[...]).astype(BF16)
        acc_scr[...] = jnp.zeros(acc_scr.shape, F32)

    acc_scr[...] += _swiglu_chunk(xn_scr[...], w1_ref, w3_ref, w2_ref)

    @pl.when(f == pl.num_programs(1) - 1)
    def _():
        o_ref[...] = h_ref[...] + acc_scr[...]


def _ffn_call(h, g, w1, w3, w2):
    t = h.shape[0]
    d_ff = w1.shape[1]
    tm = _row_tile(t)
    tf = 512 if d_ff % 512 == 0 else d_ff
    return pl.pallas_call(
        _ffn_kernel,
        grid=(t // tm, d_ff // tf),
        in_specs=[pl.BlockSpec((tm, D_MODEL), lambda i, f: (i, 0)),
                  pl.BlockSpec(g.shape, lambda i, f: (0, 0)),
                  pl.BlockSpec((D_MODEL, tf), lambda i, f: (0, f)),
                  pl.BlockSpec((D_MODEL, tf), lambda i, f: (0, f)),
                  pl.BlockSpec((tf, D_MODEL), lambda i, f: (f, 0))],
        out_specs=pl.BlockSpec((tm, D_MODEL), lambda i, f: (i, 0)),
        out_shape=jax.ShapeDtypeStruct(h.shape, F32),
        scratch_shapes=[pltpu.VMEM((tm, D_MODEL), BF16), pltpu.VMEM((tm, D_MODEL), F32)],
        compiler_params=_cparams(("parallel", "arbitrary")),
        name="dense_swiglu",
    )(h, g, w1, w3, w2)


def _conv_kernel(h_ref, g_ref, win_ref, cw_ref, wout_ref, o_ref, z_scr):
    i = pl.program_id(1)
    tm = h_ref.shape[0]
    halo = 8
    x = h_ref[...]
    xn = _rms(x, g_ref[...]).astype(BF16)
    c_gate = _dot(xn, win_ref[:, D_MODEL:2 * D_MODEL])
    u = _dot(xn, win_ref[:, 2 * D_MODEL:3 * D_MODEL])
    z = c_gate * u

    @pl.when(i == 0)
    def _():
        z_scr[0:halo, :] = jnp.zeros((halo, D_MODEL), F32)

    @pl.when(i > 0)
    def _():
        z_scr[0:halo, :] = z_scr[tm:tm + halo, :]

    z_scr[halo:halo + tm, :] = z
    cw = cw_ref[...]
    y = (cw[0:1, :] * z_scr[halo - 2:halo - 2 + tm, :]
         + cw[1:2, :] * z_scr[halo - 1:halo - 1 + tm, :]
         + cw[2:3, :] * z)
    b_gate = _dot(xn, win_ref[:, 0:D_MODEL])
    o_ref[...] = x + _dot((b_gate * y).astype(BF16), wout_ref[...])


def _conv_call(h3, g, w_in, cw, w_out):
    b, lp, d = h3.shape
    tm = SEQ_TILE
    full = lambda a: pl.BlockSpec(a.shape, lambda bi, i: (0,) * a.ndim)
    return pl.pallas_call(
        _conv_kernel,
        grid=(b, lp // tm),
        in_specs=[pl.BlockSpec((None, tm, d), lambda bi, i: (bi, i, 0)), full(g), full(w_in),
                  full(cw), full(w_out)],
        out_specs=pl.BlockSpec((None, tm, d), lambda bi, i: (bi, i, 0)),
        out_shape=jax.ShapeDtypeStruct(h3.shape, F32),
        scratch_shapes=[pltpu.VMEM((tm + 8, d), F32)],
        compiler_params=_cparams(("arbitrary", "arbitrary")),
        name="short_conv_mixer",
    )(h3, g, w_in, cw, w_out)


def _as_tiles(x):
    return x.reshape(x.shape[0], D_MODEL // LANES, LANES)


def _as_rows(x3):
    return x3.reshape(x3.shape[0], D_MODEL)


def _route_kernel(h_ref, g_ref, wr_ref, xg_ref, didx_ref, gate_ref, cnt_ref,
                  xs_scr, base_scr, dvm_scr, dsm_scr, sem_idx, sem_rows, *, stride):
    i = pl.program_id(0)
    tm = h_ref.shape[0]
    lane = lax.broadcasted_iota(I32, (tm, LANES), 1)
    lane_f = lane.astype(F32)

    @pl.when(i == 0)
    def _():
        base_scr[...] = jnp.zeros(base_scr.shape, F32)

    xnf = _rms(h_ref[...], g_ref[...])
    logits = jnp.where(lane < N_EXPERTS, _dot(xnf.astype(BF16), wr_ref[...]), -jnp.inf)
    v1 = jnp.max(logits, axis=-1, keepdims=True)
    i1 = jnp.min(jnp.where(logits == v1, lane_f, float(LANES)), axis=-1, keepdims=True)
    rest = jnp.where(lane_f == i1, -jnp.inf, logits)
    v2 = jnp.max(rest, axis=-1, keepdims=True)
    i2 = jnp.min(jnp.where(rest == v2, lane_f, float(LANES)), axis=-1, keepdims=True)
    e2 = jnp.exp(v2 - v1)
    den = 1.0 + e2
    gates = (1.0 / den, e2 / den)

    oh1 = jnp.where(lane_f == i1, 1.0, 0.0)
    oh2 = jnp.where(lane_f == i2, 1.0, 0.0)
    oh = oh1 + oh2
    earlier = (lax.broadcasted_iota(I32, (tm, tm), 1) < lax.broadcasted_iota(I32, (tm, tm), 0))
    prefix = _dot(jnp.where(earlier, 1.0, 0.0).astype(BF16), oh.astype(BF16))
    pos = base_scr[0:1, :] + prefix
    d1 = jnp.sum(oh1 * pos, axis=-1, keepdims=True) + i1 * float(stride)
    d2 = jnp.sum(oh2 * pos, axis=-1, keepdims=True) + i2 * float(stride)
    base_scr[...] = base_scr[...] + jnp.sum(oh, axis=0, keepdims=True)
    cnt_ref[...] = base_scr[...]

    gate_ref[...] = jnp.where(lane == 0, gates[0], jnp.where(lane == 1, gates[1], 0.0))
    slot = lax.rem(i, 2)
    xs_scr[slot] = _as_tiles(xnf)

    dmat = jnp.where(lane == 0, d1, jnp.where(lane == 1, d2, 0.0))
    dvm_scr[...] = jnp.transpose(dmat)[0:8, :].astype(I32)
    didx_ref[...] = dvm_scr[...]
    cp = pltpu.make_async_copy(dvm_scr, dsm_scr, sem_idx)
    cp.start()
    cp.wait()

    for r in range(tm):
        for s in range(TOP_K):
            pltpu.make_async_copy(xs_scr.at[slot, r], xg_ref.at[dsm_scr[s, r]],
                                  sem_rows.at[slot]).start(priority=s)

    def wait_rows(sl):
        for s in range(TOP_K):
            pltpu.make_async_copy(xs_scr.at[sl], xg_ref.at[pl.ds(0, tm)], sem_rows.at[sl]).wait()

    @pl.when(i > 0)
    def _():
        wait_rows(1 - slot)

    @pl.when(i == pl.num_programs(0) - 1)
    def _():
        wait_rows(slot)
        xs_scr[0] = jnp.zeros(xs_scr.shape[1:], F32)
        dvm_scr[:, 0:LANES] = base_scr[...].astype(I32)
        cp2 = pltpu.make_async_copy(dvm_scr, dsm_scr, sem_idx)
        cp2.start()
        cp2.wait()
        for phase in ("start", "wait"):
            for e in range(N_EXPERTS):
                for k in range(EXPERT_TILE // tm):
                    blk = pltpu.make_async_copy(
                        xs_scr.at[0], xg_ref.at[pl.ds(e * stride + dsm_scr[0, e] + k * tm, tm)],
                        sem_rows.at[0])
                    blk.start() if phase == "start" else blk.wait()


def _route_call(h, g, wr, stride):
    t = h.shape[0]
    tm = MOE_TILE
    return pl.pallas_call(
        functools.partial(_route_kernel, stride=stride),
        grid=(t // tm,),
        in_specs=[pl.BlockSpec((tm, D_MODEL), lambda i: (i, 0)),
                  pl.BlockSpec(g.shape, lambda i: (0, 0)),
                  pl.BlockSpec(wr.shape, lambda i: (0, 0))],
        out_specs=[pl.BlockSpec(memory_space=pl.ANY),
                   pl.BlockSpec((8, tm), lambda i: (0, i)),
                   pl.BlockSpec((tm, LANES), lambda i: (i, 0)),
                   pl.BlockSpec((8, LANES), lambda i: (0, 0))],
        out_shape=[jax.ShapeDtypeStruct((N_EXPERTS * stride, D_MODEL // LANES, LANES), F32),
                   jax.ShapeDtypeStruct((8, t), I32),
                   jax.ShapeDtypeStruct((t, LANES), F32),
                   jax.ShapeDtypeStruct((8, LANES), F32)],
        scratch_shapes=[pltpu.VMEM((2, tm, D_MODEL // LANES, LANES), F32), pltpu.VMEM((8, LANES), F32),
                        pltpu.VMEM((8, tm), I32), pltpu.SMEM((8, tm), I32),
                        pltpu.SemaphoreType.DMA, pltpu.SemaphoreType.DMA((2,))],
        compiler_params=_cparams(("arbitrary",)),
        name="moe_route",
    )(h, g, wr)


def _expert_kernel(blk_ref, exp_ref, nu_ref, x_ref, w1_ref, w3_ref, w2_ref, o_ref, xb_scr, acc_scr):
    j = pl.program_id(0)
    f = pl.program_id(1)

    @pl.when(j < nu_ref[0])
    def _():
        @pl.when(f == 0)
        def _():
            xb_scr[...] = _as_rows(x_ref[...]).astype(BF16)
            acc_scr[...] = jnp.zeros(acc_scr.shape, F32)

        acc_scr[...] += _swiglu_chunk(xb_scr[...], w1_ref, w3_ref, w2_ref)

        @pl.when(f == pl.num_programs(1) - 1)
        def _():
            o_ref[...] = _as_tiles(acc_scr[...])


def _expert_call(tile_blk, tile_exp, n_used, xg, w1, w3, w2):
    d_ff = w1.shape[2]
    tm = EXPERT_TILE
    tf = 512 if d_ff % 512 == 0 else d_ff
    nf = d_ff // tf
    n_tiles = tile_blk.shape[0]
    fe = lambda j, f, nu: jnp.where(j < nu[0], f, nf - 1)
    tile_block = pl.BlockSpec((tm, D_MODEL // LANES, LANES), lambda j, f, blk, ex, nu: (blk[j], 0, 0))
    grid_spec = pltpu.PrefetchScalarGridSpec(
        num_scalar_prefetch=3,
        grid=(n_tiles, nf),
        in_specs=[tile_block,
                  pl.BlockSpec((None, D_MODEL, tf), lambda j, f, blk, ex, nu: (ex[j], 0, fe(j, f, nu))),
                  pl.BlockSpec((None, D_MODEL, tf), lambda j, f, blk, ex, nu: (ex[j], 0, fe(j, f, nu))),
                  pl.BlockSpec((None, tf, D_MODEL), lambda j, f, blk, ex, nu: (ex[j], fe(j, f, nu), 0))],
        out_specs=tile_block,
        scratch_shapes=[pltpu.VMEM((tm, D_MODEL), BF16), pltpu.VMEM((tm, D_MODEL), F32)])
    return pl.pallas_call(
        _expert_kernel,
        grid_spec=grid_spec,
        out_shape=jax.ShapeDtypeStruct(xg.shape, F32),
        compiler_params=_cparams(("arbitrary", "arbitrary")),
        name="expert_swiglu",
    )(tile_blk, tile_exp, n_used, xg, w1, w3, w2)


def _combine_kernel(h_ref, gate_ref, didx_ref, yg_ref, o_ref, y_scr, dsm_scr, sem_idx, sem_rows):
    i = pl.program_id(0)
    tm = h_ref.shape[0]
    slot = lax.rem(i, 2)

    def fetch(j, sl):
        cp = pltpu.make_async_copy(didx_ref.at[:, pl.ds(pl.multiple_of(j * tm, tm), tm)], dsm_scr,
                                   sem_idx)
        cp.start()
        cp.wait()
        for r in range(tm):
            for s in range(TOP_K):
                pltpu.make_async_copy(yg_ref.at[dsm_scr[s, r]], y_scr.at[sl, s, r],
                                      sem_rows.at[sl]).start(priority=s)

    @pl.when(i == 0)
    def _():
        fetch(0, 0)

    @pl.when(i + 1 < pl.num_programs(0))
    def _():
        fetch(i + 1, 1 - slot)

    for s in range(TOP_K):
        pltpu.make_async_copy(yg_ref.at[pl.ds(0, tm)], y_scr.at[slot, s], sem_rows.at[slot]).wait()
    gate = gate_ref[...]
    o_ref[...] = (h_ref[...] + gate[:, 0:1] * _as_rows(y_scr[slot, 0])
                  + gate[:, 1:2] * _as_rows(y_scr[slot, 1]))


def _combine_call(h, gate, didx, yg):
    t = h.shape[0]
    tm = MOE_TILE
    return pl.pallas_call(
        _combine_kernel,
        grid=(t // tm,),
        in_specs=[pl.BlockSpec((tm, D_MODEL), lambda i: (i, 0)),
                  pl.BlockSpec((tm, LANES), lambda i: (i, 0)),
                  pl.BlockSpec(memory_space=pl.ANY), pl.BlockSpec(memory_space=pl.ANY)],
        out_specs=pl.BlockSpec((tm, D_MODEL), lambda i: (i, 0)),
        out_shape=jax.ShapeDtypeStruct(h.shape, F32),
        scratch_shapes=[pltpu.VMEM((2, TOP_K, tm, D_MODEL // LANES, LANES), F32),
                        pltpu.SMEM((8, tm), I32),
                        pltpu.SemaphoreType.DMA, pltpu.SemaphoreType.DMA((2,))],
        compiler_params=_cparams(("arbitrary",)),
        name="moe_combine",
    )(h, gate, didx, yg)


def _moe_call(h, g, wr, w1, w3, w2):
    t = h.shape[0]
    tm = EXPERT_TILE
    assert t % MOE_TILE == 0 and tm % MOE_TILE == 0
    stride = -(-t // tm) * tm + tm
    xg, didx, gate, cnt = _route_call(h, g, wr, stride)
    counts = cnt[0, :N_EXPERTS].astype(I32)
    tiles_e = (counts + tm - 1) // tm
    cum = jnp.cumsum(tiles_e)
    n_used = cum[-1]
    n_tiles = -(-TOP_K * t // tm) + N_EXPERTS
    jj = jnp.minimum(jnp.arange(n_tiles, dtype=I32), n_used - 1)
    tile_exp = jnp.sum((jj[:, None] >= cum[None, :]).astype(I32), axis=1)
    tile_blk = tile_exp * (stride // tm) + jj - (cum - tiles_e)[tile_exp]
    yg = _expert_call(tile_blk, tile_exp, n_used[None], xg, w1, w3, w2)
    return _combine_call(h, gate, didx, yg)


def _mix_in_weights(w):
    offs = np.concatenate([[0], np.cumsum(MIX_IN_SIZES)])
    cq, ckv, kr, dq, dk, dv, iq, ik, iw = [w[:, offs[j]:offs[j + 1]] for j in range(9)]
    z = lambda n: jnp.zeros((w.shape[0], n), w.dtype)
    half = MLA_ROPE // 2
    kr_main = jnp.concatenate([z(MLA_NOPE), kr, z(LANES - MLA_QK)], axis=1)
    kr_swap = jnp.concatenate([z(MLA_NOPE), kr[:, half:], kr[:, :half], z(LANES - MLA_QK)], axis=1)
    ik4 = jnp.concatenate([ik] * (LANES // IDX_DIM), axis=1)
    iw_p = jnp.concatenate([iw, z(LANES - IDX_HEADS)], axis=1)
    return jnp.concatenate([cq, ckv, kr_main, kr_swap, dq, dk, dv, iq, ik4, iw_p],
                           axis=1).astype(BF16)


def _mla_q_weights(w_uq):
    r = w_uq.shape[0]
    w = w_uq.reshape(r, MLA_HEADS, MLA_QK)
    nope, rope = w[..., :MLA_NOPE], w[..., MLA_NOPE:]
    half = MLA_ROPE // 2
    z = lambda n: jnp.zeros((r, MLA_HEADS, n), w.dtype)
    main = jnp.concatenate([nope, rope, z(LANES - MLA_QK)], axis=-1)
    swap = jnp.concatenate([z(MLA_NOPE), rope[..., half:], rope[..., :half], z(LANES - MLA_QK)],
                           axis=-1)
    return (main.reshape(r, MLA_HEADS * LANES).astype(BF16),
            swap.reshape(r, MLA_HEADS * LANES).astype(BF16))


def _mla_kv_weights(w_ukv):
    r = w_ukv.shape[0]
    w = w_ukv.reshape(r, MLA_HEADS, MLA_NOPE + MLA_V)
    k_nope = jnp.concatenate([w[..., :MLA_NOPE], jnp.zeros((r, MLA_HEADS, LANES - MLA_NOPE), w.dtype)],
                             axis=-1)
    return (k_nope.reshape(r, MLA_HEADS * LANES).astype(BF16),
            w[..., MLA_NOPE:].reshape(r, MLA_HEADS * MLA_V).astype(BF16))


def _qk_gains(g):
    half = MLA_ROPE // 2
    z = lambda n: jnp.zeros((n,), g.dtype)
    main = jnp.concatenate([g, z(LANES - MLA_QK)])
    swap = jnp.concatenate([z(MLA_NOPE), g[MLA_NOPE + half:], g[MLA_NOPE:MLA_NOPE + half],
                            z(LANES - MLA_QK)])
    return main[None, :], swap[None, :]


def _rope_tables(lp):
    half = MLA_ROPE // 2
    inv = ROPE_BASE ** (-jnp.arange(half, dtype=F32) / half)
    ang = jnp.arange(lp, dtype=jnp.int32).astype(F32)[:, None] * inv[None, :]
    cos, sin = jnp.cos(ang), jnp.sin(ang)
    ones = jnp.ones((lp, MLA_NOPE), F32)
    pad1 = jnp.ones((lp, LANES - MLA_QK), F32)
    zeros = jnp.zeros((lp, MLA_NOPE), F32)
    pad0 = jnp.zeros((lp, LANES - MLA_QK), F32)
    return (jnp.concatenate([ones, cos, cos, pad1], axis=1),
            jnp.concatenate([zeros, -sin, sin, pad0], axis=1))


def _rel_buckets(n):
    max_exact = REL_BUCKETS // 2
    d = np.arange(n)
    df = np.maximum(d, 1).astype(np.float32)
    large = max_exact + (np.log(df / np.float32(max_exact))
                         / np.float32(math.log(REL_MAX_DIST / max_exact))
                         * np.float32(REL_BUCKETS - max_exact)).astype(np.int32)
    large = np.minimum(large, REL_BUCKETS - 1)
    return np.where(d < max_exact, d, large)


def _bias_tiles(rel_bias):
    buckets = _rel_buckets(2 * LANES)
    assert np.all(buckets[LANES - 1:] == REL_BUCKETS - 1)
    n_heads = rel_bias.shape[1]
    shifted = (rel_bias - rel_bias[REL_BUCKETS - 1:REL_BUCKETS, :]) * LOG2E
    period = 3 * LANES
    w = jnp.concatenate([jnp.take(shifted, jnp.asarray(buckets), axis=0),
                         jnp.zeros((period - 2 * LANES, n_heads), F32)], axis=0)
    u = jnp.roll(jnp.flip(w, axis=0), -(2 * LANES - 1), axis=0).T
    flat = jnp.tile(u, (1, LANES))[:, :LANES * (period - 1)]
    return flat.reshape(n_heads, LANES, period - 1)[:, :, :2 * LANES].astype(F32)


def kernel(x, meta_tokens, rel_bias, ev_norm_mix, ev_w_mix_in, ev_g_q_lat, ev_g_kv_lat, ev_w_uq,
           ev_w_ukv, ev_mla_q_norm, ev_mla_k_norm, ev_dsa_q_norm, ev_dsa_k_norm, ev_w_mix_out,
           ev_norm_ffn, ev_w1, ev_w3, ev_w2, od_norm_mix, od_w_in, od_conv_w, od_w_out,
           od_norm_ffn, od_w_router, od_w1, od_w3, od_w2):
    b, seq, d = x.shape
    assert d == D_MODEL
    l_tot = seq + N_META
    lp = -(-l_tot // BLOCK_Q) * BLOCK_Q
    assert lp % SEQ_TILE == 0, "sequence tiling assumes the padded length is a multiple of 384"
    top_k = min(DSA_TOPK_MAX, l_tot // 4)
    depth = ev_norm_mix.shape[0] + od_norm_mix.shape[0]

    meta = jnp.broadcast_to(meta_tokens[None].astype(x.dtype), (b, N_META, d))
    h = jnp.concatenate([meta, x, jnp.zeros((b, lp - l_tot, d), x.dtype)], axis=1)
    h = h.reshape(b * lp, d)

    cos_t, sin_t = _rope_tables(lp)
    tz = _bias_tiles(rel_bias)
    row2 = lambda v: v[None, :]

    for layer in range(depth):
        i = layer // 2
        if layer % 2 == 0:
            wqm, wqs = _mla_q_weights(ev_w_uq[i])
            wkk, wkv = _mla_kv_weights(ev_w_ukv[i])
            gqm, gqs = _qk_gains(ev_mla_q_norm[i])
            gkm, gks = _qk_gains(ev_mla_k_norm[i])
            gdq = row2(jnp.concatenate([ev_dsa_q_norm[i]] * 2))
            gdk = row2(jnp.concatenate([ev_dsa_k_norm[i]] * 2))
            qm, km, vm, qd, kd, vd, iq, ik, iw = _prep_call(
                h, row2(ev_norm_mix[i]), _mix_in_weights(ev_w_mix_in[i]), row2(ev_g_q_lat[i]),
                row2(ev_g_kv_lat[i]), wqm, wqs, wkk, wkv, gqm, gqs, gkm, gks, gdq, gdk,
                cos_t, sin_t, lp)
            seq3 = lambda a: a.reshape(b, lp, a.shape[1])
            o_mla = _mla_call(seq3(qm), seq3(km), seq3(vm))
            o_dsa = _dsa_call(seq3(iq), seq3(iw), seq3(ik), seq3(qd), seq3(kd), seq3(vd), tz, top_k)
            w_o = ev_w_mix_out[i].astype(BF16)
            n_mla = MLA_HEADS * MLA_V
            h = _mix_out_call(h, o_mla.reshape(b * lp, -1), o_dsa.reshape(b * lp, -1),
                              w_o[:n_mla], w_o[n_mla:])
            h = _ffn_call(h, row2(ev_norm_ffn[i]), ev_w1[i], ev_w3[i], ev_w2[i])
        else:
            h = _conv_call(h.reshape(b, lp, d), row2(od_norm_mix[i]), od_w_in[i].astype(BF16),
                           od_conv_w[i].reshape(CONV_WIDTH, d), od_w_out[i].astype(BF16))
            h = h.reshape(b * lp, d)
            wr = jnp.concatenate(
                [od_w_router[i], jnp.zeros((d, LANES - N_EXPERTS), od_w_router.dtype)], axis=1)
            h = _moe_call(h, row2(od_norm_ffn[i]), wr.astype(BF16), od_w1[i], od_w3[i], od_w2[i])
    return h.reshape(b, lp, d)[:, N_META:l_tot]
```

```python
import functools
import math

import numpy as np
import jax
import jax.numpy as jnp
from jax import lax
from jax.experimental import pallas as pl
from jax.experimental.pallas import tpu as pltpu

F32 = jnp.float32
BF16 = jnp.bfloat16
I32 = jnp.int32

D_MODEL = 1024
N_META = 16
BLOCK_Q = 128
EPS = 1e-6
MLA_HEADS = 8
MLA_Q_RANK = 384
MLA_KV_RANK = 256
MLA_NOPE = 64
MLA_ROPE = 32
MLA_V = 64
MLA_QK = MLA_NOPE + MLA_ROPE
MLA_SCALE = MLA_QK ** -0.5
ROPE_BASE = 10000.0
DSA_HEADS = 8
DSA_HEAD_DIM = 64
DSA_WIDTH = DSA_HEADS * DSA_HEAD_DIM
DSA_SCALE = DSA_HEAD_DIM ** -0.5
IDX_HEADS = 8
IDX_DIM = 32
DSA_TOPK_MAX = 256
REL_BUCKETS = 32
REL_MAX_DIST = 128
MIX_IN_SIZES = (MLA_Q_RANK, MLA_KV_RANK, MLA_ROPE, DSA_WIDTH, DSA_WIDTH, DSA_WIDTH,
                IDX_HEADS * IDX_DIM, IDX_DIM, IDX_HEADS)
CONV_WIDTH = 3
N_EXPERTS = 8
TOP_K = 2

LANES = 128
VMEM_LIMIT_BYTES = 56 * 1024 * 1024

SEQ_TILE = 3 * LANES
PREP_TILE = 2 * SEQ_TILE
MOE_TILE = 512
EXPERT_TILE = 2 * MOE_TILE
QK_LOOKAHEAD = 2
BITS_PER_EXIT_TEST = 4
LOG2E = math.log2(math.e)
NEG_BIG = -1e30
INT_MIN = -2 ** 31

_C_CQ = 0
_C_CKV = _C_CQ + MLA_Q_RANK
_C_KRM = _C_CKV + MLA_KV_RANK
_C_KRS = _C_KRM + LANES
_C_DQ = _C_KRS + LANES
_C_DK = _C_DQ + DSA_WIDTH
_C_DV = _C_DK + DSA_WIDTH
_C_IQ = _C_DV + DSA_WIDTH
_C_IK = _C_IQ + IDX_HEADS * IDX_DIM
_C_IW = _C_IK + LANES
_C_END = _C_IW + LANES


def _cparams(sem):
    return pltpu.CompilerParams(dimension_semantics=sem, vmem_limit_bytes=VMEM_LIMIT_BYTES)


def _row_tile(n_rows, candidates=(1024, 768, 512, 384, 256, 128)):
    for c in candidates:
        if n_rows % c == 0:
            return c
    raise ValueError(f"no row tile for {n_rows}")


def _rms(x, g):
    ms = jnp.mean(x * x, axis=-1, keepdims=True)
    return x * lax.rsqrt(ms + EPS) * g


def _dot(a, b):
    return jnp.dot(a, b, preferred_element_type=F32)


def _dot_nt(a, b):
    return lax.dot_general(a, b, (((1,), (1,)), ((), ())), preferred_element_type=F32)


def _prep_kernel(h_ref, g_ref, wext_ref, gql_ref, gkvl_ref, wqm_ref, wqs_ref, wkk_ref, wkv_ref,
                 gqm_ref, gqs_ref, gkm_ref, gks_ref, gdq_ref, gdk_ref, cos_ref, sin_ref,
                 qm_o, km_o, vm_o, qd_o, kd_o, vd_o, iq_o, ik_o, iw_o):
    xn = _rms(h_ref[...], g_ref[...]).astype(BF16)

    def proj(lo, hi):
        return _dot(xn, wext_ref[:, lo:hi])

    cos = cos_ref[...]
    sin = sin_ref[...]
    lane = lax.broadcasted_iota(I32, (xn.shape[0], LANES), 1)

    cqn = _rms(proj(_C_CQ, _C_CKV), gql_ref[...]).astype(BF16)
    q_main = _dot(cqn, wqm_ref[...])
    q_swap = _dot(cqn, wqs_ref[...])
    for hd in range(MLA_HEADS):
        sl = slice(hd * LANES, (hd + 1) * LANES)
        a = q_main[:, sl]
        r = lax.rsqrt(jnp.sum(a * a, axis=-1, keepdims=True) * (1.0 / MLA_QK) + EPS)
        out = (a * r * gqm_ref[...]) * cos + (q_swap[:, sl] * r * gqs_ref[...]) * sin
        qm_o[:, sl] = (out * (MLA_SCALE * LOG2E)).astype(BF16)

    ckvn = _rms(proj(_C_CKV, _C_KRM), gkvl_ref[...]).astype(BF16)
    k_nope = _dot(ckvn, wkk_ref[...])
    vm_o[...] = _dot(ckvn, wkv_ref[...]).astype(BF16)
    kr_main = proj(_C_KRM, _C_KRS)
    kr_swap = proj(_C_KRS, _C_DQ)
    for hd in range(MLA_HEADS):
        sl = slice(hd * LANES, (hd + 1) * LANES)
        a = k_nope[:, sl] + kr_main
        r = lax.rsqrt(jnp.sum(a * a, axis=-1, keepdims=True) * (1.0 / MLA_QK) + EPS)
        out = (a * r * gkm_ref[...]) * cos + (kr_swap * r * gks_ref[...]) * sin
        km_o[:, sl] = out.astype(BF16)

    first = lane < DSA_HEAD_DIM
    for (lo, g2_ref, o_ref, post) in ((_C_DQ, gdq_ref, qd_o, DSA_SCALE * LOG2E),
                                      (_C_DK, gdk_ref, kd_o, None)):
        for pr in range(DSA_HEADS // 2):
            x = proj(lo + pr * LANES, lo + (pr + 1) * LANES)
            sq = x * x
            s0 = jnp.sum(jnp.where(first, sq, 0.0), axis=-1, keepdims=True)
            s1 = jnp.sum(jnp.where(first, 0.0, sq), axis=-1, keepdims=True)
            r0 = lax.rsqrt(s0 * (1.0 / DSA_HEAD_DIM) + EPS)
            r1 = lax.rsqrt(s1 * (1.0 / DSA_HEAD_DIM) + EPS)
            out = x * jnp.where(first, r0, r1) * g2_ref[...]
            if post is None:
                o_ref[:, pr * LANES:(pr + 1) * LANES] = out.astype(BF16)
            else:
                out = out * post
                o_ref[:, (2 * pr) * LANES:(2 * pr + 1) * LANES] = jnp.where(first, out, 0.0).astype(BF16)
                o_ref[:, (2 * pr + 1) * LANES:(2 * pr + 2) * LANES] = jnp.where(first, 0.0, out).astype(BF16)

    vd_o[...] = proj(_C_DV, _C_IQ).astype(BF16)
    lane_group = lax.shift_right_logical(lane, int(math.log2(IDX_DIM)))
    for quad in range(IDX_HEADS * IDX_DIM // LANES):
        x = proj(_C_IQ + quad * LANES, _C_IQ + (quad + 1) * LANES)
        for j in range(LANES // IDX_DIM):
            hd = quad * (LANES // IDX_DIM) + j
            iq_o[:, hd * LANES:(hd + 1) * LANES] = jnp.where(lane_group == j, x, 0.0).astype(BF16)
    ik_o[...] = proj(_C_IK, _C_IW).astype(BF16)
    iw_o[...] = proj(_C_IW, _C_END)


def _prep_call(h, g_mix, wext, gql, gkvl, wqm, wqs, wkk, wkv, gqm, gqs, gkm, gks, gdq, gdk,
               cos_t, sin_t, lp):
    t = h.shape[0]
    tm = PREP_TILE
    assert t % tm == 0
    reps = tm // math.gcd(lp, tm)
    nt = reps * lp // tm
    cos_t, sin_t = jnp.tile(cos_t, (reps, 1)), jnp.tile(sin_t, (reps, 1))
    row = lambda w: pl.BlockSpec((tm, w), lambda i: (i, 0))
    full = lambda a: pl.BlockSpec(a.shape, lambda i: (0, 0), pipeline_mode=pl.Buffered(1))
    tab = pl.BlockSpec((tm, LANES), lambda i: (i % nt, 0))
    hw = MLA_HEADS * LANES
    out_shape = [
        jax.ShapeDtypeStruct((t, hw), BF16), jax.ShapeDtypeStruct((t, hw), BF16),
        jax.ShapeDtypeStruct((t, MLA_HEADS * MLA_V), BF16),
        jax.ShapeDtypeStruct((t, DSA_HEADS * LANES), BF16), jax.ShapeDtypeStruct((t, DSA_WIDTH), BF16),
        jax.ShapeDtypeStruct((t, DSA_WIDTH), BF16),
        jax.ShapeDtypeStruct((t, IDX_HEADS * LANES), BF16),
        jax.ShapeDtypeStruct((t, LANES), BF16), jax.ShapeDtypeStruct((t, LANES), F32),
    ]
    return pl.pallas_call(
        _prep_kernel,
        grid=(t // tm,),
        in_specs=[row(D_MODEL), full(g_mix), full(wext), full(gql), full(gkvl), full(wqm),
                  full(wqs), full(wkk), full(wkv), full(gqm), full(gqs), full(gkm), full(gks),
                  full(gdq), full(gdk), tab, tab],
        out_specs=[row(s.shape[1]) for s in out_shape],
        out_shape=out_shape,
        compiler_params=_cparams(("parallel",)),
        name="prep_mix_in",
    )(h, g_mix, wext, gql, gkvl, wqm, wqs, wkk, wkv, gqm, gqs, gkm, gks, gdq, gdk, cos_t, sin_t)


def _flash_init(m_scr, l_scr, acc_scr):
    m_scr[...] = jnp.full(m_scr.shape, -jnp.inf, F32)
    l_scr[...] = jnp.zeros(l_scr.shape, F32)
    acc_scr[...] = jnp.zeros(acc_scr.shape, F32)


def _lane_fold(x, op):
    return functools.reduce(op, [x[:, j * LANES:(j + 1) * LANES] for j in range(x.shape[1] // LANES)])


def _flash_update(hd, s, vc, m_scr, l_scr, acc_scr):
    m_old = m_scr[hd]
    m_new = jnp.maximum(m_old, jnp.max(_lane_fold(s, jnp.maximum), axis=-1, keepdims=True))
    alpha = jnp.exp2(m_old - m_new)
    p = jnp.exp2(s - jnp.concatenate([m_new] * (s.shape[1] // LANES), axis=1))
    m_scr[hd] = m_new
    l_scr[hd] = alpha * l_scr[hd] + _lane_fold(p, jnp.add)
    acc_scr[hd] = alpha * acc_scr[hd] + _dot(p.astype(BF16), vc)


def _flash_store(o_ref, l_scr, acc_scr, head_dim):
    lane = lax.broadcasted_iota(I32, acc_scr.shape[1:], 1)
    for pr in range(acc_scr.shape[0] // 2):
        o0, o1 = [acc_scr[hd] / jnp.sum(l_scr[hd], axis=-1, keepdims=True)
                  for hd in (2 * pr, 2 * pr + 1)]
        o_ref[:, pr * LANES:(pr + 1) * LANES] = jnp.where(lane < head_dim, o0, o1).astype(o_ref.dtype)


def _mla_kernel(q_ref, k_ref, v_ref, o_ref, m_scr, l_scr, acc_scr):
    qi = pl.program_id(1)
    tq = q_ref.shape[0]
    _flash_init(m_scr, l_scr, acc_scr)

    def chunk(c, diagonal):
        ks = pl.ds(pl.multiple_of(c * tq, tq), tq)

        def scores(hd):
            sl = slice(hd * LANES, (hd + 1) * LANES)
            return _dot_nt(q_ref[:, sl], k_ref[ks, sl])

        pending = [scores(hd) for hd in range(QK_LOOKAHEAD)]
        for hd in range(MLA_HEADS):
            if hd + QK_LOOKAHEAD < MLA_HEADS:
                pending.append(scores(hd + QK_LOOKAHEAD))
            s = pending.pop(0)
            if diagonal:
                row = lax.broadcasted_iota(I32, (tq, tq), 0)
                col = lax.broadcasted_iota(I32, (tq, tq), 1)
                s = jnp.where(col <= row, s, NEG_BIG)
            pr = hd // 2
            _flash_update(hd, s, v_ref[ks, pr * LANES:(pr + 1) * LANES], m_scr, l_scr, acc_scr)

    def body(c, carry):
        chunk(c, False)
        return carry

    lax.fori_loop(0, qi, body, 0)
    chunk(qi, True)
    _flash_store(o_ref, l_scr, acc_scr, MLA_V)


def _mla_call(q, k, v):
    b, lp, _ = q.shape
    tq = SEQ_TILE
    nq = lp // tq
    state = pltpu.VMEM((MLA_HEADS, tq, LANES), F32)
    return pl.pallas_call(
        _mla_kernel,
        grid=(b, nq),
        in_specs=[pl.BlockSpec((None, tq, q.shape[2]), lambda bi, i: (bi, i, 0)),
                  pl.BlockSpec((None, lp, k.shape[2]), lambda bi, i: (bi, 0, 0)),
                  pl.BlockSpec((None, lp, v.shape[2]), lambda bi, i: (bi, 0, 0))],
        out_specs=pl.BlockSpec((None, tq, v.shape[2]), lambda bi, i: (bi, i, 0)),
        out_shape=jax.ShapeDtypeStruct((b, lp, v.shape[2]), BF16),
        scratch_shapes=[state, state, state],
        compiler_params=_cparams(("parallel", "arbitrary")),
        name="mla_attention",
    )(q, k, v)


def _dsa_kernel(iq_ref, iw_ref, ik_ref, q_ref, k_ref, v_ref, tz_ref, o_ref,
                key_scr, cap_scr, wt_scr, ans_scr, jst_scr, m_scr, l_scr, acc_scr,
                *, top_k):
    qi = pl.program_id(1)
    tq = q_ref.shape[0]
    nsub = tq // LANES
    sublanes = 8
    n_chunks = qi + 1
    key_i = lax.broadcasted_iota(I32, (tq, tq), 0)
    qry_i = lax.broadcasted_iota(I32, (tq, tq), 1)
    chunk_rows = lambda c: pl.ds(pl.multiple_of(c * tq, tq), tq)
    head_slot = lambda hd: slice(hd * LANES, (hd + 1) * LANES)
    wt_scr[...] = jnp.transpose(iw_ref[...])[0:IDX_HEADS, :]

    def index_chunk(c, diagonal):
        ikc = ik_ref[chunk_rows(c), :]
        sc = jnp.zeros((tq, tq), F32)
        for hd in range(IDX_HEADS):
            act = jnp.maximum(_dot_nt(ikc, iq_ref[:, head_slot(hd)]), 0.0)
            sc = sc + wt_scr[hd:hd + 1, :] * act
        sc = jnp.where(sc == 0.0, 0.0, sc)
        if diagonal:
            sc = jnp.where(key_i <= qry_i, sc, -jnp.inf)
        bits = pltpu.bitcast(sc, I32)
        key_scr[c] = bits ^ (lax.shift_right_arithmetic(bits, 31) & 0x7FFFFFFF)

    def index_body(c, carry):
        index_chunk(c, False)
        return carry

    lax.fori_loop(0, qi, index_body, 0)
    index_chunk(qi, True)

    kf = float(top_k)
    vec = (sublanes, tq)
    sub_i = lax.broadcasted_iota(I32, vec, 0)

    def count(pred):
        def body(c, accs):
            accs = list(accs)
            kk = key_scr[c]
            for g in range(tq // sublanes):
                hit = jnp.where(pred(kk[g * sublanes:(g + 1) * sublanes, :], c * tq + g * sublanes),
                                1.0, 0.0)
                accs[g % 2] = accs[g % 2] + hit
            return tuple(accs)
        zero = jnp.zeros(vec, F32)
        a0, a1 = lax.fori_loop(0, n_chunks, body, (zero, zero))
        return jnp.broadcast_to(jnp.sum(a0 + a1, axis=0, keepdims=True), vec)

    def bisect_step(bit, ans, done):
        cand = ans + lax.shift_left(jnp.int32(1), bit)
        cnt = count(lambda kk, _i: kk >= cand)
        open_ = done == 0
        ans = jnp.where(jnp.logical_and(open_, cnt >= kf), cand, ans)
        done = jnp.where(jnp.logical_and(open_, cnt == kf), 1, done)
        return ans, done

    def n_open_of(done):
        return jnp.sum(jnp.where(done == 0, 1.0, 0.0))

    def bis_cond(st):
        bit, _, _, n_open = st
        return jnp.logical_and(bit >= 0, n_open > 0.0)

    def bis_body(st):
        bit, ans, done, _ = st
        for k in range(BITS_PER_EXIT_TEST):
            ans, done = bisect_step(bit - k, ans, done)
        return bit - BITS_PER_EXIT_TEST, ans, done, n_open_of(done)

    t_pos = qi * tq + lax.broadcasted_iota(I32, vec, 1)
    done0 = (t_pos + 1 <= top_k).astype(I32)
    ans0 = jnp.full(vec, INT_MIN, I32)
    _, ans, done, n_open = lax.while_loop(bis_cond, bis_body,
                                          (jnp.int32(31), ans0, done0, n_open_of(done0)))
    ans_scr[...] = ans
    jst_scr[...] = jnp.full(vec, 2 ** 31 - 1, I32)

    @pl.when(n_open > 0.0)
    def _():
        need = kf - count(lambda kk, _i: kk > ans)

        def tie_body(i, jst):
            cand = jst + lax.shift_left(jnp.int32(1), 13 - i)
            cnt = count(lambda kk, i0: jnp.logical_and(kk == ans, sub_i + i0 < cand))
            return jnp.where(cnt < need, cand, jst)

        jst = lax.fori_loop(0, 14, tie_body, jnp.zeros(vec, I32))
        jst_scr[...] = jnp.where(done == 0, jst, 2 ** 31 - 1)

    ans_row = ans_scr[0:1, :]
    jst_row = jst_scr[0:1, :]

    def mask_chunk(c, diagonal):
        kk = key_scr[c]
        sel = jnp.logical_or(kk > ans_row,
                             jnp.logical_and(kk == ans_row, key_i + c * tq <= jst_row))
        if diagonal:
            sel = jnp.logical_and(sel, key_i <= qry_i)
        cap_scr[c] = jnp.transpose(jnp.where(sel, -NEG_BIG, NEG_BIG))

    def mask_body(c, carry):
        mask_chunk(c, False)
        return carry

    lax.fori_loop(0, qi, mask_body, 0)
    mask_chunk(qi, True)

    _flash_init(m_scr, l_scr, acc_scr)

    def bias_of(hd, where_):
        near = tz_ref[hd, :, LANES:2 * LANES]
        far = tz_ref[hd, :, 0:LANES]
        z = jnp.zeros((LANES, LANES), F32)
        if where_ == "previous":
            blocks = [[far if (a == 0 and b == nsub - 1) else z for b in range(nsub)]
                      for a in range(nsub)]
        else:
            blocks = [[near if b == a else far if b == a - 1 else z for b in range(nsub)]
                      for a in range(nsub)]
        return jnp.concatenate([jnp.concatenate(r, axis=1) for r in blocks], axis=0)

    def chunk(c, where_):
        ks = chunk_rows(c)
        cap = cap_scr[c]
        pair = lambda hd: slice((hd // 2) * LANES, (hd // 2 + 1) * LANES)
        scores = lambda hd: _dot_nt(q_ref[:, head_slot(hd)], k_ref[ks, pair(hd)])

        pending = [scores(hd) for hd in range(QK_LOOKAHEAD)]
        for hd in range(DSA_HEADS):
            if hd + QK_LOOKAHEAD < DSA_HEADS:
                pending.append(scores(hd + QK_LOOKAHEAD))
            s = jnp.minimum(pending.pop(0), cap)
            if where_ is not None:
                s = s + bias_of(hd, where_)
            _flash_update(hd, s, v_ref[ks, pair(hd)], m_scr, l_scr, acc_scr)

    def body(c, carry):
        chunk(c, None)
        return carry

    lax.fori_loop(0, jnp.maximum(qi - 1, 0), body, 0)

    @pl.when(qi >= 1)
    def _():
        chunk(qi - 1, "previous")

    chunk(qi, "diagonal")
    _flash_store(o_ref, l_scr, acc_scr, DSA_HEAD_DIM)


def _dsa_call(iq, iw, ik, q, k, v, tz, top_k):
    b, lp, _ = q.shape
    tq = SEQ_TILE
    nq = lp // tq
    qspec = lambda w: pl.BlockSpec((None, tq, w), lambda bi, i: (bi, i, 0))
    kspec = lambda w: pl.BlockSpec((None, lp, w), lambda bi, i: (bi, 0, 0))
    state = pltpu.VMEM((DSA_HEADS, tq, LANES), F32)
    return pl.pallas_call(
        functools.partial(_dsa_kernel, top_k=top_k),
        grid=(b, nq),
        in_specs=[qspec(iq.shape[2]), qspec(LANES), kspec(LANES), qspec(q.shape[2]),
                  kspec(DSA_WIDTH), kspec(DSA_WIDTH),
                  pl.BlockSpec(tz.shape, lambda bi, i: (0, 0, 0))],
        out_specs=qspec(DSA_WIDTH),
        out_shape=jax.ShapeDtypeStruct((b, lp, DSA_WIDTH), BF16),
        scratch_shapes=[pltpu.VMEM((nq, tq, tq), I32), pltpu.VMEM((nq, tq, tq), F32),
                        pltpu.VMEM((IDX_HEADS, tq), F32),
                        pltpu.VMEM((8, tq), I32), pltpu.VMEM((8, tq), I32),
                        state, state, state],
        compiler_params=_cparams(("parallel", "arbitrary")),
        name="dsa_attention",
    )(iq, iw, ik, q, k, v, tz)


def _swiglu_chunk(xb, w1_ref, w3_ref, w2_ref):
    a = _dot(xb, w1_ref[...].astype(BF16))
    act = (a * jax.nn.sigmoid(a)) * _dot(xb, w3_ref[...].astype(BF16))
    return _dot(act.astype(BF16), w2_ref[...].astype(BF16))


def _mix_out_ffn_kernel(h_ref, a_ref, b_ref, wa_ref, wb_ref, g_ref, w1_ref, w3_ref, w2_ref, o_ref,
                        h1_scr, xn_scr, acc_scr):
    f = pl.program_id(1)

    @pl.when(f == 0)
    def _():
        h1 = h_ref[...] + _dot(a_ref[...], wa_ref[...]) + _dot(b_ref[...], wb_ref[...])
        h1_scr[...] = h1
        xn_scr[...] = _rms(h1, g_ref[...]).astype(BF16)
        acc_scr[...] = jnp.zeros(acc_scr.shape, F32)

    acc_scr[...] += _swiglu_chunk(xn_scr[...], w1_ref, w3_ref, w2_ref)

    @pl.when(f == pl.num_programs(1) - 1)
    def _():
        o_ref[...] = h1_scr[...] + acc_scr[...]


def _mix_out_ffn_call(h, a, b, wa, wb, g, w1, w3, w2):
    t = h.shape[0]
    d_ff = w1.shape[1]
    tm = _row_tile(t)
    tf = 512 if d_ff % 512 == 0 else d_ff
    row = lambda w: pl.BlockSpec((tm, w), lambda i, f: (i, 0))
    full = lambda x: pl.BlockSpec(x.shape, lambda i, f: (0, 0), pipeline_mode=pl.Buffered(1))
    return pl.pallas_call(
        _mix_out_ffn_kernel,
        grid=(t // tm, d_ff // tf),
        in_specs=[row(D_MODEL), row(a.shape[1]), row(b.shape[1]), full(wa), full(wb), full(g),
                  pl.BlockSpec((D_MODEL, tf), lambda i, f: (0, f)),
                  pl.BlockSpec((D_MODEL, tf), lambda i, f: (0, f)),
                  pl.BlockSpec((tf, D_MODEL), lambda i, f: (f, 0))],
        out_specs=row(D_MODEL),
        out_shape=jax.ShapeDtypeStruct(h.shape, F32),
        scratch_shapes=[pltpu.VMEM((tm, D_MODEL), F32), pltpu.VMEM((tm, D_MODEL), BF16),
                        pltpu.VMEM((tm, D_MODEL), F32)],
        compiler_params=_cparams(("parallel", "arbitrary")),
        name="mix_out_swiglu",
    )(h, a, b, wa, wb, g, w1, w3, w2)


def _conv_kernel(h_ref, g_ref, win_ref, cw_ref, wout_ref, o_ref, z_scr):
    i = pl.program_id(1)
    tm = h_ref.shape[0]
    halo = 8
    x = h_ref[...]
    xn = _rms(x, g_ref[...]).astype(BF16)
    c_gate = _dot(xn, win_ref[:, D_MODEL:2 * D_MODEL])
    u = _dot(xn, win_ref[:, 2 * D_MODEL:3 * D_MODEL])
    z = c_gate * u

    @pl.when(i == 0)
    def _():
        z_scr[0:halo, :] = jnp.zeros((halo, D_MODEL), F32)

    @pl.when(i > 0)
    def _():
        z_scr[0:halo, :] = z_scr[tm:tm + halo, :]

    z_scr[halo:halo + tm, :] = z
    cw = cw_ref[...]
    y = (cw[0:1, :] * z_scr[halo - 2:halo - 2 + tm, :]
         + cw[1:2, :] * z_scr[halo - 1:halo - 1 + tm, :]
         + cw[2:3, :] * z)
    b_gate = _dot(xn, win_ref[:, 0:D_MODEL])
    o_ref[...] = x + _dot((b_gate * y).astype(BF16), wout_ref[...])


def _conv_call(h3, g, w_in, cw, w_out):
    b, lp, d = h3.shape
    tm = SEQ_TILE
    full = lambda a: pl.BlockSpec(a.shape, lambda bi, i: (0,) * a.ndim)
    return pl.pallas_call(
        _conv_kernel,
        grid=(b, lp // tm),
        in_specs=[pl.BlockSpec((None, tm, d), lambda bi, i: (bi, i, 0)), full(g), full(w_in),
                  full(cw), full(w_out)],
        out_specs=pl.BlockSpec((None, tm, d), lambda bi, i: (bi, i, 0)),
        out_shape=jax.ShapeDtypeStruct(h3.shape, F32),
        scratch_shapes=[pltpu.VMEM((tm + 8, d), F32)],
        compiler_params=_cparams(("arbitrary", "arbitrary")),
        name="short_conv_mixer",
    )(h3, g, w_in, cw, w_out)


def _as_tiles(x):
    return x.reshape(x.shape[0], D_MODEL // LANES, LANES)


def _as_rows(x3):
    return x3.reshape(x3.shape[0], D_MODEL)


def _route_kernel(h_ref, g_ref, wr_ref, xg_ref, didx_ref, gate_ref, cnt_ref,
                  xs_scr, base_scr, dvm_scr, dsm_scr, sem_idx, sem_rows, *, stride):
    i = pl.program_id(0)
    tm = h_ref.shape[0]
    lane = lax.broadcasted_iota(I32, (tm, LANES), 1)
    lane_f = lane.astype(F32)

    @pl.when(i == 0)
    def _():
        base_scr[...] = jnp.zeros(base_scr.shape, F32)

    xnf = _rms(h_ref[...], g_ref[...])
    logits = jnp.where(lane < N_EXPERTS, _dot(xnf.astype(BF16), wr_ref[...]), -jnp.inf)
    v1 = jnp.max(logits, axis=-1, keepdims=True)
    i1 = jnp.min(jnp.where(logits == v1, lane_f, float(LANES)), axis=-1, keepdims=True)
    rest = jnp.where(lane_f == i1, -jnp.inf, logits)
    v2 = jnp.max(rest, axis=-1, keepdims=True)
    i2 = jnp.min(jnp.where(rest == v2, lane_f, float(LANES)), axis=-1, keepdims=True)
    e2 = jnp.exp(v2 - v1)
    den = 1.0 + e2
    gates = (1.0 / den, e2 / den)

    oh1 = jnp.where(lane_f == i1, 1.0, 0.0)
    oh2 = jnp.where(lane_f == i2, 1.0, 0.0)
    oh = oh1 + oh2
    earlier = (lax.broadcasted_iota(I32, (tm, tm), 1) < lax.broadcasted_iota(I32, (tm, tm), 0))
    prefix = _dot(jnp.where(earlier, 1.0, 0.0).astype(BF16), oh.astype(BF16))
    pos = base_scr[0:1, :] + prefix
    d1 = jnp.sum(oh1 * pos, axis=-1, keepdims=True) + i1 * float(stride)
    d2 = jnp.sum(oh2 * pos, axis=-1, keepdims=True) + i2 * float(stride)
    base_scr[...] = base_scr[...] + jnp.sum(oh, axis=0, keepdims=True)
    cnt_ref[...] = base_scr[...]

    gate_ref[...] = jnp.where(lane == 0, gates[0], jnp.where(lane == 1, gates[1], 0.0))
    slot = lax.rem(i, 2)
    xs_scr[slot] = _as_tiles(xnf)

    dmat = jnp.where(lane == 0, d1, jnp.where(lane == 1, d2, 0.0))
    dvm_scr[...] = jnp.transpose(dmat)[0:8, :].astype(I32)
    didx_ref[...] = dvm_scr[...]
    cp = pltpu.make_async_copy(dvm_scr, dsm_scr, sem_idx)
    cp.start()
    cp.wait()

    for r in range(tm):
        for s in range(TOP_K):
            pltpu.make_async_copy(xs_scr.at[slot, r], xg_ref.at[dsm_scr[s, r]],
                                  sem_rows.at[slot]).start(priority=s)

    def wait_rows(sl):
        for s in range(TOP_K):
            pltpu.make_async_copy(xs_scr.at[sl], xg_ref.at[pl.ds(0, tm)], sem_rows.at[sl]).wait()

    @pl.when(i > 0)
    def _():
        wait_rows(1 - slot)

    @pl.when(i == pl.num_programs(0) - 1)
    def _():
        wait_rows(slot)
        xs_scr[0] = jnp.zeros(xs_scr.shape[1:], F32)
        dvm_scr[:, 0:LANES] = base_scr[...].astype(I32)
        cp2 = pltpu.make_async_copy(dvm_scr, dsm_scr, sem_idx)
        cp2.start()
        cp2.wait()
        for phase in ("start", "wait"):
            for e in range(N_EXPERTS):
                for k in range(EXPERT_TILE // tm):
                    blk = pltpu.make_async_copy(
                        xs_scr.at[0], xg_ref.at[pl.ds(e * stride + dsm_scr[0, e] + k * tm, tm)],
                        sem_rows.at[0])
                    blk.start() if phase == "start" else blk.wait()


def _route_call(h, g, wr, stride):
    t = h.shape[0]
    tm = MOE_TILE
    return pl.pallas_call(
        functools.partial(_route_kernel, stride=stride),
        grid=(t // tm,),
        in_specs=[pl.BlockSpec((tm, D_MODEL), lambda i: (i, 0)),
                  pl.BlockSpec(g.shape, lambda i: (0, 0)),
                  pl.BlockSpec(wr.shape, lambda i: (0, 0))],
        out_specs=[pl.BlockSpec(memory_space=pl.ANY),
                   pl.BlockSpec((8, tm), lambda i: (0, i)),
                   pl.BlockSpec((tm, LANES), lambda i: (i, 0)),
                   pl.BlockSpec((8, LANES), lambda i: (0, 0))],
        out_shape=[jax.ShapeDtypeStruct((N_EXPERTS * stride, D_MODEL // LANES, LANES), F32),
                   jax.ShapeDtypeStruct((8, t), I32),
                   jax.ShapeDtypeStruct((t, LANES), F32),
                   jax.ShapeDtypeStruct((8, LANES), F32)],
        scratch_shapes=[pltpu.VMEM((2, tm, D_MODEL // LANES, LANES), F32), pltpu.VMEM((8, LANES), F32),
                        pltpu.VMEM((8, tm), I32), pltpu.SMEM((8, tm), I32),
                        pltpu.SemaphoreType.DMA, pltpu.SemaphoreType.DMA((2,))],
        compiler_params=_cparams(("arbitrary",)),
        name="moe_route",
    )(h, g, wr)


def _expert_kernel(blk_ref, exp_ref, nu_ref, x_ref, w1_ref, w3_ref, w2_ref, o_ref, xb_scr, acc_scr):
    j = pl.program_id(0)
    f = pl.program_id(1)

    @pl.when(j < nu_ref[0])
    def _():
        @pl.when(f == 0)
        def _():
            xb_scr[...] = _as_rows(x_ref[...]).astype(BF16)
            acc_scr[...] = jnp.zeros(acc_scr.shape, F32)

        acc_scr[...] += _swiglu_chunk(xb_scr[...], w1_ref, w3_ref, w2_ref)

        @pl.when(f == pl.num_programs(1) - 1)
        def _():
            o_ref[...] = _as_tiles(acc_scr[...])


def _expert_call(tile_blk, tile_exp, n_used, xg, w1, w3, w2):
    d_ff = w1.shape[2]
    tm = EXPERT_TILE
    tf = 512 if d_ff % 512 == 0 else d_ff
    nf = d_ff // tf
    n_tiles = tile_blk.shape[0]
    fe = lambda j, f, nu: jnp.where(j < nu[0], f, nf - 1)
    tile_block = pl.BlockSpec((tm, D_MODEL // LANES, LANES), lambda j, f, blk, ex, nu: (blk[j], 0, 0))
    grid_spec = pltpu.PrefetchScalarGridSpec(
        num_scalar_prefetch=3,
        grid=(n_tiles, nf),
        in_specs=[tile_block,
                  pl.BlockSpec((None, D_MODEL, tf), lambda j, f, blk, ex, nu: (ex[j], 0, fe(j, f, nu))),
                  pl.BlockSpec((None, D_MODEL, tf), lambda j, f, blk, ex, nu: (ex[j], 0, fe(j, f, nu))),
                  pl.BlockSpec((None, tf, D_MODEL), lambda j, f, blk, ex, nu: (ex[j], fe(j, f, nu), 0))],
        out_specs=tile_block,
        scratch_shapes=[pltpu.VMEM((tm, D_MODEL), BF16), pltpu.VMEM((tm, D_MODEL), F32)])
    return pl.pallas_call(
        _expert_kernel,
        grid_spec=grid_spec,
        out_shape=jax.ShapeDtypeStruct(xg.shape, F32),
        compiler_params=_cparams(("arbitrary", "arbitrary")),
        name="expert_swiglu",
    )(tile_blk, tile_exp, n_used, xg, w1, w3, w2)


def _combine_kernel(h_ref, gate_ref, didx_ref, yg_ref, o_ref, y_scr, dsm_scr, sem_idx, sem_rows):
    i = pl.program_id(0)
    tm = h_ref.shape[0]
    slot = lax.rem(i, 2)

    def fetch(j, sl):
        cp = pltpu.make_async_copy(didx_ref.at[:, pl.ds(pl.multiple_of(j * tm, tm), tm)], dsm_scr,
                                   sem_idx)
        cp.start()
        cp.wait()
        for r in range(tm):
            for s in range(TOP_K):
                pltpu.make_async_copy(yg_ref.at[dsm_scr[s, r]], y_scr.at[sl, s, r],
                                      sem_rows.at[sl]).start(priority=s)

    @pl.when(i == 0)
    def _():
        fetch(0, 0)

    @pl.when(i + 1 < pl.num_programs(0))
    def _():
        fetch(i + 1, 1 - slot)

    for s in range(TOP_K):
        pltpu.make_async_copy(yg_ref.at[pl.ds(0, tm)], y_scr.at[slot, s], sem_rows.at[slot]).wait()
    gate = gate_ref[...]
    o_ref[...] = (h_ref[...] + gate[:, 0:1] * _as_rows(y_scr[slot, 0])
                  + gate[:, 1:2] * _as_rows(y_scr[slot, 1]))


def _combine_call(h, gate, didx, yg, rows=None):
    t = h.shape[0]
    tm = MOE_TILE
    n_seq, seq_stride, first, n = rows if rows is not None else (1, 0, 0, t)
    assert n % tm == 0 and first % 8 == 0 and seq_stride % 8 == 0
    tiles = n // tm
    if rows is not None:
        didx = didx.reshape(8, n_seq, seq_stride)[:, :, first:first + n].reshape(8, n_seq * n)

    def token_rows(width):
        start = lambda i: pl.multiple_of((i // tiles) * seq_stride + first + (i % tiles) * tm, 8)
        return pl.BlockSpec((pl.Element(tm), pl.Element(width)), lambda i: (start(i), 0))

    return pl.pallas_call(
        _combine_kernel,
        grid=(n_seq * tiles,),
        in_specs=[token_rows(D_MODEL), token_rows(LANES),
                  pl.BlockSpec(memory_space=pl.ANY), pl.BlockSpec(memory_space=pl.ANY)],
        out_specs=pl.BlockSpec((tm, D_MODEL), lambda i: (i, 0)),
        out_shape=jax.ShapeDtypeStruct((n_seq * n, D_MODEL), F32),
        scratch_shapes=[pltpu.VMEM((2, TOP_K, tm, D_MODEL // LANES, LANES), F32),
                        pltpu.SMEM((8, tm), I32),
                        pltpu.SemaphoreType.DMA, pltpu.SemaphoreType.DMA((2,))],
        compiler_params=_cparams(("arbitrary",)),
        name="moe_combine",
    )(h, gate, didx, yg)


def _moe_call(h, g, wr, w1, w3, w2, rows=None):
    t = h.shape[0]
    tm = EXPERT_TILE
    assert t % MOE_TILE == 0 and tm % MOE_TILE == 0
    stride = -(-t // tm) * tm + tm
    xg, didx, gate, cnt = _route_call(h, g, wr, stride)
    counts = cnt[0, :N_EXPERTS].astype(I32)
    tiles_e = (counts + tm - 1) // tm
    cum = jnp.cumsum(tiles_e)
    n_used = cum[-1]
    n_tiles = -(-TOP_K * t // tm) + N_EXPERTS
    jj = jnp.minimum(jnp.arange(n_tiles, dtype=I32), n_used - 1)
    tile_exp = jnp.sum((jj[:, None] >= cum[None, :]).astype(I32), axis=1)
    tile_blk = tile_exp * (stride // tm) + jj - (cum - tiles_e)[tile_exp]
    yg = _expert_call(tile_blk, tile_exp, n_used[None], xg, w1, w3, w2)
    return _combine_call(h, gate, didx, yg, rows)


def _mix_in_weights(w):
    offs = np.concatenate([[0], np.cumsum(MIX_IN_SIZES)])
    cq, ckv, kr, dq, dk, dv, iq, ik, iw = [w[:, offs[j]:offs[j + 1]] for j in range(9)]
    z = lambda n: jnp.zeros((w.shape[0], n), w.dtype)
    half = MLA_ROPE // 2
    kr_main = jnp.concatenate([z(MLA_NOPE), kr, z(LANES - MLA_QK)], axis=1)
    kr_swap = jnp.concatenate([z(MLA_NOPE), kr[:, half:], kr[:, :half], z(LANES - MLA_QK)], axis=1)
    ik4 = jnp.concatenate([ik] * (LANES // IDX_DIM), axis=1)
    iw_p = jnp.concatenate([iw, z(LANES - IDX_HEADS)], axis=1)
    return jnp.concatenate([cq, ckv, kr_main, kr_swap, dq, dk, dv, iq, ik4, iw_p],
                           axis=1).astype(BF16)


def _mla_q_weights(w_uq):
    r = w_uq.shape[0]
    w = w_uq.reshape(r, MLA_HEADS, MLA_QK)
    nope, rope = w[..., :MLA_NOPE], w[..., MLA_NOPE:]
    half = MLA_ROPE // 2
    z = lambda n: jnp.zeros((r, MLA_HEADS, n), w.dtype)
    main = jnp.concatenate([nope, rope, z(LANES - MLA_QK)], axis=-1)
    swap = jnp.concatenate([z(MLA_NOPE), rope[..., half:], rope[..., :half], z(LANES - MLA_QK)],
                           axis=-1)
    return (main.reshape(r, MLA_HEADS * LANES).astype(BF16),
            swap.reshape(r, MLA_HEADS * LANES).astype(BF16))


def _mla_kv_weights(w_ukv):
    r = w_ukv.shape[0]
    w = w_ukv.reshape(r, MLA_HEADS, MLA_NOPE + MLA_V)
    k_nope = jnp.concatenate([w[..., :MLA_NOPE], jnp.zeros((r, MLA_HEADS, LANES - MLA_NOPE), w.dtype)],
                             axis=-1)
    return (k_nope.reshape(r, MLA_HEADS * LANES).astype(BF16),
            w[..., MLA_NOPE:].reshape(r, MLA_HEADS * MLA_V).astype(BF16))


def _qk_gains(g):
    half = MLA_ROPE // 2
    z = lambda n: jnp.zeros((n,), g.dtype)
    main = jnp.concatenate([g, z(LANES - MLA_QK)])
    swap = jnp.concatenate([z(MLA_NOPE), g[MLA_NOPE + half:], g[MLA_NOPE:MLA_NOPE + half],
                            z(LANES - MLA_QK)])
    return main[None, :], swap[None, :]


def _rope_tables(lp):
    half = MLA_ROPE // 2
    inv = ROPE_BASE ** (-jnp.arange(half, dtype=F32) / half)
    ang = jnp.arange(lp, dtype=jnp.int32).astype(F32)[:, None] * inv[None, :]
    cos, sin = jnp.cos(ang), jnp.sin(ang)
    ones = jnp.ones((lp, MLA_NOPE), F32)
    pad1 = jnp.ones((lp, LANES - MLA_QK), F32)
    zeros = jnp.zeros((lp, MLA_NOPE), F32)
    pad0 = jnp.zeros((lp, LANES - MLA_QK), F32)
    return (jnp.concatenate([ones, cos, cos, pad1], axis=1),
            jnp.concatenate([zeros, -sin, sin, pad0], axis=1))


def _rel_buckets(n):
    max_exact = REL_BUCKETS // 2
    d = np.arange(n)
    df = np.maximum(d, 1).astype(np.float32)
    large = max_exact + (np.log(df / np.float32(max_exact))
                         / np.float32(math.log(REL_MAX_DIST / max_exact))
                         * np.float32(REL_BUCKETS - max_exact)).astype(np.int32)
    large = np.minimum(large, REL_BUCKETS - 1)
    return np.where(d < max_exact, d, large)


def _bias_tiles(rel_bias):
    buckets = _rel_buckets(2 * LANES)
    assert np.all(buckets[LANES - 1:] == REL_BUCKETS - 1)
    n_heads = rel_bias.shape[1]
    shifted = (rel_bias - rel_bias[REL_BUCKETS - 1:REL_BUCKETS, :]) * LOG2E
    period = 3 * LANES
    w = jnp.concatenate([jnp.take(shifted, jnp.asarray(buckets), axis=0),
                         jnp.zeros((period - 2 * LANES, n_heads), F32)], axis=0)
    u = jnp.roll(jnp.flip(w, axis=0), -(2 * LANES - 1), axis=0).T
    flat = jnp.tile(u, (1, LANES))[:, :LANES * (period - 1)]
    return flat.reshape(n_heads, LANES, period - 1)[:, :, :2 * LANES].astype(F32)


def kernel(x, meta_tokens, rel_bias, ev_norm_mix, ev_w_mix_in, ev_g_q_lat, ev_g_kv_lat, ev_w_uq,
           ev_w_ukv, ev_mla_q_norm, ev_mla_k_norm, ev_dsa_q_norm, ev_dsa_k_norm, ev_w_mix_out,
           ev_norm_ffn, ev_w1, ev_w3, ev_w2, od_norm_mix, od_w_in, od_conv_w, od_w_out,
           od_norm_ffn, od_w_router, od_w1, od_w3, od_w2):
    b, seq, d = x.shape
    assert d == D_MODEL
    l_tot = seq + N_META
    lp = -(-l_tot // BLOCK_Q) * BLOCK_Q
    assert lp % SEQ_TILE == 0, "sequence tiling assumes the padded length is a multiple of 384"
    top_k = min(DSA_TOPK_MAX, l_tot // 4)
    depth = ev_norm_mix.shape[0] + od_norm_mix.shape[0]

    meta = jnp.broadcast_to(meta_tokens[None].astype(x.dtype), (b, N_META, d))
    h = jnp.concatenate([meta, x, jnp.zeros((b, lp - l_tot, d), x.dtype)], axis=1)
    h = h.reshape(b * lp, d)

    cos_t, sin_t = _rope_tables(lp)
    tz = _bias_tiles(rel_bias)
    row2 = lambda v: v[None, :]

    for layer in range(depth):
        i = layer // 2
        if layer % 2 == 0:
            wqm, wqs = _mla_q_weights(ev_w_uq[i])
            wkk, wkv = _mla_kv_weights(ev_w_ukv[i])
            gqm, gqs = _qk_gains(ev_mla_q_norm[i])
            gkm, gks = _qk_gains(ev_mla_k_norm[i])
            gdq = row2(jnp.concatenate([ev_dsa_q_norm[i]] * 2))
            gdk = row2(jnp.concatenate([ev_dsa_k_norm[i]] * 2))
            qm, km, vm, qd, kd, vd, iq, ik, iw = _prep_call(
                h, row2(ev_norm_mix[i]), _mix_in_weights(ev_w_mix_in[i]), row2(ev_g_q_lat[i]),
                row2(ev_g_kv_lat[i]), wqm, wqs, wkk, wkv, gqm, gqs, gkm, gks, gdq, gdk,
                cos_t, sin_t, lp)
            seq3 = lambda a: a.reshape(b, lp, a.shape[1])
            o_mla = _mla_call(seq3(qm), seq3(km), seq3(vm))
            o_dsa = _dsa_call(seq3(iq), seq3(iw), seq3(ik), seq3(qd), seq3(kd), seq3(vd), tz, top_k)
            w_o = ev_w_mix_out[i].astype(BF16)
            n_mla = MLA_HEADS * MLA_V
            h = _mix_out_ffn_call(h, o_mla.reshape(b * lp, -1), o_dsa.reshape(b * lp, -1),
                                  w_o[:n_mla], w_o[n_mla:], row2(ev_norm_ffn[i]),
                                  ev_w1[i], ev_w3[i], ev_w2[i])
        else:
            h = _conv_call(h.reshape(b, lp, d), row2(od_norm_mix[i]), od_w_in[i].astype(BF16),
                           od_conv_w[i].reshape(CONV_WIDTH, d), od_w_out[i].astype(BF16))
            h = h.reshape(b * lp, d)
            wr = jnp.concatenate(
                [od_w_router[i], jnp.zeros((d, LANES - N_EXPERTS), od_w_router.dtype)], axis=1)
            rows = (b, lp, N_META, seq) if layer == depth - 1 and seq % MOE_TILE == 0 else None
            h = _moe_call(h, row2(od_norm_ffn[i]), wr.astype(BF16), od_w1[i], od_w3[i], od_w2[i],
                          rows)
            if rows is not None:
                return h.reshape(b, seq, d)
    return h.reshape(b, lp, d)[:, N_META:l_tot]
```

```python
import functools
import math

import numpy as np
import jax
import jax.numpy as jnp
from jax import lax
from jax.experimental import pallas as pl
from jax.experimental.pallas import tpu as pltpu

F32 = jnp.float32
BF16 = jnp.bfloat16
I32 = jnp.int32
I16 = jnp.int16

D_MODEL = 1024
N_META = 16
BLOCK_Q = 128
EPS = 1e-6
MLA_HEADS = 8
MLA_Q_RANK = 384
MLA_KV_RANK = 256
MLA_NOPE = 64
MLA_ROPE = 32
MLA_V = 64
MLA_QK = MLA_NOPE + MLA_ROPE
MLA_SCALE = MLA_QK ** -0.5
ROPE_BASE = 10000.0
DSA_HEADS = 8
DSA_HEAD_DIM = 64
DSA_WIDTH = DSA_HEADS * DSA_HEAD_DIM
DSA_SCALE = DSA_HEAD_DIM ** -0.5
IDX_HEADS = 8
IDX_DIM = 32
DSA_TOPK_MAX = 256
REL_BUCKETS = 32
REL_MAX_DIST = 128
MIX_IN_SIZES = (MLA_Q_RANK, MLA_KV_RANK, MLA_ROPE, DSA_WIDTH, DSA_WIDTH, DSA_WIDTH,
                IDX_HEADS * IDX_DIM, IDX_DIM, IDX_HEADS)
CONV_WIDTH = 3
N_EXPERTS = 8
TOP_K = 2

LANES = 128
VMEM_LIMIT_BYTES = 56 * 1024 * 1024

SEQ_TILE = 3 * LANES
PREP_TILE = 2 * SEQ_TILE
MOE_TILE = 512
EXPERT_TILE = 2 * MOE_TILE
QK_LOOKAHEAD = 2
HALF_BITS = 16
BITS_PER_EXIT_TEST = 8
LOG2E = math.log2(math.e)
NEG_BIG = -1e30
INT_MIN = -2 ** 31

_C_CQ = 0
_C_CKV = _C_CQ + MLA_Q_RANK
_C_KRM = _C_CKV + MLA_KV_RANK
_C_KRS = _C_KRM + LANES
_C_DQ = _C_KRS + LANES
_C_DK = _C_DQ + DSA_WIDTH
_C_DV = _C_DK + DSA_WIDTH
_C_IQ = _C_DV + DSA_WIDTH
_C_IK = _C_IQ + IDX_HEADS * IDX_DIM
_C_IW = _C_IK + LANES
_C_END = _C_IW + LANES


def _cparams(sem):
    return pltpu.CompilerParams(dimension_semantics=sem, vmem_limit_bytes=VMEM_LIMIT_BYTES)


def _row_tile(n_rows, candidates=(1024, 768, 512, 384, 256, 128)):
    for c in candidates:
        if n_rows % c == 0:
            return c
    raise ValueError(f"no row tile for {n_rows}")


def _rms(x, g):
    ms = jnp.mean(x * x, axis=-1, keepdims=True)
    return x * lax.rsqrt(ms + EPS) * g


def _dot(a, b):
    return jnp.dot(a, b, preferred_element_type=F32)


def _dot_nt(a, b):
    return lax.dot_general(a, b, (((1,), (1,)), ((), ())), preferred_element_type=F32)


def _prep_kernel(h_ref, g_ref, wext_ref, gql_ref, gkvl_ref, wqm_ref, wqs_ref, wkk_ref, wkv_ref,
                 gqm_ref, gqs_ref, gkm_ref, gks_ref, gdq_ref, gdk_ref, cos_ref, sin_ref,
                 qm_o, km_o, vm_o, qd_o, kd_o, vd_o, iq_o, ik_o, iw_o):
    xn = _rms(h_ref[...], g_ref[...]).astype(BF16)

    def proj(lo, hi):
        return _dot(xn, wext_ref[:, lo:hi])

    cos = cos_ref[...]
    sin = sin_ref[...]
    lane = lax.broadcasted_iota(I32, (xn.shape[0], LANES), 1)

    cqn = _rms(proj(_C_CQ, _C_CKV), gql_ref[...]).astype(BF16)
    q_main = _dot(cqn, wqm_ref[...])
    q_swap = _dot(cqn, wqs_ref[...])
    for hd in range(MLA_HEADS):
        sl = slice(hd * LANES, (hd + 1) * LANES)
        a = q_main[:, sl]
        r = lax.rsqrt(jnp.sum(a * a, axis=-1, keepdims=True) * (1.0 / MLA_QK) + EPS)
        out = (a * r * gqm_ref[...]) * cos + (q_swap[:, sl] * r * gqs_ref[...]) * sin
        qm_o[:, sl] = (out * (MLA_SCALE * LOG2E)).astype(BF16)

    ckvn = _rms(proj(_C_CKV, _C_KRM), gkvl_ref[...]).astype(BF16)
    k_nope = _dot(ckvn, wkk_ref[...])
    vm_o[...] = _dot(ckvn, wkv_ref[...]).astype(BF16)
    kr_main = proj(_C_KRM, _C_KRS)
    kr_swap = proj(_C_KRS, _C_DQ)
    for hd in range(MLA_HEADS):
        sl = slice(hd * LANES, (hd + 1) * LANES)
        a = k_nope[:, sl] + kr_main
        r = lax.rsqrt(jnp.sum(a * a, axis=-1, keepdims=True) * (1.0 / MLA_QK) + EPS)
        out = (a * r * gkm_ref[...]) * cos + (kr_swap * r * gks_ref[...]) * sin
        km_o[:, sl] = out.astype(BF16)

    first = lane < DSA_HEAD_DIM
    for (lo, g2_ref, o_ref, post) in ((_C_DQ, gdq_ref, qd_o, DSA_SCALE * LOG2E),
                                      (_C_DK, gdk_ref, kd_o, None)):
        for pr in range(DSA_HEADS // 2):
            x = proj(lo + pr * LANES, lo + (pr + 1) * LANES)
            sq = x * x
            s0 = jnp.sum(jnp.where(first, sq, 0.0), axis=-1, keepdims=True)
            s1 = jnp.sum(jnp.where(first, 0.0, sq), axis=-1, keepdims=True)
            r0 = lax.rsqrt(s0 * (1.0 / DSA_HEAD_DIM) + EPS)
            r1 = lax.rsqrt(s1 * (1.0 / DSA_HEAD_DIM) + EPS)
            out = x * jnp.where(first, r0, r1) * g2_ref[...]
            if post is None:
                o_ref[:, pr * LANES:(pr + 1) * LANES] = out.astype(BF16)
            else:
                out = out * post
                o_ref[:, (2 * pr) * LANES:(2 * pr + 1) * LANES] = jnp.where(first, out, 0.0).astype(BF16)
                o_ref[:, (2 * pr + 1) * LANES:(2 * pr + 2) * LANES] = jnp.where(first, 0.0, out).astype(BF16)

    vd_o[...] = proj(_C_DV, _C_IQ).astype(BF16)
    lane_group = lax.shift_right_logical(lane, int(math.log2(IDX_DIM)))
    for quad in range(IDX_HEADS * IDX_DIM // LANES):
        x = proj(_C_IQ + quad * LANES, _C_IQ + (quad + 1) * LANES)
        for j in range(LANES // IDX_DIM):
            hd = quad * (LANES // IDX_DIM) + j
            iq_o[:, hd * LANES:(hd + 1) * LANES] = jnp.where(lane_group == j, x, 0.0).astype(BF16)
    ik_o[...] = proj(_C_IK, _C_IW).astype(BF16)
    iw_o[...] = proj(_C_IW, _C_END)


def _prep_call(h, g_mix, wext, gql, gkvl, wqm, wqs, wkk, wkv, gqm, gqs, gkm, gks, gdq, gdk,
               cos_t, sin_t, lp):
    t = h.shape[0]
    tm = PREP_TILE
    assert t % tm == 0
    reps = tm // math.gcd(lp, tm)
    nt = reps * lp // tm
    cos_t, sin_t = jnp.tile(cos_t, (reps, 1)), jnp.tile(sin_t, (reps, 1))
    row = lambda w: pl.BlockSpec((tm, w), lambda i: (i, 0))
    full = lambda a: pl.BlockSpec(a.shape, lambda i: (0, 0), pipeline_mode=pl.Buffered(1))
    tab = pl.BlockSpec((tm, LANES), lambda i: (i % nt, 0))
    hw = MLA_HEADS * LANES
    out_shape = [
        jax.ShapeDtypeStruct((t, hw), BF16), jax.ShapeDtypeStruct((t, hw), BF16),
        jax.ShapeDtypeStruct((t, MLA_HEADS * MLA_V), BF16),
        jax.ShapeDtypeStruct((t, DSA_HEADS * LANES), BF16), jax.ShapeDtypeStruct((t, DSA_WIDTH), BF16),
        jax.ShapeDtypeStruct((t, DSA_WIDTH), BF16),
        jax.ShapeDtypeStruct((t, IDX_HEADS * LANES), BF16),
        jax.ShapeDtypeStruct((t, LANES), BF16), jax.ShapeDtypeStruct((t, LANES), F32),
    ]
    return pl.pallas_call(
        _prep_kernel,
        grid=(t // tm,),
        in_specs=[row(D_MODEL), full(g_mix), full(wext), full(gql), full(gkvl), full(wqm),
                  full(wqs), full(wkk), full(wkv), full(gqm), full(gqs), full(gkm), full(gks),
                  full(gdq), full(gdk), tab, tab],
        out_specs=[row(s.shape[1]) for s in out_shape],
        out_shape=out_shape,
        compiler_params=_cparams(("parallel",)),
        name="prep_mix_in",
    )(h, g_mix, wext, gql, gkvl, wqm, wqs, wkk, wkv, gqm, gqs, gkm, gks, gdq, gdk, cos_t, sin_t)


def _flash_init(m_scr, l_scr, acc_scr):
    m_scr[...] = jnp.full(m_scr.shape, -jnp.inf, F32)
    l_scr[...] = jnp.zeros(l_scr.shape, F32)
    acc_scr[...] = jnp.zeros(acc_scr.shape, F32)


def _lane_fold(x, op):
    return functools.reduce(op, [x[:, j * LANES:(j + 1) * LANES] for j in range(x.shape[1] // LANES)])


def _flash_update(hd, s, vc, m_scr, l_scr, acc_scr):
    m_old = m_scr[hd]
    m_new = jnp.maximum(m_old, jnp.max(_lane_fold(s, jnp.maximum), axis=-1, keepdims=True))
    alpha = jnp.exp2(m_old - m_new)
    p = jnp.exp2(s - jnp.concatenate([m_new] * (s.shape[1] // LANES), axis=1))
    m_scr[hd] = m_new
    l_scr[hd] = alpha * l_scr[hd] + _lane_fold(p, jnp.add)
    acc_scr[hd] = alpha * acc_scr[hd] + _dot(p.astype(BF16), vc)


def _flash_store(o_ref, l_scr, acc_scr, head_dim):
    lane = lax.broadcasted_iota(I32, acc_scr.shape[1:], 1)
    for pr in range(acc_scr.shape[0] // 2):
        o0, o1 = [acc_scr[hd] / jnp.sum(l_scr[hd], axis=-1, keepdims=True)
                  for hd in (2 * pr, 2 * pr + 1)]
        o_ref[:, pr * LANES:(pr + 1) * LANES] = jnp.where(lane < head_dim, o0, o1).astype(o_ref.dtype)


def _mla_kernel(q_ref, k_ref, v_ref, o_ref, m_scr, l_scr, acc_scr):
    qi = pl.program_id(1)
    tq = q_ref.shape[0]
    _flash_init(m_scr, l_scr, acc_scr)

    def chunk(c, diagonal):
        ks = pl.ds(pl.multiple_of(c * tq, tq), tq)

        def scores(hd):
            sl = slice(hd * LANES, (hd + 1) * LANES)
            return _dot_nt(q_ref[:, sl], k_ref[ks, sl])

        pending = [scores(hd) for hd in range(QK_LOOKAHEAD)]
        for hd in range(MLA_HEADS):
            if hd + QK_LOOKAHEAD < MLA_HEADS:
                pending.append(scores(hd + QK_LOOKAHEAD))
            s = pending.pop(0)
            if diagonal:
                row = lax.broadcasted_iota(I32, (tq, tq), 0)
                col = lax.broadcasted_iota(I32, (tq, tq), 1)
                s = jnp.where(col <= row, s, NEG_BIG)
            pr = hd // 2
            _flash_update(hd, s, v_ref[ks, pr * LANES:(pr + 1) * LANES], m_scr, l_scr, acc_scr)

    def body(c, carry):
        chunk(c, False)
        return carry

    lax.fori_loop(0, qi, body, 0)
    chunk(qi, True)
    _flash_store(o_ref, l_scr, acc_scr, MLA_V)


def _mla_call(q, k, v):
    b, lp, _ = q.shape
    tq = SEQ_TILE
    nq = lp // tq
    state = pltpu.VMEM((MLA_HEADS, tq, LANES), F32)
    return pl.pallas_call(
        _mla_kernel,
        grid=(b, nq),
        in_specs=[pl.BlockSpec((None, tq, q.shape[2]), lambda bi, i: (bi, i, 0)),
                  pl.BlockSpec((None, lp, k.shape[2]), lambda bi, i: (bi, 0, 0)),
                  pl.BlockSpec((None, lp, v.shape[2]), lambda bi, i: (bi, 0, 0))],
        out_specs=pl.BlockSpec((None, tq, v.shape[2]), lambda bi, i: (bi, i, 0)),
        out_shape=jax.ShapeDtypeStruct((b, lp, v.shape[2]), BF16),
        scratch_shapes=[state, state, state],
        compiler_params=_cparams(("parallel", "arbitrary")),
        name="mla_attention",
    )(q, k, v)


def _dsa_kernel(iq_ref, iw_ref, ik_ref, q_ref, k_ref, v_ref, tz_ref, o_ref,
                key_scr, cap_scr, hi_scr, lo_scr, wt_scr, ans_scr, jst_scr, m_scr, l_scr, acc_scr,
                *, top_k):
    qi = pl.program_id(1)
    tq = q_ref.shape[0]
    nsub = tq // LANES
    sublanes = 8
    n_chunks = qi + 1
    key_i = lax.broadcasted_iota(I32, (tq, tq), 0)
    qry_i = lax.broadcasted_iota(I32, (tq, tq), 1)
    chunk_rows = lambda c: pl.ds(pl.multiple_of(c * tq, tq), tq)
    head_slot = lambda hd: slice(hd * LANES, (hd + 1) * LANES)
    wt_scr[...] = jnp.transpose(iw_ref[...])[0:IDX_HEADS, :]

    def index_chunk(c, diagonal):
        ikc = ik_ref[chunk_rows(c), :]
        sc = jnp.zeros((tq, tq), F32)
        for hd in range(IDX_HEADS):
            act = jnp.maximum(_dot_nt(ikc, iq_ref[:, head_slot(hd)]), 0.0)
            sc = sc + wt_scr[hd:hd + 1, :] * act
        sc = jnp.where(sc == 0.0, 0.0, sc)
        if diagonal:
            sc = jnp.where(key_i <= qry_i, sc, -jnp.inf)
        bits = pltpu.bitcast(sc, I32)
        key = bits ^ (lax.shift_right_arithmetic(bits, 31) & 0x7FFFFFFF)
        key_scr[c] = key
        hi_scr[c] = lax.shift_right_arithmetic(key, HALF_BITS).astype(I16)

    def index_body(c, carry):
        index_chunk(c, False)
        return carry

    lax.fori_loop(0, qi, index_body, 0)
    index_chunk(qi, True)

    kf = float(top_k)
    vec = (sublanes, tq)
    sub_i = lax.broadcasted_iota(I32, vec, 0)

    def count(pred):
        def body(c, accs):
            accs = list(accs)
            kk = key_scr[c]
            for g in range(tq // sublanes):
                hit = jnp.where(pred(kk[g * sublanes:(g + 1) * sublanes, :], c * tq + g * sublanes),
                                1.0, 0.0)
                accs[g % 2] = accs[g % 2] + hit
            return tuple(accs)
        zero = jnp.zeros(vec, F32)
        a0, a1 = lax.fori_loop(0, n_chunks, body, (zero, zero))
        return jnp.broadcast_to(jnp.sum(a0 + a1, axis=0, keepdims=True), vec)

    packed = 16

    def count_half(half_scr, cand):
        c16 = jnp.broadcast_to(cand[0:1, :], (packed, tq)).astype(I16)

        def body(c, accs):
            accs = list(accs)
            for g in range(tq // packed):
                kk = half_scr[c, g * packed:(g + 1) * packed, :]
                accs[g % 2] = accs[g % 2] + jnp.where(kk >= c16, jnp.int16(1), jnp.int16(0))
            return tuple(accs)
        zero = jnp.zeros((packed, tq), I16)
        a0, a1 = lax.fori_loop(0, n_chunks, body, (zero, zero))
        tot = a0.astype(I32) + a1.astype(I32)
        return jnp.broadcast_to(jnp.sum(tot, axis=0, keepdims=True).astype(F32), vec)

    def n_open_of(done):
        return jnp.sum(jnp.where(done == 0, 1.0, 0.0))

    def bisect(count_ge, ans, done):
        def step(bit, ans, done):
            cand = ans + lax.shift_left(jnp.int32(1), bit)
            cnt = count_ge(cand)
            open_ = done == 0
            ans = jnp.where(jnp.logical_and(open_, cnt >= kf), cand, ans)
            done = jnp.where(jnp.logical_and(open_, cnt == kf), 1, done)
            return ans, done

        def cond(st):
            bit, _, _, n_open = st
            return jnp.logical_and(bit >= 0, n_open > 0.0)

        def body(st):
            bit, ans, done, _ = st
            for k in range(BITS_PER_EXIT_TEST):
                ans, done = step(bit - k, ans, done)
            return bit - BITS_PER_EXIT_TEST, ans, done, n_open_of(done)

        _, ans, done, _ = lax.while_loop(cond, body,
                                         (jnp.int32(HALF_BITS - 1), ans, done, n_open_of(done)))
        return ans, done

    t_pos = qi * tq + lax.broadcasted_iota(I32, vec, 1)
    done0 = (t_pos + 1 <= top_k).astype(I32)
    half_min = -(1 << (HALF_BITS - 1))

    hi_ans, done = bisect(lambda cand: count_half(hi_scr, cand), jnp.full(vec, half_min, I32), done0)

    hi_row = hi_ans[0:1, :]

    def low_chunk(c, carry):
        key = key_scr[c]
        low = (key & ((1 << HALF_BITS) - 1)) + half_min
        inside = lax.shift_right_arithmetic(key, HALF_BITS) == hi_row
        lo_scr[c] = jnp.where(inside, low, half_min).astype(I16)
        return carry

    lax.fori_loop(0, n_chunks, low_chunk, 0)
    above = count_half(hi_scr, hi_ans + 1)
    lo_ans, done = bisect(lambda cand: above + count_half(lo_scr, cand + half_min),
                          jnp.zeros(vec, I32), done)
    ans = lax.shift_left(hi_ans, HALF_BITS) + lo_ans
    n_open = n_open_of(done)
    ans_scr[...] = ans
    jst_scr[...] = jnp.full(vec, 2 ** 31 - 1, I32)

    @pl.when(n_open > 0.0)
    def _():
        need = kf - count(lambda kk, _i: kk > ans)

        def tie_body(i, jst):
            cand = jst + lax.shift_left(jnp.int32(1), 13 - i)
            cnt = count(lambda kk, i0: jnp.logical_and(kk == ans, sub_i + i0 < cand))
            return jnp.where(cnt < need, cand, jst)

        jst = lax.fori_loop(0, 14, tie_body, jnp.zeros(vec, I32))
        jst_scr[...] = jnp.where(done == 0, jst, 2 ** 31 - 1)

    ans_row = ans_scr[0:1, :]
    jst_row = jst_scr[0:1, :]

    def mask_chunk(c, diagonal):
        kk = key_scr[c]
        sel = jnp.logical_or(kk > ans_row,
                             jnp.logical_and(kk == ans_row, key_i + c * tq <= jst_row))
        if diagonal:
            sel = jnp.logical_and(sel, key_i <= qry_i)
        cap_scr[c] = jnp.transpose(jnp.where(sel, -NEG_BIG, NEG_BIG))

    def mask_body(c, carry):
        mask_chunk(c, False)
        return carry

    lax.fori_loop(0, qi, mask_body, 0)
    mask_chunk(qi, True)

    _flash_init(m_scr, l_scr, acc_scr)

    def bias_of(hd, where_):
        near = tz_ref[hd, :, LANES:2 * LANES]
        far = tz_ref[hd, :, 0:LANES]
        z = jnp.zeros((LANES, LANES), F32)
        if where_ == "previous":
            blocks = [[far if (a == 0 and b == nsub - 1) else z for b in range(nsub)]
                      for a in range(nsub)]
        else:
            blocks = [[near if b == a else far if b == a - 1 else z for b in range(nsub)]
                      for a in range(nsub)]
        return jnp.concatenate([jnp.concatenate(r, axis=1) for r in blocks], axis=0)

    def chunk(c, where_):
        ks = chunk_rows(c)
        cap = cap_scr[c]
        pair = lambda hd: slice((hd // 2) * LANES, (hd // 2 + 1) * LANES)
        scores = lambda hd: _dot_nt(q_ref[:, head_slot(hd)], k_ref[ks, pair(hd)])

        pending = [scores(hd) for hd in range(QK_LOOKAHEAD)]
        for hd in range(DSA_HEADS):
            if hd + QK_LOOKAHEAD < DSA_HEADS:
                pending.append(scores(hd + QK_LOOKAHEAD))
            s = jnp.minimum(pending.pop(0), cap)
            if where_ is not None:
                s = s + bias_of(hd, where_)
            _flash_update(hd, s, v_ref[ks, pair(hd)], m_scr, l_scr, acc_scr)

    def body(c, carry):
        chunk(c, None)
        return carry

    lax.fori_loop(0, jnp.maximum(qi - 1, 0), body, 0)

    @pl.when(qi >= 1)
    def _():
        chunk(qi - 1, "previous")

    chunk(qi, "diagonal")
    _flash_store(o_ref, l_scr, acc_scr, DSA_HEAD_DIM)


def _dsa_call(iq, iw, ik, q, k, v, tz, top_k):
    b, lp, _ = q.shape
    tq = SEQ_TILE
    nq = lp // tq
    qspec = lambda w: pl.BlockSpec((None, tq, w), lambda bi, i: (bi, i, 0))
    kspec = lambda w: pl.BlockSpec((None, lp, w), lambda bi, i: (bi, 0, 0))
    state = pltpu.VMEM((DSA_HEADS, tq, LANES), F32)
    return pl.pallas_call(
        functools.partial(_dsa_kernel, top_k=top_k),
        grid=(b, nq),
        in_specs=[qspec(iq.shape[2]), qspec(LANES), kspec(LANES), qspec(q.shape[2]),
                  kspec(DSA_WIDTH), kspec(DSA_WIDTH),
                  pl.BlockSpec(tz.shape, lambda bi, i: (0, 0, 0))],
        out_specs=qspec(DSA_WIDTH),
        out_shape=jax.ShapeDtypeStruct((b, lp, DSA_WIDTH), BF16),
        scratch_shapes=[pltpu.VMEM((nq, tq, tq), I32), pltpu.VMEM((nq, tq, tq), F32),
                        pltpu.VMEM((nq, tq, tq), I16), pltpu.VMEM((nq, tq, tq), I16),
                        pltpu.VMEM((IDX_HEADS, tq), F32),
                        pltpu.VMEM((8, tq), I32), pltpu.VMEM((8, tq), I32),
                        state, state, state],
        compiler_params=_cparams(("parallel", "arbitrary")),
        name="dsa_attention",
    )(iq, iw, ik, q, k, v, tz)


def _swiglu_chunk(xb, w1_ref, w3_ref, w2_ref):
    a = _dot(xb, w1_ref[...].astype(BF16))
    act = (a * jax.nn.sigmoid(a)) * _dot(xb, w3_ref[...].astype(BF16))
    return _dot(act.astype(BF16), w2_ref[...].astype(BF16))


def _mix_out_ffn_kernel(h_ref, a_ref, b_ref, wa_ref, wb_ref, g_ref, w1_ref, w3_ref, w2_ref, o_ref,
                        h1_scr, xn_scr, acc_scr):
    f = pl.program_id(1)

    @pl.when(f == 0)
    def _():
        h1 = h_ref[...] + _dot(a_ref[...], wa_ref[...]) + _dot(b_ref[...], wb_ref[...])
        h1_scr[...] = h1
        xn_scr[...] = _rms(h1, g_ref[...]).astype(BF16)
        acc_scr[...] = jnp.zeros(acc_scr.shape, F32)

    acc_scr[...] += _swiglu_chunk(xn_scr[...], w1_ref, w3_ref, w2_ref)

    @pl.when(f == pl.num_programs(1) - 1)
    def _():
        o_ref[...] = h1_scr[...] + acc_scr[...]


def _mix_out_ffn_call(h, a, b, wa, wb, g, w1, w3, w2):
    t = h.shape[0]
    d_ff = w1.shape[1]
    tm = _row_tile(t)
    tf = 512 if d_ff % 512 == 0 else d_ff
    row = lambda w: pl.BlockSpec((tm, w), lambda i, f: (i, 0))
    full = lambda x: pl.BlockSpec(x.shape, lambda i, f: (0, 0), pipeline_mode=pl.Buffered(1))
    return pl.pallas_call(
        _mix_out_ffn_kernel,
        grid=(t // tm, d_ff // tf),
        in_specs=[row(D_MODEL), row(a.shape[1]), row(b.shape[1]), full(wa), full(wb), full(g),
                  pl.BlockSpec((D_MODEL, tf), lambda i, f: (0, f)),
                  pl.BlockSpec((D_MODEL, tf), lambda i, f: (0, f)),
                  pl.BlockSpec((tf, D_MODEL), lambda i, f: (f, 0))],
        out_specs=row(D_MODEL),
        out_shape=jax.ShapeDtypeStruct(h.shape, F32),
        scratch_shapes=[pltpu.VMEM((tm, D_MODEL), F32), pltpu.VMEM((tm, D_MODEL), BF16),
                        pltpu.VMEM((tm, D_MODEL), F32)],
        compiler_params=_cparams(("parallel", "arbitrary")),
        name="mix_out_swiglu",
    )(h, a, b, wa, wb, g, w1, w3, w2)


def _conv_kernel(h_ref, g_ref, win_ref, cw_ref, wout_ref, o_ref, z_scr):
    i = pl.program_id(1)
    tm = h_ref.shape[0]
    halo = 8
    x = h_ref[...]
    xn = _rms(x, g_ref[...]).astype(BF16)
    c_gate = _dot(xn, win_ref[:, D_MODEL:2 * D_MODEL])
    u = _dot(xn, win_ref[:, 2 * D_MODEL:3 * D_MODEL])
    z = c_gate * u

    @pl.when(i == 0)
    def _():
        z_scr[0:halo, :] = jnp.zeros((halo, D_MODEL), F32)

    @pl.when(i > 0)
    def _():
        z_scr[0:halo, :] = z_scr[tm:tm + halo, :]

    z_scr[halo:halo + tm, :] = z
    cw = cw_ref[...]
    y = (cw[0:1, :] * z_scr[halo - 2:halo - 2 + tm, :]
         + cw[1:2, :] * z_scr[halo - 1:halo - 1 + tm, :]
         + cw[2:3, :] * z)
    b_gate = _dot(xn, win_ref[:, 0:D_MODEL])
    o_ref[...] = x + _dot((b_gate * y).astype(BF16), wout_ref[...])


def _conv_call(h3, g, w_in, cw, w_out):
    b, lp, d = h3.shape
    tm = SEQ_TILE
    full = lambda a: pl.BlockSpec(a.shape, lambda bi, i: (0,) * a.ndim)
    return pl.pallas_call(
        _conv_kernel,
        grid=(b, lp // tm),
        in_specs=[pl.BlockSpec((None, tm, d), lambda bi, i: (bi, i, 0)), full(g), full(w_in),
                  full(cw), full(w_out)],
        out_specs=pl.BlockSpec((None, tm, d), lambda bi, i: (bi, i, 0)),
        out_shape=jax.ShapeDtypeStruct(h3.shape, F32),
        scratch_shapes=[pltpu.VMEM((tm + 8, d), F32)],
        compiler_params=_cparams(("arbitrary", "arbitrary")),
        name="short_conv_mixer",
    )(h3, g, w_in, cw, w_out)


def _as_tiles(x):
    return x.reshape(x.shape[0], D_MODEL // LANES, LANES)


def _as_rows(x3):
    return x3.reshape(x3.shape[0], D_MODEL)


def _route_kernel(h_ref, g_ref, wr_ref, xg_ref, didx_ref, gate_ref, cnt_ref,
                  xs_scr, base_scr, dvm_scr, dsm_scr, sem_idx, sem_rows, *, stride):
    i = pl.program_id(0)
    tm = h_ref.shape[0]
    lane = lax.broadcasted_iota(I32, (tm, LANES), 1)
    lane_f = lane.astype(F32)

    @pl.when(i == 0)
    def _():
        base_scr[...] = jnp.zeros(base_scr.shape, F32)

    xnf = _rms(h_ref[...], g_ref[...])
    logits = jnp.where(lane < N_EXPERTS, _dot(xnf.astype(BF16), wr_ref[...]), -jnp.inf)
    v1 = jnp.max(logits, axis=-1, keepdims=True)
    i1 = jnp.min(jnp.where(logits == v1, lane_f, float(LANES)), axis=-1, keepdims=True)
    rest = jnp.where(lane_f == i1, -jnp.inf, logits)
    v2 = jnp.max(rest, axis=-1, keepdims=True)
    i2 = jnp.min(jnp.where(rest == v2, lane_f, float(LANES)), axis=-1, keepdims=True)
    e2 = jnp.exp(v2 - v1)
    den = 1.0 + e2
    gates = (1.0 / den, e2 / den)

    oh1 = jnp.where(lane_f == i1, 1.0, 0.0)
    oh2 = jnp.where(lane_f == i2, 1.0, 0.0)
    oh = oh1 + oh2
    earlier = (lax.broadcasted_iota(I32, (tm, tm), 1) < lax.broadcasted_iota(I32, (tm, tm), 0))
    prefix = _dot(jnp.where(earlier, 1.0, 0.0).astype(BF16), oh.astype(BF16))
    pos = base_scr[0:1, :] + prefix
    d1 = jnp.sum(oh1 * pos, axis=-1, keepdims=True) + i1 * float(stride)
    d2 = jnp.sum(oh2 * pos, axis=-1, keepdims=True) + i2 * float(stride)
    base_scr[...] = base_scr[...] + jnp.sum(oh, axis=0, keepdims=True)
    cnt_ref[...] = base_scr[...]

    gate_ref[...] = jnp.where(lane == 0, gates[0], jnp.where(lane == 1, gates[1], 0.0))
    slot = lax.rem(i, 2)
    xs_scr[slot] = _as_tiles(xnf)

    dmat = jnp.where(lane == 0, d1, jnp.where(lane == 1, d2, 0.0))
    dvm_scr[...] = jnp.transpose(dmat)[0:8, :].astype(I32)
    didx_ref[...] = dvm_scr[...]
    cp = pltpu.make_async_copy(dvm_scr, dsm_scr, sem_idx)
    cp.start()
    cp.wait()

    for r in range(tm):
        for s in range(TOP_K):
            pltpu.make_async_copy(xs_scr.at[slot, r], xg_ref.at[dsm_scr[s, r]],
                                  sem_rows.at[slot]).start(priority=s)

    def wait_rows(sl):
        for s in range(TOP_K):
            pltpu.make_async_copy(xs_scr.at[sl], xg_ref.at[pl.ds(0, tm)], sem_rows.at[sl]).wait()

    @pl.when(i > 0)
    def _():
        wait_rows(1 - slot)

    @pl.when(i == pl.num_programs(0) - 1)
    def _():
        wait_rows(slot)
        xs_scr[0] = jnp.zeros(xs_scr.shape[1:], F32)
        dvm_scr[:, 0:LANES] = base_scr[...].astype(I32)
        cp2 = pltpu.make_async_copy(dvm_scr, dsm_scr, sem_idx)
        cp2.start()
        cp2.wait()
        for phase in ("start", "wait"):
            for e in range(N_EXPERTS):
                for k in range(EXPERT_TILE // tm):
                    blk = pltpu.make_async_copy(
                        xs_scr.at[0], xg_ref.at[pl.ds(e * stride + dsm_scr[0, e] + k * tm, tm)],
                        sem_rows.at[0])
                    blk.start() if phase == "start" else blk.wait()


def _route_call(h, g, wr, stride):
    t = h.shape[0]
    tm = MOE_TILE
    return pl.pallas_call(
        functools.partial(_route_kernel, stride=stride),
        grid=(t // tm,),
        in_specs=[pl.BlockSpec((tm, D_MODEL), lambda i: (i, 0)),
                  pl.BlockSpec(g.shape, lambda i: (0, 0)),
                  pl.BlockSpec(wr.shape, lambda i: (0, 0))],
        out_specs=[pl.BlockSpec(memory_space=pl.ANY),
                   pl.BlockSpec((8, tm), lambda i: (0, i)),
                   pl.BlockSpec((tm, LANES), lambda i: (i, 0)),
                   pl.BlockSpec((8, LANES), lambda i: (0, 0))],
        out_shape=[jax.ShapeDtypeStruct((N_EXPERTS * stride, D_MODEL // LANES, LANES), F32),
                   jax.ShapeDtypeStruct((8, t), I32),
                   jax.ShapeDtypeStruct((t, LANES), F32),
                   jax.ShapeDtypeStruct((8, LANES), F32)],
        scratch_shapes=[pltpu.VMEM((2, tm, D_MODEL // LANES, LANES), F32), pltpu.VMEM((8, LANES), F32),
                        pltpu.VMEM((8, tm), I32), pltpu.SMEM((8, tm), I32),
                        pltpu.SemaphoreType.DMA, pltpu.SemaphoreType.DMA((2,))],
        compiler_params=_cparams(("arbitrary",)),
        name="moe_route",
    )(h, g, wr)


def _expert_kernel(blk_ref, exp_ref, nu_ref, x_ref, w1_ref, w3_ref, w2_ref, o_ref, xb_scr, acc_scr):
    j = pl.program_id(0)
    f = pl.program_id(1)

    @pl.when(j < nu_ref[0])
    def _():
        @pl.when(f == 0)
        def _():
            xb_scr[...] = _as_rows(x_ref[...]).astype(BF16)
            acc_scr[...] = jnp.zeros(acc_scr.shape, F32)

        acc_scr[...] += _swiglu_chunk(xb_scr[...], w1_ref, w3_ref, w2_ref)

        @pl.when(f == pl.num_programs(1) - 1)
        def _():
            o_ref[...] = _as_tiles(acc_scr[...])


def _expert_call(tile_blk, tile_exp, n_used, xg, w1, w3, w2):
    d_ff = w1.shape[2]
    tm = EXPERT_TILE
    tf = 512 if d_ff % 512 == 0 else d_ff
    nf = d_ff // tf
    n_tiles = tile_blk.shape[0]
    fe = lambda j, f, nu: jnp.where(j < nu[0], f, nf - 1)
    tile_block = pl.BlockSpec((tm, D_MODEL // LANES, LANES), lambda j, f, blk, ex, nu: (blk[j], 0, 0))
    grid_spec = pltpu.PrefetchScalarGridSpec(
        num_scalar_prefetch=3,
        grid=(n_tiles, nf),
        in_specs=[tile_block,
                  pl.BlockSpec((None, D_MODEL, tf), lambda j, f, blk, ex, nu: (ex[j], 0, fe(j, f, nu))),
                  pl.BlockSpec((None, D_MODEL, tf), lambda j, f, blk, ex, nu: (ex[j], 0, fe(j, f, nu))),
                  pl.BlockSpec((None, tf, D_MODEL), lambda j, f, blk, ex, nu: (ex[j], fe(j, f, nu), 0))],
        out_specs=tile_block,
        scratch_shapes=[pltpu.VMEM((tm, D_MODEL), BF16), pltpu.VMEM((tm, D_MODEL), F32)])
    return pl.pallas_call(
        _expert_kernel,
        grid_spec=grid_spec,
        out_shape=jax.ShapeDtypeStruct(xg.shape, F32),
        compiler_params=_cparams(("arbitrary", "arbitrary")),
        name="expert_swiglu",
    )(tile_blk, tile_exp, n_used, xg, w1, w3, w2)


def _combine_kernel(h_ref, gate_ref, didx_ref, yg_ref, o_ref, y_scr, dsm_scr, sem_idx, sem_rows):
    i = pl.program_id(0)
    tm = h_ref.shape[0]
    slot = lax.rem(i, 2)

    def fetch(j, sl):
        cp = pltpu.make_async_copy(didx_ref.at[:, pl.ds(pl.multiple_of(j * tm, tm), tm)], dsm_scr,
                                   sem_idx)
        cp.start()
        cp.wait()
        for r in range(tm):
            for s in range(TOP_K):
                pltpu.make_async_copy(yg_ref.at[dsm_scr[s, r]], y_scr.at[sl, s, r],
                                      sem_rows.at[sl]).start(priority=s)

    @pl.when(i == 0)
    def _():
        fetch(0, 0)

    @pl.when(i + 1 < pl.num_programs(0))
    def _():
        fetch(i + 1, 1 - slot)

    for s in range(TOP_K):
        pltpu.make_async_copy(yg_ref.at[pl.ds(0, tm)], y_scr.at[slot, s], sem_rows.at[slot]).wait()
    gate = gate_ref[...]
    o_ref[...] = (h_ref[...] + gate[:, 0:1] * _as_rows(y_scr[slot, 0])
                  + gate[:, 1:2] * _as_rows(y_scr[slot, 1]))


def _combine_call(h, gate, didx, yg, rows=None):
    t = h.shape[0]
    tm = MOE_TILE
    n_seq, seq_stride, first, n = rows if rows is not None else (1, 0, 0, t)
    assert n % tm == 0 and first % 8 == 0 and seq_stride % 8 == 0
    tiles = n // tm
    if rows is not None:
        didx = didx.reshape(8, n_seq, seq_stride)[:, :, first:first + n].reshape(8, n_seq * n)

    def token_rows(width):
        start = lambda i: pl.multiple_of((i // tiles) * seq_stride + first + (i % tiles) * tm, 8)
        return pl.BlockSpec((pl.Element(tm), pl.Element(width)), lambda i: (start(i), 0))

    return pl.pallas_call(
        _combine_kernel,
        grid=(n_seq * tiles,),
        in_specs=[token_rows(D_MODEL), token_rows(LANES),
                  pl.BlockSpec(memory_space=pl.ANY), pl.BlockSpec(memory_space=pl.ANY)],
        out_specs=pl.BlockSpec((tm, D_MODEL), lambda i: (i, 0)),
        out_shape=jax.ShapeDtypeStruct((n_seq * n, D_MODEL), F32),
        scratch_shapes=[pltpu.VMEM((2, TOP_K, tm, D_MODEL // LANES, LANES), F32),
                        pltpu.SMEM((8, tm), I32),
                        pltpu.SemaphoreType.DMA, pltpu.SemaphoreType.DMA((2,))],
        compiler_params=_cparams(("arbitrary",)),
        name="moe_combine",
    )(h, gate, didx, yg)


def _moe_call(h, g, wr, w1, w3, w2, rows=None):
    t = h.shape[0]
    tm = EXPERT_TILE
    assert t % MOE_TILE == 0 and tm % MOE_TILE == 0
    stride = -(-t // tm) * tm + tm
    xg, didx, gate, cnt = _route_call(h, g, wr, stride)
    counts = cnt[0, :N_EXPERTS].astype(I32)
    tiles_e = (counts + tm - 1) // tm
    cum = jnp.cumsum(tiles_e)
    n_used = cum[-1]
    n_tiles = -(-TOP_K * t // tm) + N_EXPERTS
    jj = jnp.minimum(jnp.arange(n_tiles, dtype=I32), n_used - 1)
    tile_exp = jnp.sum((jj[:, None] >= cum[None, :]).astype(I32), axis=1)
    tile_blk = tile_exp * (stride // tm) + jj - (cum - tiles_e)[tile_exp]
    yg = _expert_call(tile_blk, tile_exp, n_used[None], xg, w1, w3, w2)
    return _combine_call(h, gate, didx, yg, rows)


def _mix_in_weights(w):
    offs = np.concatenate([[0], np.cumsum(MIX_IN_SIZES)])
    cq, ckv, kr, dq, dk, dv, iq, ik, iw = [w[:, offs[j]:offs[j + 1]] for j in range(9)]
    z = lambda n: jnp.zeros((w.shape[0], n), w.dtype)
    half = MLA_ROPE // 2
    kr_main = jnp.concatenate([z(MLA_NOPE), kr, z(LANES - MLA_QK)], axis=1)
    kr_swap = jnp.concatenate([z(MLA_NOPE), kr[:, half:], kr[:, :half], z(LANES - MLA_QK)], axis=1)
    ik4 = jnp.concatenate([ik] * (LANES // IDX_DIM), axis=1)
    iw_p = jnp.concatenate([iw, z(LANES - IDX_HEADS)], axis=1)
    return jnp.concatenate([cq, ckv, kr_main, kr_swap, dq, dk, dv, iq, ik4, iw_p],
                           axis=1).astype(BF16)


def _mla_q_weights(w_uq):
    r = w_uq.shape[0]
    w = w_uq.reshape(r, MLA_HEADS, MLA_QK)
    nope, rope = w[..., :MLA_NOPE], w[..., MLA_NOPE:]
    half = MLA_ROPE // 2
    z = lambda n: jnp.zeros((r, MLA_HEADS, n), w.dtype)
    main = jnp.concatenate([nope, rope, z(LANES - MLA_QK)], axis=-1)
    swap = jnp.concatenate([z(MLA_NOPE), rope[..., half:], rope[..., :half], z(LANES - MLA_QK)],
                           axis=-1)
    return (main.reshape(r, MLA_HEADS * LANES).astype(BF16),
            swap.reshape(r, MLA_HEADS * LANES).astype(BF16))


def _mla_kv_weights(w_ukv):
    r = w_ukv.shape[0]
    w = w_ukv.reshape(r, MLA_HEADS, MLA_NOPE + MLA_V)
    k_nope = jnp.concatenate([w[..., :MLA_NOPE], jnp.zeros((r, MLA_HEADS, LANES - MLA_NOPE), w.dtype)],
                             axis=-1)
    return (k_nope.reshape(r, MLA_HEADS * LANES).astype(BF16),
            w[..., MLA_NOPE:].reshape(r, MLA_HEADS * MLA_V).astype(BF16))


def _qk_gains(g):
    half = MLA_ROPE // 2
    z = lambda n: jnp.zeros((n,), g.dtype)
    main = jnp.concatenate([g, z(LANES - MLA_QK)])
    swap = jnp.concatenate([z(MLA_NOPE), g[MLA_NOPE + half:], g[MLA_NOPE:MLA_NOPE + half],
                            z(LANES - MLA_QK)])
    return main[None, :], swap[None, :]


def _rope_tables(lp):
    half = MLA_ROPE // 2
    inv = ROPE_BASE ** (-jnp.arange(half, dtype=F32) / half)
    ang = jnp.arange(lp, dtype=jnp.int32).astype(F32)[:, None] * inv[None, :]
    cos, sin = jnp.cos(ang), jnp.sin(ang)
    ones = jnp.ones((lp, MLA_NOPE), F32)
    pad1 = jnp.ones((lp, LANES - MLA_QK), F32)
    zeros = jnp.zeros((lp, MLA_NOPE), F32)
    pad0 = jnp.zeros((lp, LANES - MLA_QK), F32)
    return (jnp.concatenate([ones, cos, cos, pad1], axis=1),
            jnp.concatenate([zeros, -sin, sin, pad0], axis=1))


def _rel_buckets(n):
    max_exact = REL_BUCKETS // 2
    d = np.arange(n)
    df = np.maximum(d, 1).astype(np.float32)
    large = max_exact + (np.log(df / np.float32(max_exact))
                         / np.float32(math.log(REL_MAX_DIST / max_exact))
                         * np.float32(REL_BUCKETS - max_exact)).astype(np.int32)
    large = np.minimum(large, REL_BUCKETS - 1)
    return np.where(d < max_exact, d, large)


def _bias_tiles(rel_bias):
    buckets = _rel_buckets(2 * LANES)
    assert np.all(buckets[LANES - 1:] == REL_BUCKETS - 1)
    n_heads = rel_bias.shape[1]
    shifted = (rel_bias - rel_bias[REL_BUCKETS - 1:REL_BUCKETS, :]) * LOG2E
    period = 3 * LANES
    w = jnp.concatenate([jnp.take(shifted, jnp.asarray(buckets), axis=0),
                         jnp.zeros((period - 2 * LANES, n_heads), F32)], axis=0)
    u = jnp.roll(jnp.flip(w, axis=0), -(2 * LANES - 1), axis=0).T
    flat = jnp.tile(u, (1, LANES))[:, :LANES * (period - 1)]
    return flat.reshape(n_heads, LANES, period - 1)[:, :, :2 * LANES].astype(F32)


def kernel(x, meta_tokens, rel_bias, ev_norm_mix, ev_w_mix_in, ev_g_q_lat, ev_g_kv_lat, ev_w_uq,
           ev_w_ukv, ev_mla_q_norm, ev_mla_k_norm, ev_dsa_q_norm, ev_dsa_k_norm, ev_w_mix_out,
           ev_norm_ffn, ev_w1, ev_w3, ev_w2, od_norm_mix, od_w_in, od_conv_w, od_w_out,
           od_norm_ffn, od_w_router, od_w1, od_w3, od_w2):
    b, seq, d = x.shape
    assert d == D_MODEL
    l_tot = seq + N_META
    lp = -(-l_tot // BLOCK_Q) * BLOCK_Q
    assert lp % SEQ_TILE == 0, "sequence tiling assumes the padded length is a multiple of 384"
    top_k = min(DSA_TOPK_MAX, l_tot // 4)
    depth = ev_norm_mix.shape[0] + od_norm_mix.shape[0]

    meta = jnp.broadcast_to(meta_tokens[None].astype(x.dtype), (b, N_META, d))
    h = jnp.concatenate([meta, x, jnp.zeros((b, lp - l_tot, d), x.dtype)], axis=1)
    h = h.reshape(b * lp, d)

    cos_t, sin_t = _rope_tables(lp)
    tz = _bias_tiles(rel_bias)
    row2 = lambda v: v[None, :]

    for layer in range(depth):
        i = layer // 2
        if layer % 2 == 0:
            wqm, wqs = _mla_q_weights(ev_w_uq[i])
            wkk, wkv = _mla_kv_weights(ev_w_ukv[i])
            gqm, gqs = _qk_gains(ev_mla_q_norm[i])
            gkm, gks = _qk_gains(ev_mla_k_norm[i])
            gdq = row2(jnp.concatenate([ev_dsa_q_norm[i]] * 2))
            gdk = row2(jnp.concatenate([ev_dsa_k_norm[i]] * 2))
            qm, km, vm, qd, kd, vd, iq, ik, iw = _prep_call(
                h, row2(ev_norm_mix[i]), _mix_in_weights(ev_w_mix_in[i]), row2(ev_g_q_lat[i]),
                row2(ev_g_kv_lat[i]), wqm, wqs, wkk, wkv, gqm, gqs, gkm, gks, gdq, gdk,
                cos_t, sin_t, lp)
            seq3 = lambda a: a.reshape(b, lp, a.shape[1])
            o_mla = _mla_call(seq3(qm), seq3(km), seq3(vm))
            o_dsa = _dsa_call(seq3(iq), seq3(iw), seq3(ik), seq3(qd), seq3(kd), seq3(vd), tz, top_k)
            w_o = ev_w_mix_out[i].astype(BF16)
            n_mla = MLA_HEADS * MLA_V
            h = _mix_out_ffn_call(h, o_mla.reshape(b * lp, -1), o_dsa.reshape(b * lp, -1),
                                  w_o[:n_mla], w_o[n_mla:], row2(ev_norm_ffn[i]),
                                  ev_w1[i], ev_w3[i], ev_w2[i])
        else:
            h = _conv_call(h.reshape(b, lp, d), row2(od_norm_mix[i]), od_w_in[i].astype(BF16),
                           od_conv_w[i].reshape(CONV_WIDTH, d), od_w_out[i].astype(BF16))
            h = h.reshape(b * lp, d)
            wr = jnp.concatenate(
                [od_w_router[i], jnp.zeros((d, LANES - N_EXPERTS), od_w_router.dtype)], axis=1)
            rows = (b, lp, N_META, seq) if layer == depth - 1 and seq % MOE_TILE == 0 else None
            h = _moe_call(h, row2(od_norm_ffn[i]), wr.astype(BF16), od_w1[i], od_w3[i], od_w2[i],
                          rows)
            if rows is not None:
                return h.reshape(b, seq, d)
    return h.reshape(b, lp, d)[:, N_META:l_tot]
```

```python
import functools
import math

import numpy as np
import jax
import jax.numpy as jnp
from jax import lax
from jax.experimental import pallas as pl
from jax.experimental.pallas import tpu as pltpu

F32 = jnp.float32
BF16 = jnp.bfloat16
I32 = jnp.int32
I16 = jnp.int16

D_MODEL = 1024
N_META = 16
BLOCK_Q = 128
EPS = 1e-6
MLA_HEADS = 8
MLA_Q_RANK = 384
MLA_KV_RANK = 256
MLA_NOPE = 64
MLA_ROPE = 32
MLA_V = 64
MLA_QK = MLA_NOPE + MLA_ROPE
MLA_SCALE = MLA_QK ** -0.5
ROPE_BASE = 10000.0
DSA_HEADS = 8
DSA_HEAD_DIM = 64
DSA_WIDTH = DSA_HEADS * DSA_HEAD_DIM
DSA_SCALE = DSA_HEAD_DIM ** -0.5
IDX_HEADS = 8
IDX_DIM = 32
DSA_TOPK_MAX = 256
REL_BUCKETS = 32
REL_MAX_DIST = 128
MIX_IN_SIZES = (MLA_Q_RANK, MLA_KV_RANK, MLA_ROPE, DSA_WIDTH, DSA_WIDTH, DSA_WIDTH,
                IDX_HEADS * IDX_DIM, IDX_DIM, IDX_HEADS)
CONV_WIDTH = 3
N_EXPERTS = 8
TOP_K = 2

LANES = 128
VMEM_LIMIT_BYTES = 56 * 1024 * 1024

SEQ_TILE = 3 * LANES
PREP_TILE = 2 * SEQ_TILE
MOE_TILE = 512
EXPERT_TILE = 2 * MOE_TILE
QK_LOOKAHEAD = 2
HALF_BITS = 16
BITS_PER_EXIT_TEST = 8
LOG2E = math.log2(math.e)
NEG_BIG = -1e30
INT_MIN = -2 ** 31

_C_CQ = 0
_C_CKV = _C_CQ + MLA_Q_RANK
_C_KRM = _C_CKV + MLA_KV_RANK
_C_KRS = _C_KRM + LANES
_C_DQ = _C_KRS + LANES
_C_DK = _C_DQ + DSA_WIDTH
_C_DV = _C_DK + DSA_WIDTH
_C_IQ = _C_DV + DSA_WIDTH
_C_IK = _C_IQ + IDX_HEADS * IDX_DIM
_C_IW = _C_IK + LANES
_C_END = _C_IW + LANES


def _cparams(sem):
    return pltpu.CompilerParams(dimension_semantics=sem, vmem_limit_bytes=VMEM_LIMIT_BYTES)


def _row_tile(n_rows, candidates=(1024, 768, 512, 384, 256, 128)):
    for c in candidates:
        if n_rows % c == 0:
            return c
    raise ValueError(f"no row tile for {n_rows}")


def _rms(x, g):
    ms = jnp.mean(x * x, axis=-1, keepdims=True)
    return x * lax.rsqrt(ms + EPS) * g


def _dot(a, b):
    return jnp.dot(a, b, preferred_element_type=F32)


def _dot_nt(a, b):
    return lax.dot_general(a, b, (((1,), (1,)), ((), ())), preferred_element_type=F32)


def _prep_kernel(h_ref, g_ref, wext_ref, gql_ref, gkvl_ref, wqm_ref, wqs_ref, wkk_ref, wkv_ref,
                 gqm_ref, gqs_ref, gkm_ref, gks_ref, gdq_ref, gdk_ref, cos_ref, sin_ref,
                 qm_o, km_o, vm_o, qd_o, kd_o, vd_o, iq_o, ik_o, iw_o):
    xn = _rms(h_ref[...], g_ref[...]).astype(BF16)

    def proj(lo, hi):
        return _dot(xn, wext_ref[:, lo:hi])

    cos = cos_ref[...]
    sin = sin_ref[...]
    lane = lax.broadcasted_iota(I32, (xn.shape[0], LANES), 1)

    cqn = _rms(proj(_C_CQ, _C_CKV), gql_ref[...]).astype(BF16)
    q_main = _dot(cqn, wqm_ref[...])
    q_swap = _dot(cqn, wqs_ref[...])
    for hd in range(MLA_HEADS):
        sl = slice(hd * LANES, (hd + 1) * LANES)
        a = q_main[:, sl]
        r = lax.rsqrt(jnp.sum(a * a, axis=-1, keepdims=True) * (1.0 / MLA_QK) + EPS)
        out = (a * r * gqm_ref[...]) * cos + (q_swap[:, sl] * r * gqs_ref[...]) * sin
        qm_o[:, sl] = (out * (MLA_SCALE * LOG2E)).astype(BF16)

    ckvn = _rms(proj(_C_CKV, _C_KRM), gkvl_ref[...]).astype(BF16)
    k_nope = _dot(ckvn, wkk_ref[...])
    vm_o[...] = _dot(ckvn, wkv_ref[...]).astype(BF16)
    kr_main = proj(_C_KRM, _C_KRS)
    kr_swap = proj(_C_KRS, _C_DQ)
    for hd in range(MLA_HEADS):
        sl = slice(hd * LANES, (hd + 1) * LANES)
        a = k_nope[:, sl] + kr_main
        r = lax.rsqrt(jnp.sum(a * a, axis=-1, keepdims=True) * (1.0 / MLA_QK) + EPS)
        out = (a * r * gkm_ref[...]) * cos + (kr_swap * r * gks_ref[...]) * sin
        km_o[:, sl] = out.astype(BF16)

    first = lane < DSA_HEAD_DIM
    for (lo, g2_ref, o_ref, post) in ((_C_DQ, gdq_ref, qd_o, DSA_SCALE * LOG2E),
                                      (_C_DK, gdk_ref, kd_o, None)):
        for pr in range(DSA_HEADS // 2):
            x = proj(lo + pr * LANES, lo + (pr + 1) * LANES)
            sq = x * x
            s0 = jnp.sum(jnp.where(first, sq, 0.0), axis=-1, keepdims=True)
            s1 = jnp.sum(jnp.where(first, 0.0, sq), axis=-1, keepdims=True)
            r0 = lax.rsqrt(s0 * (1.0 / DSA_HEAD_DIM) + EPS)
            r1 = lax.rsqrt(s1 * (1.0 / DSA_HEAD_DIM) + EPS)
            out = x * jnp.where(first, r0, r1) * g2_ref[...]
            if post is None:
                o_ref[:, pr * LANES:(pr + 1) * LANES] = out.astype(BF16)
            else:
                out = out * post
                o_ref[:, (2 * pr) * LANES:(2 * pr + 1) * LANES] = jnp.where(first, out, 0.0).astype(BF16)
                o_ref[:, (2 * pr + 1) * LANES:(2 * pr + 2) * LANES] = jnp.where(first, 0.0, out).astype(BF16)

    vd_o[...] = proj(_C_DV, _C_IQ).astype(BF16)
    lane_group = lax.shift_right_logical(lane, int(math.log2(IDX_DIM)))
    for quad in range(IDX_HEADS * IDX_DIM // LANES):
        x = proj(_C_IQ + quad * LANES, _C_IQ + (quad + 1) * LANES)
        for j in range(LANES // IDX_DIM):
            hd = quad * (LANES // IDX_DIM) + j
            iq_o[:, hd * LANES:(hd + 1) * LANES] = jnp.where(lane_group == j, x, 0.0).astype(BF16)
    ik_o[...] = proj(_C_IK, _C_IW).astype(BF16)
    iw_o[...] = proj(_C_IW, _C_END)


def _prep_call(h, g_mix, wext, gql, gkvl, wqm, wqs, wkk, wkv, gqm, gqs, gkm, gks, gdq, gdk,
               cos_t, sin_t, lp):
    t = h.shape[0]
    tm = PREP_TILE
    assert t % tm == 0
    reps = tm // math.gcd(lp, tm)
    nt = reps * lp // tm
    cos_t, sin_t = jnp.tile(cos_t, (reps, 1)), jnp.tile(sin_t, (reps, 1))
    row = lambda w: pl.BlockSpec((tm, w), lambda i: (i, 0))
    full = lambda a: pl.BlockSpec(a.shape, lambda i: (0, 0), pipeline_mode=pl.Buffered(1))
    tab = pl.BlockSpec((tm, LANES), lambda i: (i % nt, 0))
    hw = MLA_HEADS * LANES
    out_shape = [
        jax.ShapeDtypeStruct((t, hw), BF16), jax.ShapeDtypeStruct((t, hw), BF16),
        jax.ShapeDtypeStruct((t, MLA_HEADS * MLA_V), BF16),
        jax.ShapeDtypeStruct((t, DSA_HEADS * LANES), BF16), jax.ShapeDtypeStruct((t, DSA_WIDTH), BF16),
        jax.ShapeDtypeStruct((t, DSA_WIDTH), BF16),
        jax.ShapeDtypeStruct((t, IDX_HEADS * LANES), BF16),
        jax.ShapeDtypeStruct((t, LANES), BF16), jax.ShapeDtypeStruct((t, LANES), F32),
    ]
    return pl.pallas_call(
        _prep_kernel,
        grid=(t // tm,),
        in_specs=[row(D_MODEL), full(g_mix), full(wext), full(gql), full(gkvl), full(wqm),
                  full(wqs), full(wkk), full(wkv), full(gqm), full(gqs), full(gkm), full(gks),
                  full(gdq), full(gdk), tab, tab],
        out_specs=[row(s.shape[1]) for s in out_shape],
        out_shape=out_shape,
        compiler_params=_cparams(("parallel",)),
        name="prep_mix_in",
    )(h, g_mix, wext, gql, gkvl, wqm, wqs, wkk, wkv, gqm, gqs, gkm, gks, gdq, gdk, cos_t, sin_t)


def _flash_init(m_scr, l_scr, acc_scr):
    m_scr[...] = jnp.full(m_scr.shape, -jnp.inf, F32)
    l_scr[...] = jnp.zeros(l_scr.shape, F32)
    acc_scr[...] = jnp.zeros(acc_scr.shape, F32)


def _lane_fold(x, op):
    return functools.reduce(op, [x[:, j * LANES:(j + 1) * LANES] for j in range(x.shape[1] // LANES)])


def _flash_update(hd, s, vc, m_scr, l_scr, acc_scr):
    m_old = m_scr[hd]
    m_new = jnp.maximum(m_old, jnp.max(_lane_fold(s, jnp.maximum), axis=-1, keepdims=True))
    alpha = jnp.exp2(m_old - m_new)
    p = jnp.exp2(s - jnp.concatenate([m_new] * (s.shape[1] // LANES), axis=1))
    m_scr[hd] = m_new
    l_scr[hd] = alpha * l_scr[hd] + _lane_fold(p, jnp.add)
    acc_scr[hd] = alpha * acc_scr[hd] + _dot(p.astype(BF16), vc)


def _flash_store(o_ref, l_scr, acc_scr, head_dim):
    lane = lax.broadcasted_iota(I32, acc_scr.shape[1:], 1)
    for pr in range(acc_scr.shape[0] // 2):
        o0, o1 = [acc_scr[hd] / jnp.sum(l_scr[hd], axis=-1, keepdims=True)
                  for hd in (2 * pr, 2 * pr + 1)]
        o_ref[:, pr * LANES:(pr + 1) * LANES] = jnp.where(lane < head_dim, o0, o1).astype(o_ref.dtype)


def _mla_kernel(q_ref, k_ref, v_ref, o_ref, m_scr, l_scr, acc_scr):
    qi = pl.program_id(1)
    tq = q_ref.shape[0]
    _flash_init(m_scr, l_scr, acc_scr)

    def chunk(c, diagonal):
        ks = pl.ds(pl.multiple_of(c * tq, tq), tq)

        def scores(hd):
            sl = slice(hd * LANES, (hd + 1) * LANES)
            return _dot_nt(q_ref[:, sl], k_ref[ks, sl])

        pending = [scores(hd) for hd in range(QK_LOOKAHEAD)]
        for hd in range(MLA_HEADS):
            if hd + QK_LOOKAHEAD < MLA_HEADS:
                pending.append(scores(hd + QK_LOOKAHEAD))
            s = pending.pop(0)
            if diagonal:
                row = lax.broadcasted_iota(I32, (tq, tq), 0)
                col = lax.broadcasted_iota(I32, (tq, tq), 1)
                s = jnp.where(col <= row, s, NEG_BIG)
            pr = hd // 2
            _flash_update(hd, s, v_ref[ks, pr * LANES:(pr + 1) * LANES], m_scr, l_scr, acc_scr)

    def body(c, carry):
        chunk(c, False)
        return carry

    lax.fori_loop(0, qi, body, 0)
    chunk(qi, True)
    _flash_store(o_ref, l_scr, acc_scr, MLA_V)


def _mla_call(q, k, v):
    b, lp, _ = q.shape
    tq = SEQ_TILE
    nq = lp // tq
    state = pltpu.VMEM((MLA_HEADS, tq, LANES), F32)
    return pl.pallas_call(
        _mla_kernel,
        grid=(b, nq),
        in_specs=[pl.BlockSpec((None, tq, q.shape[2]), lambda bi, i: (bi, i, 0)),
                  pl.BlockSpec((None, lp, k.shape[2]), lambda bi, i: (bi, 0, 0)),
                  pl.BlockSpec((None, lp, v.shape[2]), lambda bi, i: (bi, 0, 0))],
        out_specs=pl.BlockSpec((None, tq, v.shape[2]), lambda bi, i: (bi, i, 0)),
        out_shape=jax.ShapeDtypeStruct((b, lp, v.shape[2]), BF16),
        scratch_shapes=[state, state, state],
        compiler_params=_cparams(("parallel", "arbitrary")),
        name="mla_attention",
    )(q, k, v)


def _dsa_kernel(iq_ref, iw_ref, ik_ref, q_ref, k_ref, v_ref, tz_ref, o_ref,
                key_scr, cap_scr, hi_scr, lo_scr, wt_scr, ans_scr, jst_scr, m_scr, l_scr, acc_scr,
                *, top_k):
    qi = pl.program_id(1)
    tq = q_ref.shape[0]
    nsub = tq // LANES
    sublanes = 8
    n_chunks = qi + 1
    key_i = lax.broadcasted_iota(I32, (tq, tq), 0)
    qry_i = lax.broadcasted_iota(I32, (tq, tq), 1)
    chunk_rows = lambda c: pl.ds(pl.multiple_of(c * tq, tq), tq)
    head_slot = lambda hd: slice(hd * LANES, (hd + 1) * LANES)
    wt_scr[...] = jnp.transpose(iw_ref[...])[0:IDX_HEADS, :]

    def index_chunk(c, diagonal):
        ikc = ik_ref[chunk_rows(c), :]
        sc = jnp.zeros((tq, tq), F32)
        for hd in range(IDX_HEADS):
            act = jnp.maximum(_dot_nt(ikc, iq_ref[:, head_slot(hd)]), 0.0)
            sc = sc + wt_scr[hd:hd + 1, :] * act
        sc = jnp.where(sc == 0.0, 0.0, sc)
        if diagonal:
            sc = jnp.where(key_i <= qry_i, sc, -jnp.inf)
        bits = pltpu.bitcast(sc, I32)
        key = bits ^ (lax.shift_right_arithmetic(bits, 31) & 0x7FFFFFFF)
        key_scr[c] = key
        hi_scr[c] = lax.shift_right_arithmetic(key, HALF_BITS).astype(I16)

    def index_body(c, carry):
        index_chunk(c, False)
        return carry

    lax.fori_loop(0, qi, index_body, 0)
    index_chunk(qi, True)

    kf = float(top_k)
    vec = (sublanes, tq)
    sub_i = lax.broadcasted_iota(I32, vec, 0)

    def count(pred):
        def body(c, accs):
            accs = list(accs)
            kk = key_scr[c]
            for g in range(tq // sublanes):
                hit = jnp.where(pred(kk[g * sublanes:(g + 1) * sublanes, :], c * tq + g * sublanes),
                                1.0, 0.0)
                accs[g % 2] = accs[g % 2] + hit
            return tuple(accs)
        zero = jnp.zeros(vec, F32)
        a0, a1 = lax.fori_loop(0, n_chunks, body, (zero, zero))
        return jnp.broadcast_to(jnp.sum(a0 + a1, axis=0, keepdims=True), vec)

    packed = 16

    def count_half(half_scr, cand):
        c16 = jnp.broadcast_to(cand[0:1, :], (packed, tq)).astype(I16)

        def body(c, accs):
            accs = list(accs)
            for g in range(tq // packed):
                kk = half_scr[c, g * packed:(g + 1) * packed, :]
                accs[g % 2] = accs[g % 2] + jnp.where(kk >= c16, jnp.int16(1), jnp.int16(0))
            return tuple(accs)
        zero = jnp.zeros((packed, tq), I16)
        a0, a1 = lax.fori_loop(0, n_chunks, body, (zero, zero))
        tot = a0.astype(I32) + a1.astype(I32)
        return jnp.broadcast_to(jnp.sum(tot, axis=0, keepdims=True).astype(F32), vec)

    def n_open_of(done):
        return jnp.sum(jnp.where(done == 0, 1.0, 0.0))

    def bisect(count_ge, ans, done):
        def step(bit, ans, done):
            cand = ans + lax.shift_left(jnp.int32(1), bit)
            cnt = count_ge(cand)
            open_ = done == 0
            ans = jnp.where(jnp.logical_and(open_, cnt >= kf), cand, ans)
            done = jnp.where(jnp.logical_and(open_, cnt == kf), 1, done)
            return ans, done

        def cond(st):
            bit, _, _, n_open = st
            return jnp.logical_and(bit >= 0, n_open > 0.0)

        def body(st):
            bit, ans, done, _ = st
            for k in range(BITS_PER_EXIT_TEST):
                ans, done = step(bit - k, ans, done)
            return bit - BITS_PER_EXIT_TEST, ans, done, n_open_of(done)

        _, ans, done, _ = lax.while_loop(cond, body,
                                         (jnp.int32(HALF_BITS - 1), ans, done, n_open_of(done)))
        return ans, done

    t_pos = qi * tq + lax.broadcasted_iota(I32, vec, 1)
    done0 = (t_pos + 1 <= top_k).astype(I32)
    half_min = -(1 << (HALF_BITS - 1))

    hi_ans, done = bisect(lambda cand: count_half(hi_scr, cand), jnp.full(vec, half_min, I32), done0)

    hi_row = hi_ans[0:1, :]

    def low_chunk(c, carry):
        key = key_scr[c]
        low = (key & ((1 << HALF_BITS) - 1)) + half_min
        inside = lax.shift_right_arithmetic(key, HALF_BITS) == hi_row
        lo_scr[c] = jnp.where(inside, low, half_min).astype(I16)
        return carry

    lax.fori_loop(0, n_chunks, low_chunk, 0)
    above = count_half(hi_scr, hi_ans + 1)
    lo_ans, done = bisect(lambda cand: above + count_half(lo_scr, cand + half_min),
                          jnp.zeros(vec, I32), done)
    ans = lax.shift_left(hi_ans, HALF_BITS) + lo_ans
    n_open = n_open_of(done)
    ans_scr[...] = ans
    jst_scr[...] = jnp.full(vec, 2 ** 31 - 1, I32)

    @pl.when(n_open > 0.0)
    def _():
        need = kf - count(lambda kk, _i: kk > ans)

        def tie_body(i, jst):
            cand = jst + lax.shift_left(jnp.int32(1), 13 - i)
            cnt = count(lambda kk, i0: jnp.logical_and(kk == ans, sub_i + i0 < cand))
            return jnp.where(cnt < need, cand, jst)

        jst = lax.fori_loop(0, 14, tie_body, jnp.zeros(vec, I32))
        jst_scr[...] = jnp.where(done == 0, jst, 2 ** 31 - 1)

    ans_row = ans_scr[0:1, :]
    jst_row = jst_scr[0:1, :]
    eye = jnp.where(key_i == qry_i, 1.0, 0.0).astype(BF16)

    def mask_chunk(c, diagonal):
        kk = key_scr[c]
        sel = jnp.logical_or(kk > ans_row,
                             jnp.logical_and(kk == ans_row, key_i + c * tq <= jst_row))
        if diagonal:
            sel = jnp.logical_and(sel, key_i <= qry_i)
        cap_scr[c] = _dot_nt(eye, jnp.where(sel, -NEG_BIG, NEG_BIG).astype(BF16))

    def mask_body(c, carry):
        mask_chunk(c, False)
        return carry

    lax.fori_loop(0, qi, mask_body, 0)
    mask_chunk(qi, True)

    _flash_init(m_scr, l_scr, acc_scr)

    def bias_of(hd, where_):
        near = tz_ref[hd, :, LANES:2 * LANES]
        far = tz_ref[hd, :, 0:LANES]
        z = jnp.zeros((LANES, LANES), F32)
        if where_ == "previous":
            blocks = [[far if (a == 0 and b == nsub - 1) else z for b in range(nsub)]
                      for a in range(nsub)]
        else:
            blocks = [[near if b == a else far if b == a - 1 else z for b in range(nsub)]
                      for a in range(nsub)]
        return jnp.concatenate([jnp.concatenate(r, axis=1) for r in blocks], axis=0)

    def chunk(c, where_):
        ks = chunk_rows(c)
        cap = cap_scr[c]
        pair = lambda hd: slice((hd // 2) * LANES, (hd // 2 + 1) * LANES)
        scores = lambda hd: _dot_nt(q_ref[:, head_slot(hd)], k_ref[ks, pair(hd)])

        pending = [scores(hd) for hd in range(QK_LOOKAHEAD)]
        for hd in range(DSA_HEADS):
            if hd + QK_LOOKAHEAD < DSA_HEADS:
                pending.append(scores(hd + QK_LOOKAHEAD))
            s = jnp.minimum(pending.pop(0), cap)
            if where_ is not None:
                s = s + bias_of(hd, where_)
            _flash_update(hd, s, v_ref[ks, pair(hd)], m_scr, l_scr, acc_scr)

    def body(c, carry):
        chunk(c, None)
        return carry

    lax.fori_loop(0, jnp.maximum(qi - 1, 0), body, 0)

    @pl.when(qi >= 1)
    def _():
        chunk(qi - 1, "previous")

    chunk(qi, "diagonal")
    _flash_store(o_ref, l_scr, acc_scr, DSA_HEAD_DIM)


def _dsa_call(iq, iw, ik, q, k, v, tz, top_k):
    b, lp, _ = q.shape
    tq = SEQ_TILE
    nq = lp // tq
    qspec = lambda w: pl.BlockSpec((None, tq, w), lambda bi, i: (bi, i, 0))
    kspec = lambda w: pl.BlockSpec((None, lp, w), lambda bi, i: (bi, 0, 0))
    state = pltpu.VMEM((DSA_HEADS, tq, LANES), F32)
    return pl.pallas_call(
        functools.partial(_dsa_kernel, top_k=top_k),
        grid=(b, nq),
        in_specs=[qspec(iq.shape[2]), qspec(LANES), kspec(LANES), qspec(q.shape[2]),
                  kspec(DSA_WIDTH), kspec(DSA_WIDTH),
                  pl.BlockSpec(tz.shape, lambda bi, i: (0, 0, 0))],
        out_specs=qspec(DSA_WIDTH),
        out_shape=jax.ShapeDtypeStruct((b, lp, DSA_WIDTH), BF16),
        scratch_shapes=[pltpu.VMEM((nq, tq, tq), I32), pltpu.VMEM((nq, tq, tq), F32),
                        pltpu.VMEM((nq, tq, tq), I16), pltpu.VMEM((nq, tq, tq), I16),
                        pltpu.VMEM((IDX_HEADS, tq), F32),
                        pltpu.VMEM((8, tq), I32), pltpu.VMEM((8, tq), I32),
                        state, state, state],
        compiler_params=_cparams(("parallel", "arbitrary")),
        name="dsa_attention",
    )(iq, iw, ik, q, k, v, tz)


def _swiglu_chunk(xb, w1_ref, w3_ref, w2_ref):
    a = _dot(xb, w1_ref[...].astype(BF16))
    act = (a * jax.nn.sigmoid(a)) * _dot(xb, w3_ref[...].astype(BF16))
    return _dot(act.astype(BF16), w2_ref[...].astype(BF16))


def _mix_out_ffn_kernel(h_ref, a_ref, b_ref, wa_ref, wb_ref, g_ref, w1_ref, w3_ref, w2_ref, o_ref,
                        h1_scr, xn_scr, acc_scr):
    f = pl.program_id(1)

    @pl.when(f == 0)
    def _():
        h1 = h_ref[...] + _dot(a_ref[...], wa_ref[...]) + _dot(b_ref[...], wb_ref[...])
        h1_scr[...] = h1
        xn_scr[...] = _rms(h1, g_ref[...]).astype(BF16)
        acc_scr[...] = jnp.zeros(acc_scr.shape, F32)

    acc_scr[...] += _swiglu_chunk(xn_scr[...], w1_ref, w3_ref, w2_ref)

    @pl.when(f == pl.num_programs(1) - 1)
    def _():
        o_ref[...] = h1_scr[...] + acc_scr[...]


def _mix_out_ffn_call(h, a, b, wa, wb, g, w1, w3, w2):
    t = h.shape[0]
    d_ff = w1.shape[1]
    tm = _row_tile(t)
    tf = 512 if d_ff % 512 == 0 else d_ff
    row = lambda w: pl.BlockSpec((tm, w), lambda i, f: (i, 0))
    full = lambda x: pl.BlockSpec(x.shape, lambda i, f: (0, 0), pipeline_mode=pl.Buffered(1))
    return pl.pallas_call(
        _mix_out_ffn_kernel,
        grid=(t // tm, d_ff // tf),
        in_specs=[row(D_MODEL), row(a.shape[1]), row(b.shape[1]), full(wa), full(wb), full(g),
                  pl.BlockSpec((D_MODEL, tf), lambda i, f: (0, f)),
                  pl.BlockSpec((D_MODEL, tf), lambda i, f: (0, f)),
                  pl.BlockSpec((tf, D_MODEL), lambda i, f: (f, 0))],
        out_specs=row(D_MODEL),
        out_shape=jax.ShapeDtypeStruct(h.shape, F32),
        scratch_shapes=[pltpu.VMEM((tm, D_MODEL), F32), pltpu.VMEM((tm, D_MODEL), BF16),
                        pltpu.VMEM((tm, D_MODEL), F32)],
        compiler_params=_cparams(("parallel", "arbitrary")),
        name="mix_out_swiglu",
    )(h, a, b, wa, wb, g, w1, w3, w2)


def _conv_kernel(h_ref, g_ref, win_ref, cw_ref, wout_ref, o_ref, z_scr):
    i = pl.program_id(1)
    tm = h_ref.shape[0]
    halo = 8
    x = h_ref[...]
    xn = _rms(x, g_ref[...]).astype(BF16)
    c_gate = _dot(xn, win_ref[:, D_MODEL:2 * D_MODEL])
    u = _dot(xn, win_ref[:, 2 * D_MODEL:3 * D_MODEL])
    z = c_gate * u

    @pl.when(i == 0)
    def _():
        z_scr[0:halo, :] = jnp.zeros((halo, D_MODEL), F32)

    @pl.when(i > 0)
    def _():
        z_scr[0:halo, :] = z_scr[tm:tm + halo, :]

    z_scr[halo:halo + tm, :] = z
    cw = cw_ref[...]
    y = (cw[0:1, :] * z_scr[halo - 2:halo - 2 + tm, :]
         + cw[1:2, :] * z_scr[halo - 1:halo - 1 + tm, :]
         + cw[2:3, :] * z)
    b_gate = _dot(xn, win_ref[:, 0:D_MODEL])
    o_ref[...] = x + _dot((b_gate * y).astype(BF16), wout_ref[...])


def _conv_call(h3, g, w_in, cw, w_out):
    b, lp, d = h3.shape
    tm = SEQ_TILE
    full = lambda a: pl.BlockSpec(a.shape, lambda bi, i: (0,) * a.ndim)
    return pl.pallas_call(
        _conv_kernel,
        grid=(b, lp // tm),
        in_specs=[pl.BlockSpec((None, tm, d), lambda bi, i: (bi, i, 0)), full(g), full(w_in),
                  full(cw), full(w_out)],
        out_specs=pl.BlockSpec((None, tm, d), lambda bi, i: (bi, i, 0)),
        out_shape=jax.ShapeDtypeStruct(h3.shape, F32),
        scratch_shapes=[pltpu.VMEM((tm + 8, d), F32)],
        compiler_params=_cparams(("arbitrary", "arbitrary")),
        name="short_conv_mixer",
    )(h3, g, w_in, cw, w_out)


def _as_tiles(x):
    return x.reshape(x.shape[0], D_MODEL // LANES, LANES)


def _as_rows(x3):
    return x3.reshape(x3.shape[0], D_MODEL)


def _route_kernel(h_ref, g_ref, wr_ref, xg_ref, didx_ref, gate_ref, cnt_ref,
                  xs_scr, base_scr, dvm_scr, dsm_scr, sem_idx, sem_rows, *, stride):
    i = pl.program_id(0)
    tm = h_ref.shape[0]
    lane = lax.broadcasted_iota(I32, (tm, LANES), 1)
    lane_f = lane.astype(F32)

    @pl.when(i == 0)
    def _():
        base_scr[...] = jnp.zeros(base_scr.shape, F32)

    xnf = _rms(h_ref[...], g_ref[...])
    logits = jnp.where(lane < N_EXPERTS, _dot(xnf.astype(BF16), wr_ref[...]), -jnp.inf)
    v1 = jnp.max(logits, axis=-1, keepdims=True)
    i1 = jnp.min(jnp.where(logits == v1, lane_f, float(LANES)), axis=-1, keepdims=True)
    rest = jnp.where(lane_f == i1, -jnp.inf, logits)
    v2 = jnp.max(rest, axis=-1, keepdims=True)
    i2 = jnp.min(jnp.where(rest == v2, lane_f, float(LANES)), axis=-1, keepdims=True)
    e2 = jnp.exp(v2 - v1)
    den = 1.0 + e2
    gates = (1.0 / den, e2 / den)

    oh1 = jnp.where(lane_f == i1, 1.0, 0.0)
    oh2 = jnp.where(lane_f == i2, 1.0, 0.0)
    oh = oh1 + oh2
    earlier = (lax.broadcasted_iota(I32, (tm, tm), 1) < lax.broadcasted_iota(I32, (tm, tm), 0))
    prefix = _dot(jnp.where(earlier, 1.0, 0.0).astype(BF16), oh.astype(BF16))
    pos = base_scr[0:1, :] + prefix
    d1 = jnp.sum(oh1 * pos, axis=-1, keepdims=True) + i1 * float(stride)
    d2 = jnp.sum(oh2 * pos, axis=-1, keepdims=True) + i2 * float(stride)
    base_scr[...] = base_scr[...] + jnp.sum(oh, axis=0, keepdims=True)
    cnt_ref[...] = base_scr[...]

    gate_ref[...] = jnp.where(lane == 0, gates[0], jnp.where(lane == 1, gates[1], 0.0))
    slot = lax.rem(i, 2)
    xs_scr[slot] = _as_tiles(xnf)

    dmat = jnp.where(lane == 0, d1, jnp.where(lane == 1, d2, 0.0))
    dvm_scr[...] = jnp.transpose(dmat)[0:8, :].astype(I32)
    didx_ref[...] = dvm_scr[...]
    cp = pltpu.make_async_copy(dvm_scr, dsm_scr, sem_idx)
    cp.start()
    cp.wait()

    for r in range(tm):
        for s in range(TOP_K):
            pltpu.make_async_copy(xs_scr.at[slot, r], xg_ref.at[dsm_scr[s, r]],
                                  sem_rows.at[slot]).start(priority=s)

    def wait_rows(sl):
        for s in range(TOP_K):
            pltpu.make_async_copy(xs_scr.at[sl], xg_ref.at[pl.ds(0, tm)], sem_rows.at[sl]).wait()

    @pl.when(i > 0)
    def _():
        wait_rows(1 - slot)

    @pl.when(i == pl.num_programs(0) - 1)
    def _():
        wait_rows(slot)
        xs_scr[0] = jnp.zeros(xs_scr.shape[1:], F32)
        dvm_scr[:, 0:LANES] = base_scr[...].astype(I32)
        cp2 = pltpu.make_async_copy(dvm_scr, dsm_scr, sem_idx)
        cp2.start()
        cp2.wait()
        for phase in ("start", "wait"):
            for e in range(N_EXPERTS):
                for k in range(EXPERT_TILE // tm):
                    blk = pltpu.make_async_copy(
                        xs_scr.at[0], xg_ref.at[pl.ds(e * stride + dsm_scr[0, e] + k * tm, tm)],
                        sem_rows.at[0])
                    blk.start() if phase == "start" else blk.wait()


def _route_call(h, g, wr, stride):
    t = h.shape[0]
    tm = MOE_TILE
    return pl.pallas_call(
        functools.partial(_route_kernel, stride=stride),
        grid=(t // tm,),
        in_specs=[pl.BlockSpec((tm, D_MODEL), lambda i: (i, 0)),
                  pl.BlockSpec(g.shape, lambda i: (0, 0)),
                  pl.BlockSpec(wr.shape, lambda i: (0, 0))],
        out_specs=[pl.BlockSpec(memory_space=pl.ANY),
                   pl.BlockSpec((8, tm), lambda i: (0, i)),
                   pl.BlockSpec((tm, LANES), lambda i: (i, 0)),
                   pl.BlockSpec((8, LANES), lambda i: (0, 0))],
        out_shape=[jax.ShapeDtypeStruct((N_EXPERTS * stride, D_MODEL // LANES, LANES), F32),
                   jax.ShapeDtypeStruct((8, t), I32),
                   jax.ShapeDtypeStruct((t, LANES), F32),
                   jax.ShapeDtypeStruct((8, LANES), F32)],
        scratch_shapes=[pltpu.VMEM((2, tm, D_MODEL // LANES, LANES), F32), pltpu.VMEM((8, LANES), F32),
                        pltpu.VMEM((8, tm), I32), pltpu.SMEM((8, tm), I32),
                        pltpu.SemaphoreType.DMA, pltpu.SemaphoreType.DMA((2,))],
        compiler_params=_cparams(("arbitrary",)),
        name="moe_route",
    )(h, g, wr)


def _expert_kernel(blk_ref, exp_ref, nu_ref, x_ref, w1_ref, w3_ref, w2_ref, o_ref, xb_scr, acc_scr):
    j = pl.program_id(0)
    f = pl.program_id(1)

    @pl.when(j < nu_ref[0])
    def _():
        @pl.when(f == 0)
        def _():
            xb_scr[...] = _as_rows(x_ref[...]).astype(BF16)
            acc_scr[...] = jnp.zeros(acc_scr.shape, F32)

        acc_scr[...] += _swiglu_chunk(xb_scr[...], w1_ref, w3_ref, w2_ref)

        @pl.when(f == pl.num_programs(1) - 1)
        def _():
            o_ref[...] = _as_tiles(acc_scr[...])


def _expert_call(tile_blk, tile_exp, n_used, xg, w1, w3, w2):
    d_ff = w1.shape[2]
    tm = EXPERT_TILE
    tf = 512 if d_ff % 512 == 0 else d_ff
    nf = d_ff // tf
    n_tiles = tile_blk.shape[0]
    fe = lambda j, f, nu: jnp.where(j < nu[0], f, nf - 1)
    tile_block = pl.BlockSpec((tm, D_MODEL // LANES, LANES), lambda j, f, blk, ex, nu: (blk[j], 0, 0))
    grid_spec = pltpu.PrefetchScalarGridSpec(
        num_scalar_prefetch=3,
        grid=(n_tiles, nf),
        in_specs=[tile_block,
                  pl.BlockSpec((None, D_MODEL, tf), lambda j, f, blk, ex, nu: (ex[j], 0, fe(j, f, nu))),
                  pl.BlockSpec((None, D_MODEL, tf), lambda j, f, blk, ex, nu: (ex[j], 0, fe(j, f, nu))),
                  pl.BlockSpec((None, tf, D_MODEL), lambda j, f, blk, ex, nu: (ex[j], fe(j, f, nu), 0))],
        out_specs=tile_block,
        scratch_shapes=[pltpu.VMEM((tm, D_MODEL), BF16), pltpu.VMEM((tm, D_MODEL), F32)])
    return pl.pallas_call(
        _expert_kernel,
        grid_spec=grid_spec,
        out_shape=jax.ShapeDtypeStruct(xg.shape, F32),
        compiler_params=_cparams(("arbitrary", "arbitrary")),
        name="expert_swiglu",
    )(tile_blk, tile_exp, n_used, xg, w1, w3, w2)


def _combine_kernel(h_ref, gate_ref, didx_ref, yg_ref, o_ref, y_scr, dsm_scr, sem_idx, sem_rows):
    i = pl.program_id(0)
    tm = h_ref.shape[0]
    slot = lax.rem(i, 2)

    def fetch(j, sl):
        cp = pltpu.make_async_copy(didx_ref.at[:, pl.ds(pl.multiple_of(j * tm, tm), tm)], dsm_scr,
                                   sem_idx)
        cp.start()
        cp.wait()
        for r in range(tm):
            for s in range(TOP_K):
                pltpu.make_async_copy(yg_ref.at[dsm_scr[s, r]], y_scr.at[sl, s, r],
                                      sem_rows.at[sl]).start(priority=s)

    @pl.when(i == 0)
    def _():
        fetch(0, 0)

    @pl.when(i + 1 < pl.num_programs(0))
    def _():
        fetch(i + 1, 1 - slot)

    for s in range(TOP_K):
        pltpu.make_async_copy(yg_ref.at[pl.ds(0, tm)], y_scr.at[slot, s], sem_rows.at[slot]).wait()
    gate = gate_ref[...]
    o_ref[...] = (h_ref[...] + gate[:, 0:1] * _as_rows(y_scr[slot, 0])
                  + gate[:, 1:2] * _as_rows(y_scr[slot, 1]))


def _combine_call(h, gate, didx, yg, rows=None):
    t = h.shape[0]
    tm = MOE_TILE
    n_seq, seq_stride, first, n = rows if rows is not None else (1, 0, 0, t)
    assert n % tm == 0 and first % 8 == 0 and seq_stride % 8 == 0
    tiles = n // tm
    if rows is not None:
        didx = didx.reshape(8, n_seq, seq_stride)[:, :, first:first + n].reshape(8, n_seq * n)

    def token_rows(width):
        start = lambda i: pl.multiple_of((i // tiles) * seq_stride + first + (i % tiles) * tm, 8)
        return pl.BlockSpec((pl.Element(tm), pl.Element(width)), lambda i: (start(i), 0))

    return pl.pallas_call(
        _combine_kernel,
        grid=(n_seq * tiles,),
        in_specs=[token_rows(D_MODEL), token_rows(LANES),
                  pl.BlockSpec(memory_space=pl.ANY), pl.BlockSpec(memory_space=pl.ANY)],
        out_specs=pl.BlockSpec((tm, D_MODEL), lambda i: (i, 0)),
        out_shape=jax.ShapeDtypeStruct((n_seq * n, D_MODEL), F32),
        scratch_shapes=[pltpu.VMEM((2, TOP_K, tm, D_MODEL // LANES, LANES), F32),
                        pltpu.SMEM((8, tm), I32),
                        pltpu.SemaphoreType.DMA, pltpu.SemaphoreType.DMA((2,))],
        compiler_params=_cparams(("arbitrary",)),
        name="moe_combine",
    )(h, gate, didx, yg)


def _moe_call(h, g, wr, w1, w3, w2, rows=None):
    t = h.shape[0]
    tm = EXPERT_TILE
    assert t % MOE_TILE == 0 and tm % MOE_TILE == 0
    stride = -(-t // tm) * tm + tm
    xg, didx, gate, cnt = _route_call(h, g, wr, stride)
    counts = cnt[0, :N_EXPERTS].astype(I32)
    tiles_e = (counts + tm - 1) // tm
    cum = jnp.cumsum(tiles_e)
    n_used = cum[-1]
    n_tiles = -(-TOP_K * t // tm) + N_EXPERTS
    jj = jnp.minimum(jnp.arange(n_tiles, dtype=I32), n_used - 1)
    tile_exp = jnp.sum((jj[:, None] >= cum[None, :]).astype(I32), axis=1)
    tile_blk = tile_exp * (stride // tm) + jj - (cum - tiles_e)[tile_exp]
    yg = _expert_call(tile_blk, tile_exp, n_used[None], xg, w1, w3, w2)
    return _combine_call(h, gate, didx, yg, rows)


def _mix_in_weights(w):
    offs = np.concatenate([[0], np.cumsum(MIX_IN_SIZES)])
    cq, ckv, kr, dq, dk, dv, iq, ik, iw = [w[:, offs[j]:offs[j + 1]] for j in range(9)]
    z = lambda n: jnp.zeros((w.shape[0], n), w.dtype)
    half = MLA_ROPE // 2
    kr_main = jnp.concatenate([z(MLA_NOPE), kr, z(LANES - MLA_QK)], axis=1)
    kr_swap = jnp.concatenate([z(MLA_NOPE), kr[:, half:], kr[:, :half], z(LANES - MLA_QK)], axis=1)
    ik4 = jnp.concatenate([ik] * (LANES // IDX_DIM), axis=1)
    iw_p = jnp.concatenate([iw, z(LANES - IDX_HEADS)], axis=1)
    return jnp.concatenate([cq, ckv, kr_main, kr_swap, dq, dk, dv, iq, ik4, iw_p],
                           axis=1).astype(BF16)


def _mla_q_weights(w_uq):
    r = w_uq.shape[0]
    w = w_uq.reshape(r, MLA_HEADS, MLA_QK)
    nope, rope = w[..., :MLA_NOPE], w[..., MLA_NOPE:]
    half = MLA_ROPE // 2
    z = lambda n: jnp.zeros((r, MLA_HEADS, n), w.dtype)
    main = jnp.concatenate([nope, rope, z(LANES - MLA_QK)], axis=-1)
    swap = jnp.concatenate([z(MLA_NOPE), rope[..., half:], rope[..., :half], z(LANES - MLA_QK)],
                           axis=-1)
    return (main.reshape(r, MLA_HEADS * LANES).astype(BF16),
            swap.reshape(r, MLA_HEADS * LANES).astype(BF16))


def _mla_kv_weights(w_ukv):
    r = w_ukv.shape[0]
    w = w_ukv.reshape(r, MLA_HEADS, MLA_NOPE + MLA_V)
    k_nope = jnp.concatenate([w[..., :MLA_NOPE], jnp.zeros((r, MLA_HEADS, LANES - MLA_NOPE), w.dtype)],
                             axis=-1)
    return (k_nope.reshape(r, MLA_HEADS * LANES).astype(BF16),
            w[..., MLA_NOPE:].reshape(r, MLA_HEADS * MLA_V).astype(BF16))


def _qk_gains(g):
    half = MLA_ROPE // 2
    z = lambda n: jnp.zeros((n,), g.dtype)
    main = jnp.concatenate([g, z(LANES - MLA_QK)])
    swap = jnp.concatenate([z(MLA_NOPE), g[MLA_NOPE + half:], g[MLA_NOPE:MLA_NOPE + half],
                            z(LANES - MLA_QK)])
    return main[None, :], swap[None, :]


def _rope_tables(lp):
    half = MLA_ROPE // 2
    inv = ROPE_BASE ** (-jnp.arange(half, dtype=F32) / half)
    ang = jnp.arange(lp, dtype=jnp.int32).astype(F32)[:, None] * inv[None, :]
    cos, sin = jnp.cos(ang), jnp.sin(ang)
    ones = jnp.ones((lp, MLA_NOPE), F32)
    pad1 = jnp.ones((lp, LANES - MLA_QK), F32)
    zeros = jnp.zeros((lp, MLA_NOPE), F32)
    pad0 = jnp.zeros((lp, LANES - MLA_QK), F32)
    return (jnp.concatenate([ones, cos, cos, pad1], axis=1),
            jnp.concatenate([zeros, -sin, sin, pad0], axis=1))


def _rel_buckets(n):
    max_exact = REL_BUCKETS // 2
    d = np.arange(n)
    df = np.maximum(d, 1).astype(np.float32)
    large = max_exact + (np.log(df / np.float32(max_exact))
                         / np.float32(math.log(REL_MAX_DIST / max_exact))
                         * np.float32(REL_BUCKETS - max_exact)).astype(np.int32)
    large = np.minimum(large, REL_BUCKETS - 1)
    return np.where(d < max_exact, d, large)


def _bias_tiles(rel_bias):
    buckets = _rel_buckets(2 * LANES)
    assert np.all(buckets[LANES - 1:] == REL_BUCKETS - 1)
    n_heads = rel_bias.shape[1]
    shifted = (rel_bias - rel_bias[REL_BUCKETS - 1:REL_BUCKETS, :]) * LOG2E
    period = 3 * LANES
    w = jnp.concatenate([jnp.take(shifted, jnp.asarray(buckets), axis=0),
                         jnp.zeros((period - 2 * LANES, n_heads), F32)], axis=0)
    u = jnp.roll(jnp.flip(w, axis=0), -(2 * LANES - 1), axis=0).T
    flat = jnp.tile(u, (1, LANES))[:, :LANES * (period - 1)]
    return flat.reshape(n_heads, LANES, period - 1)[:, :, :2 * LANES].astype(F32)


def kernel(x, meta_tokens, rel_bias, ev_norm_mix, ev_w_mix_in, ev_g_q_lat, ev_g_kv_lat, ev_w_uq,
           ev_w_ukv, ev_mla_q_norm, ev_mla_k_norm, ev_dsa_q_norm, ev_dsa_k_norm, ev_w_mix_out,
           ev_norm_ffn, ev_w1, ev_w3, ev_w2, od_norm_mix, od_w_in, od_conv_w, od_w_out,
           od_norm_ffn, od_w_router, od_w1, od_w3, od_w2):
    b, seq, d = x.shape
    assert d == D_MODEL
    l_tot = seq + N_META
    lp = -(-l_tot // BLOCK_Q) * BLOCK_Q
    assert lp % SEQ_TILE == 0, "sequence tiling assumes the padded length is a multiple of 384"
    top_k = min(DSA_TOPK_MAX, l_tot // 4)
    depth = ev_norm_mix.shape[0] + od_norm_mix.shape[0]

    meta = jnp.broadcast_to(meta_tokens[None].astype(x.dtype), (b, N_META, d))
    h = jnp.concatenate([meta, x, jnp.zeros((b, lp - l_tot, d), x.dtype)], axis=1)
    h = h.reshape(b * lp, d)

    cos_t, sin_t = _rope_tables(lp)
    tz = _bias_tiles(rel_bias)
    row2 = lambda v: v[None, :]

    for layer in range(depth):
        i = layer // 2
        if layer % 2 == 0:
            wqm, wqs = _mla_q_weights(ev_w_uq[i])
            wkk, wkv = _mla_kv_weights(ev_w_ukv[i])
            gqm, gqs = _qk_gains(ev_mla_q_norm[i])
            gkm, gks = _qk_gains(ev_mla_k_norm[i])
            gdq = row2(jnp.concatenate([ev_dsa_q_norm[i]] * 2))
            gdk = row2(jnp.concatenate([ev_dsa_k_norm[i]] * 2))
            qm, km, vm, qd, kd, vd, iq, ik, iw = _prep_call(
                h, row2(ev_norm_mix[i]), _mix_in_weights(ev_w_mix_in[i]), row2(ev_g_q_lat[i]),
                row2(ev_g_kv_lat[i]), wqm, wqs, wkk, wkv, gqm, gqs, gkm, gks, gdq, gdk,
                cos_t, sin_t, lp)
            seq3 = lambda a: a.reshape(b, lp, a.shape[1])
            o_mla = _mla_call(seq3(qm), seq3(km), seq3(vm))
            o_dsa = _dsa_call(seq3(iq), seq3(iw), seq3(ik), seq3(qd), seq3(kd), seq3(vd), tz, top_k)
            w_o = ev_w_mix_out[i].astype(BF16)
            n_mla = MLA_HEADS * MLA_V
            h = _mix_out_ffn_call(h, o_mla.reshape(b * lp, -1), o_dsa.reshape(b * lp, -1),
                                  w_o[:n_mla], w_o[n_mla:], row2(ev_norm_ffn[i]),
                                  ev_w1[i], ev_w3[i], ev_w2[i])
        else:
            h = _conv_call(h.reshape(b, lp, d), row2(od_norm_mix[i]), od_w_in[i].astype(BF16),
                           od_conv_w[i].reshape(CONV_WIDTH, d), od_w_out[i].astype(BF16))
            h = h.reshape(b * lp, d)
            wr = jnp.concatenate(
                [od_w_router[i], jnp.zeros((d, LANES - N_EXPERTS), od_w_router.dtype)], axis=1)
            rows = (b, lp, N_META, seq) if layer == depth - 1 and seq % MOE_TILE == 0 else None
            h = _moe_call(h, row2(od_norm_ffn[i]), wr.astype(BF16), od_w1[i], od_w3[i], od_w2[i],
                          rows)
            if rows is not None:
                return h.reshape(b, seq, d)
    return h.reshape(b, lp, d)[:, N_META:l_tot]
```

```python
import functools
import math

import numpy as np
import jax
import jax.numpy as jnp
from jax import lax
from jax.experimental import pallas as pl
from jax.experimental.pallas import tpu as pltpu

F32 = jnp.float32
BF16 = jnp.bfloat16
I32 = jnp.int32

D_MODEL = 1024
N_META = 16
BLOCK_Q = 128
EPS = 1e-6
MLA_HEADS = 8
MLA_Q_RANK = 384
MLA_KV_RANK = 256
MLA_NOPE = 64
MLA_ROPE = 32
MLA_V = 64
MLA_QK = MLA_NOPE + MLA_ROPE
MLA_SCALE = MLA_QK ** -0.5
ROPE_BASE = 10000.0
DSA_HEADS = 8
DSA_HEAD_DIM = 64
DSA_WIDTH = DSA_HEADS * DSA_HEAD_DIM
DSA_SCALE = DSA_HEAD_DIM ** -0.5
IDX_HEADS = 8
IDX_DIM = 32
DSA_TOPK_MAX = 256
REL_BUCKETS = 32
REL_MAX_DIST = 128
MIX_IN_SIZES = (MLA_Q_RANK, MLA_KV_RANK, MLA_ROPE, DSA_WIDTH, DSA_WIDTH, DSA_WIDTH,
                IDX_HEADS * IDX_DIM, IDX_DIM, IDX_HEADS)
CONV_WIDTH = 3
N_EXPERTS = 8
TOP_K = 2

LANES = 128
VMEM_LIMIT_BYTES = 56 * 1024 * 1024

SEQ_TILE = 3 * LANES
PREP_TILE = 2 * SEQ_TILE
MOE_TILE = 512
EXPERT_TILE = 2 * MOE_TILE
QK_LOOKAHEAD = 2
HALF_BITS = 16
BITS_PER_EXIT_TEST = 8
LOG2E = math.log2(math.e)
NEG_BIG = -1e30
INT_MIN = -2 ** 31

_C_CQ = 0
_C_CKV = _C_CQ + MLA_Q_RANK
_C_KRM = _C_CKV + MLA_KV_RANK
_C_KRS = _C_KRM + LANES
_C_DQ = _C_KRS + LANES
_C_DK = _C_DQ + DSA_WIDTH
_C_DV = _C_DK + DSA_WIDTH
_C_IQ = _C_DV + DSA_WIDTH
_C_IK = _C_IQ + IDX_HEADS * IDX_DIM
_C_IW = _C_IK + LANES
_C_END = _C_IW + LANES


def _cparams(sem):
    return pltpu.CompilerParams(dimension_semantics=sem, vmem_limit_bytes=VMEM_LIMIT_BYTES)


def _row_tile(n_rows, candidates=(1024, 768, 512, 384, 256, 128)):
    for c in candidates:
        if n_rows % c == 0:
            return c
    raise ValueError(f"no row tile for {n_rows}")


def _rms(x, g):
    ms = jnp.mean(x * x, axis=-1, keepdims=True)
    return x * lax.rsqrt(ms + EPS) * g


def _dot(a, b):
    return jnp.dot(a, b, preferred_element_type=F32)


def _dot_nt(a, b):
    return lax.dot_general(a, b, (((1,), (1,)), ((), ())), preferred_element_type=F32)


def _prep_kernel(h_ref, g_ref, wext_ref, gql_ref, gkvl_ref, wqm_ref, wqs_ref, wkk_ref, wkv_ref,
                 gqm_ref, gqs_ref, gkm_ref, gks_ref, gdq_ref, gdk_ref, cos_ref, sin_ref,
                 qm_o, km_o, vm_o, qd_o, kd_o, vd_o, iq_o, ik_o, iw_o):
    xn = _rms(h_ref[...], g_ref[...]).astype(BF16)

    def proj(lo, hi):
        return _dot(xn, wext_ref[:, lo:hi])

    cos = cos_ref[...]
    sin = sin_ref[...]
    lane = lax.broadcasted_iota(I32, (xn.shape[0], LANES), 1)

    cqn = _rms(proj(_C_CQ, _C_CKV), gql_ref[...]).astype(BF16)
    q_main = _dot(cqn, wqm_ref[...])
    q_swap = _dot(cqn, wqs_ref[...])
    for hd in range(MLA_HEADS):
        sl = slice(hd * LANES, (hd + 1) * LANES)
        a = q_main[:, sl]
        r = lax.rsqrt(jnp.sum(a * a, axis=-1, keepdims=True) * (1.0 / MLA_QK) + EPS)
        out = (a * r * gqm_ref[...]) * cos + (q_swap[:, sl] * r * gqs_ref[...]) * sin
        qm_o[:, sl] = (out * (MLA_SCALE * LOG2E)).astype(BF16)

    ckvn = _rms(proj(_C_CKV, _C_KRM), gkvl_ref[...]).astype(BF16)
    k_nope = _dot(ckvn, wkk_ref[...])
    vm_o[...] = _dot(ckvn, wkv_ref[...]).astype(BF16)
    kr_main = proj(_C_KRM, _C_KRS)
    kr_swap = proj(_C_KRS, _C_DQ)
    for hd in range(MLA_HEADS):
        sl = slice(hd * LANES, (hd + 1) * LANES)
        a = k_nope[:, sl] + kr_main
        r = lax.rsqrt(jnp.sum(a * a, axis=-1, keepdims=True) * (1.0 / MLA_QK) + EPS)
        out = (a * r * gkm_ref[...]) * cos + (kr_swap * r * gks_ref[...]) * sin
        km_o[:, sl] = out.astype(BF16)

    first = lane < DSA_HEAD_DIM
    for (lo, g2_ref, o_ref, post) in ((_C_DQ, gdq_ref, qd_o, DSA_SCALE * LOG2E),
                                      (_C_DK, gdk_ref, kd_o, None)):
        for pr in range(DSA_HEADS // 2):
            x = proj(lo + pr * LANES, lo + (pr + 1) * LANES)
            sq = x * x
            s0 = jnp.sum(jnp.where(first, sq, 0.0), axis=-1, keepdims=True)
            s1 = jnp.sum(jnp.where(first, 0.0, sq), axis=-1, keepdims=True)
            r0 = lax.rsqrt(s0 * (1.0 / DSA_HEAD_DIM) + EPS)
            r1 = lax.rsqrt(s1 * (1.0 / DSA_HEAD_DIM) + EPS)
            out = x * jnp.where(first, r0, r1) * g2_ref[...]
            if post is None:
                o_ref[:, pr * LANES:(pr + 1) * LANES] = out.astype(BF16)
            else:
                out = out * post
                o_ref[:, (2 * pr) * LANES:(2 * pr + 1) * LANES] = jnp.where(first, out, 0.0).astype(BF16)
                o_ref[:, (2 * pr + 1) * LANES:(2 * pr + 2) * LANES] = jnp.where(first, 0.0, out).astype(BF16)

    vd_o[...] = proj(_C_DV, _C_IQ).astype(BF16)
    lane_group = lax.shift_right_logical(lane, int(math.log2(IDX_DIM)))
    for quad in range(IDX_HEADS * IDX_DIM // LANES):
        x = proj(_C_IQ + quad * LANES, _C_IQ + (quad + 1) * LANES)
        for j in range(LANES // IDX_DIM):
            hd = quad * (LANES // IDX_DIM) + j
            iq_o[:, hd * LANES:(hd + 1) * LANES] = jnp.where(lane_group == j, x, 0.0).astype(BF16)
    ik_o[...] = proj(_C_IK, _C_IW).astype(BF16)
    iw_o[...] = proj(_C_IW, _C_END)


def _prep_call(h, g_mix, wext, gql, gkvl, wqm, wqs, wkk, wkv, gqm, gqs, gkm, gks, gdq, gdk,
               cos_t, sin_t, lp):
    t = h.shape[0]
    tm = PREP_TILE
    assert t % tm == 0
    reps = tm // math.gcd(lp, tm)
    nt = reps * lp // tm
    cos_t, sin_t = jnp.tile(cos_t, (reps, 1)), jnp.tile(sin_t, (reps, 1))
    row = lambda w: pl.BlockSpec((tm, w), lambda i: (i, 0))
    full = lambda a: pl.BlockSpec(a.shape, lambda i: (0, 0), pipeline_mode=pl.Buffered(1))
    tab = pl.BlockSpec((tm, LANES), lambda i: (i % nt, 0))
    hw = MLA_HEADS * LANES
    out_shape = [
        jax.ShapeDtypeStruct((t, hw), BF16), jax.ShapeDtypeStruct((t, hw), BF16),
        jax.ShapeDtypeStruct((t, MLA_HEADS * MLA_V), BF16),
        jax.ShapeDtypeStruct((t, DSA_HEADS * LANES), BF16), jax.ShapeDtypeStruct((t, DSA_WIDTH), BF16),
        jax.ShapeDtypeStruct((t, DSA_WIDTH), BF16),
        jax.ShapeDtypeStruct((t, IDX_HEADS * LANES), BF16),
        jax.ShapeDtypeStruct((t, LANES), BF16), jax.ShapeDtypeStruct((t, LANES), F32),
    ]
    return pl.pallas_call(
        _prep_kernel,
        grid=(t // tm,),
        in_specs=[row(D_MODEL), full(g_mix), full(wext), full(gql), full(gkvl), full(wqm),
                  full(wqs), full(wkk), full(wkv), full(gqm), full(gqs), full(gkm), full(gks),
                  full(gdq), full(gdk), tab, tab],
        out_specs=[row(s.shape[1]) for s in out_shape],
        out_shape=out_shape,
        compiler_params=_cparams(("parallel",)),
        name="prep_mix_in",
    )(h, g_mix, wext, gql, gkvl, wqm, wqs, wkk, wkv, gqm, gqs, gkm, gks, gdq, gdk, cos_t, sin_t)


def _flash_init(m_scr, l_scr, acc_scr):
    m_scr[...] = jnp.full(m_scr.shape, -jnp.inf, F32)
    l_scr[...] = jnp.zeros(l_scr.shape, F32)
    acc_scr[...] = jnp.zeros(acc_scr.shape, F32)


def _lane_fold(x, op):
    return functools.reduce(op, [x[:, j * LANES:(j + 1) * LANES] for j in range(x.shape[1] // LANES)])


def _flash_update(hd, s, vc, m_scr, l_scr, acc_scr):
    m_old = m_scr[hd]
    m_new = jnp.maximum(m_old, jnp.max(_lane_fold(s, jnp.maximum), axis=-1, keepdims=True))
    alpha = jnp.exp2(m_old - m_new)
    p = jnp.exp2(s - jnp.concatenate([m_new] * (s.shape[1] // LANES), axis=1))
    m_scr[hd] = m_new
    l_scr[hd] = alpha * l_scr[hd] + _lane_fold(p, jnp.add)
    acc_scr[hd] = alpha * acc_scr[hd] + _dot(p.astype(BF16), vc)


def _flash_store(o_ref, l_scr, acc_scr, head_dim):
    lane = lax.broadcasted_iota(I32, acc_scr.shape[1:], 1)
    for pr in range(acc_scr.shape[0] // 2):
        o0, o1 = [acc_scr[hd] / jnp.sum(l_scr[hd], axis=-1, keepdims=True)
                  for hd in (2 * pr, 2 * pr + 1)]
        o_ref[:, pr * LANES:(pr + 1) * LANES] = jnp.where(lane < head_dim, o0, o1).astype(o_ref.dtype)


def _mla_kernel(q_ref, k_ref, v_ref, o_ref, m_scr, l_scr, acc_scr):
    qi = pl.program_id(1)
    tq = q_ref.shape[0]
    _flash_init(m_scr, l_scr, acc_scr)

    def chunk(c, diagonal):
        ks = pl.ds(pl.multiple_of(c * tq, tq), tq)

        def scores(hd):
            sl = slice(hd * LANES, (hd + 1) * LANES)
            return _dot_nt(q_ref[:, sl], k_ref[ks, sl])

        pending = [scores(hd) for hd in range(QK_LOOKAHEAD)]
        for hd in range(MLA_HEADS):
            if hd + QK_LOOKAHEAD < MLA_HEADS:
                pending.append(scores(hd + QK_LOOKAHEAD))
            s = pending.pop(0)
            if diagonal:
                row = lax.broadcasted_iota(I32, (tq, tq), 0)
                col = lax.broadcasted_iota(I32, (tq, tq), 1)
                s = jnp.where(col <= row, s, NEG_BIG)
            pr = hd // 2
            _flash_update(hd, s, v_ref[ks, pr * LANES:(pr + 1) * LANES], m_scr, l_scr, acc_scr)

    def body(c, carry):
        chunk(c, False)
        return carry

    lax.fori_loop(0, qi, body, 0)
    chunk(qi, True)
    _flash_store(o_ref, l_scr, acc_scr, MLA_V)


def _mla_call(q, k, v):
    b, lp, _ = q.shape
    tq = SEQ_TILE
    nq = lp // tq
    state = pltpu.VMEM((MLA_HEADS, tq, LANES), F32)
    return pl.pallas_call(
        _mla_kernel,
        grid=(b, nq),
        in_specs=[pl.BlockSpec((None, tq, q.shape[2]), lambda bi, i: (bi, i, 0)),
                  pl.BlockSpec((None, lp, k.shape[2]), lambda bi, i: (bi, 0, 0)),
                  pl.BlockSpec((None, lp, v.shape[2]), lambda bi, i: (bi, 0, 0))],
        out_specs=pl.BlockSpec((None, tq, v.shape[2]), lambda bi, i: (bi, i, 0)),
        out_shape=jax.ShapeDtypeStruct((b, lp, v.shape[2]), BF16),
        scratch_shapes=[state, state, state],
        compiler_params=_cparams(("parallel", "arbitrary")),
        name="mla_attention",
    )(q, k, v)


def _dsa_kernel(iq_ref, iw_ref, ik_ref, q_ref, k_ref, v_ref, tz_ref, o_ref,
                key_scr, cap_scr, hi_scr, lo_scr, wt_scr, ans_scr, jst_scr, m_scr, l_scr, acc_scr,
                *, top_k):
    qi = pl.program_id(1)
    tq = q_ref.shape[0]
    nsub = tq // LANES
    sublanes = 8
    n_chunks = qi + 1
    key_i = lax.broadcasted_iota(I32, (tq, tq), 0)
    qry_i = lax.broadcasted_iota(I32, (tq, tq), 1)
    chunk_rows = lambda c: pl.ds(pl.multiple_of(c * tq, tq), tq)
    head_slot = lambda hd: slice(hd * LANES, (hd + 1) * LANES)
    wt_scr[...] = jnp.transpose(iw_ref[...])[0:IDX_HEADS, :]

    def index_chunk(c, diagonal):
        ikc = ik_ref[chunk_rows(c), :]
        sc = jnp.zeros((tq, tq), F32)
        for hd in range(IDX_HEADS):
            act = jnp.maximum(_dot_nt(ikc, iq_ref[:, head_slot(hd)]), 0.0)
            sc = sc + wt_scr[hd:hd + 1, :] * act
        sc = jnp.where(sc == 0.0, 0.0, sc)
        if diagonal:
            sc = jnp.where(key_i <= qry_i, sc, -jnp.inf)
        bits = pltpu.bitcast(sc, I32)
        key = bits ^ (lax.shift_right_arithmetic(bits, 31) & 0x7FFFFFFF)
        key_scr[c] = key
        hi_scr[c] = lax.shift_right_arithmetic(key, HALF_BITS)

    def index_body(c, carry):
        index_chunk(c, False)
        return carry

    lax.fori_loop(0, qi, index_body, 0)
    index_chunk(qi, True)

    kf = float(top_k)
    vec = (sublanes, tq)
    sub_i = lax.broadcasted_iota(I32, vec, 0)

    def count(pred):
        def body(c, accs):
            accs = list(accs)
            kk = key_scr[c]
            for g in range(tq // sublanes):
                hit = jnp.where(pred(kk[g * sublanes:(g + 1) * sublanes, :], c * tq + g * sublanes),
                                1.0, 0.0)
                accs[g % 2] = accs[g % 2] + hit
            return tuple(accs)
        zero = jnp.zeros(vec, F32)
        a0, a1 = lax.fori_loop(0, n_chunks, body, (zero, zero))
        return jnp.broadcast_to(jnp.sum(a0 + a1, axis=0, keepdims=True), vec)

    def count_half(half_scr, cand):
        def body(c, accs):
            accs = list(accs)
            for g in range(tq // sublanes):
                kk = half_scr[c, g * sublanes:(g + 1) * sublanes, :]
                accs[g % 2] = accs[g % 2] + lax.shift_right_arithmetic(kk - cand, 31)
            return tuple(accs)
        zero = jnp.zeros(vec, I32)
        a0, a1 = lax.fori_loop(0, n_chunks, body, (zero, zero))
        below = jnp.sum(a0 + a1, axis=0, keepdims=True)
        return jnp.broadcast_to((n_chunks * tq + below).astype(F32), vec)

    def n_open_of(done):
        return jnp.sum(jnp.where(done == 0, 1.0, 0.0))

    def bisect(count_ge, ans, done):
        def step(bit, ans, done):
            cand = ans + lax.shift_left(jnp.int32(1), bit)
            cnt = count_ge(cand)
            open_ = done == 0
            ans = jnp.where(jnp.logical_and(open_, cnt >= kf), cand, ans)
            done = jnp.where(jnp.logical_and(open_, cnt == kf), 1, done)
            return ans, done

        def cond(st):
            bit, _, _, n_open = st
            return jnp.logical_and(bit >= 0, n_open > 0.0)

        def body(st):
            bit, ans, done, _ = st
            for k in range(BITS_PER_EXIT_TEST):
                ans, done = step(bit - k, ans, done)
            return bit - BITS_PER_EXIT_TEST, ans, done, n_open_of(done)

        _, ans, done, _ = lax.while_loop(cond, body,
                                         (jnp.int32(HALF_BITS - 1), ans, done, n_open_of(done)))
        return ans, done

    t_pos = qi * tq + lax.broadcasted_iota(I32, vec, 1)
    done0 = (t_pos + 1 <= top_k).astype(I32)
    half_min = -(1 << (HALF_BITS - 1))

    hi_ans, done = bisect(lambda cand: count_half(hi_scr, cand), jnp.full(vec, half_min, I32), done0)

    hi_row = hi_ans[0:1, :]

    def low_chunk(c, carry):
        key = key_scr[c]
        inside = lax.shift_right_arithmetic(key, HALF_BITS) == hi_row
        lo_scr[c] = jnp.where(inside, key & ((1 << HALF_BITS) - 1), -1)
        return carry

    lax.fori_loop(0, n_chunks, low_chunk, 0)
    above = count_half(hi_scr, hi_ans + 1)
    lo_ans, done = bisect(lambda cand: above + count_half(lo_scr, cand), jnp.zeros(vec, I32), done)
    ans = lax.shift_left(hi_ans, HALF_BITS) + lo_ans
    n_open = n_open_of(done)
    ans_scr[...] = ans
    jst_scr[...] = jnp.full(vec, 2 ** 31 - 1, I32)

    @pl.when(n_open > 0.0)
    def _():
        need = kf - count(lambda kk, _i: kk > ans)

        def tie_body(i, jst):
            cand = jst + lax.shift_left(jnp.int32(1), 13 - i)
            cnt = count(lambda kk, i0: jnp.logical_and(kk == ans, sub_i + i0 < cand))
            return jnp.where(cnt < need, cand, jst)

        jst = lax.fori_loop(0, 14, tie_body, jnp.zeros(vec, I32))
        jst_scr[...] = jnp.where(done == 0, jst, 2 ** 31 - 1)

    ans_row = ans_scr[0:1, :]
    jst_row = jst_scr[0:1, :]

    def mask_chunk(c, diagonal):
        kk = key_scr[c]
        sel = jnp.logical_or(kk > ans_row,
                             jnp.logical_and(kk == ans_row, key_i + c * tq <= jst_row))
        if diagonal:
            sel = jnp.logical_and(sel, key_i <= qry_i)
        cap_scr[c] = jnp.transpose(jnp.where(sel, -NEG_BIG, NEG_BIG))

    def mask_body(c, carry):
        mask_chunk(c, False)
        return carry

    lax.fori_loop(0, qi, mask_body, 0)
    mask_chunk(qi, True)

    _flash_init(m_scr, l_scr, acc_scr)

    def bias_of(hd, where_):
        near = tz_ref[hd, :, LANES:2 * LANES]
        far = tz_ref[hd, :, 0:LANES]
        z = jnp.zeros((LANES, LANES), F32)
        if where_ == "previous":
            blocks = [[far if (a == 0 and b == nsub - 1) else z for b in range(nsub)]
                      for a in range(nsub)]
        else:
            blocks = [[near if b == a else far if b == a - 1 else z for b in range(nsub)]
                      for a in range(nsub)]
        return jnp.concatenate([jnp.concatenate(r, axis=1) for r in blocks], axis=0)

    def chunk(c, where_):
        ks = chunk_rows(c)
        cap = cap_scr[c]
        pair = lambda hd: slice((hd // 2) * LANES, (hd // 2 + 1) * LANES)
        scores = lambda hd: _dot_nt(q_ref[:, head_slot(hd)], k_ref[ks, pair(hd)])

        pending = [scores(hd) for hd in range(QK_LOOKAHEAD)]
        for hd in range(DSA_HEADS):
            if hd + QK_LOOKAHEAD < DSA_HEADS:
                pending.append(scores(hd + QK_LOOKAHEAD))
            s = jnp.minimum(pending.pop(0), cap)
            if where_ is not None:
                s = s + bias_of(hd, where_)
            _flash_update(hd, s, v_ref[ks, pair(hd)], m_scr, l_scr, acc_scr)

    def body(c, carry):
        chunk(c, None)
        return carry

    lax.fori_loop(0, jnp.maximum(qi - 1, 0), body, 0)

    @pl.when(qi >= 1)
    def _():
        chunk(qi - 1, "previous")

    chunk(qi, "diagonal")
    _flash_store(o_ref, l_scr, acc_scr, DSA_HEAD_DIM)


def _dsa_call(iq, iw, ik, q, k, v, tz, top_k):
    b, lp, _ = q.shape
    tq = SEQ_TILE
    nq = lp // tq
    qspec = lambda w: pl.BlockSpec((None, tq, w), lambda bi, i: (bi, i, 0))
    kspec = lambda w: pl.BlockSpec((None, lp, w), lambda bi, i: (bi, 0, 0),
                                   pipeline_mode=pl.Buffered(1))
    state = pltpu.VMEM((DSA_HEADS, tq, LANES), F32)
    return pl.pallas_call(
        functools.partial(_dsa_kernel, top_k=top_k),
        grid=(b, nq),
        in_specs=[qspec(iq.shape[2]), qspec(LANES), kspec(LANES), qspec(q.shape[2]),
                  kspec(DSA_WIDTH), kspec(DSA_WIDTH),
                  pl.BlockSpec(tz.shape, lambda bi, i: (0, 0, 0))],
        out_specs=qspec(DSA_WIDTH),
        out_shape=jax.ShapeDtypeStruct((b, lp, DSA_WIDTH), BF16),
        scratch_shapes=[pltpu.VMEM((nq, tq, tq), I32), pltpu.VMEM((nq, tq, tq), F32),
                        pltpu.VMEM((nq, tq, tq), I32), pltpu.VMEM((nq, tq, tq), I32),
                        pltpu.VMEM((IDX_HEADS, tq), F32),
                        pltpu.VMEM((8, tq), I32), pltpu.VMEM((8, tq), I32),
                        state, state, state],
        compiler_params=_cparams(("parallel", "arbitrary")),
        name="dsa_attention",
    )(iq, iw, ik, q, k, v, tz)


def _swiglu_chunk(xb, w1_ref, w3_ref, w2_ref):
    a = _dot(xb, w1_ref[...].astype(BF16))
    act = (a * jax.nn.sigmoid(a)) * _dot(xb, w3_ref[...].astype(BF16))
    return _dot(act.astype(BF16), w2_ref[...].astype(BF16))


def _mix_out_ffn_kernel(h_ref, a_ref, b_ref, wa_ref, wb_ref, g_ref, w1_ref, w3_ref, w2_ref, o_ref,
                        h1_scr, xn_scr, acc_scr):
    f = pl.program_id(1)

    @pl.when(f == 0)
    def _():
        h1 = h_ref[...] + _dot(a_ref[...], wa_ref[...]) + _dot(b_ref[...], wb_ref[...])
        h1_scr[...] = h1
        xn_scr[...] = _rms(h1, g_ref[...]).astype(BF16)
        acc_scr[...] = jnp.zeros(acc_scr.shape, F32)

    acc_scr[...] += _swiglu_chunk(xn_scr[...], w1_ref, w3_ref, w2_ref)

    @pl.when(f == pl.num_programs(1) - 1)
    def _():
        o_ref[...] = h1_scr[...] + acc_scr[...]


def _mix_out_ffn_call(h, a, b, wa, wb, g, w1, w3, w2):
    t = h.shape[0]
    d_ff = w1.shape[1]
    tm = _row_tile(t)
    tf = 512 if d_ff % 512 == 0 else d_ff
    row = lambda w: pl.BlockSpec((tm, w), lambda i, f: (i, 0))
    full = lambda x: pl.BlockSpec(x.shape, lambda i, f: (0, 0), pipeline_mode=pl.Buffered(1))
    return pl.pallas_call(
        _mix_out_ffn_kernel,
        grid=(t // tm, d_ff // tf),
        in_specs=[row(D_MODEL), row(a.shape[1]), row(b.shape[1]), full(wa), full(wb), full(g),
                  pl.BlockSpec((D_MODEL, tf), lambda i, f: (0, f)),
                  pl.BlockSpec((D_MODEL, tf), lambda i, f: (0, f)),
                  pl.BlockSpec((tf, D_MODEL), lambda i, f: (f, 0))],
        out_specs=row(D_MODEL),
        out_shape=jax.ShapeDtypeStruct(h.shape, F32),
        scratch_shapes=[pltpu.VMEM((tm, D_MODEL), F32), pltpu.VMEM((tm, D_MODEL), BF16),
                        pltpu.VMEM((tm, D_MODEL), F32)],
        compiler_params=_cparams(("parallel", "arbitrary")),
        name="mix_out_swiglu",
    )(h, a, b, wa, wb, g, w1, w3, w2)


def _conv_kernel(h_ref, g_ref, win_ref, cw_ref, wout_ref, o_ref, z_scr):
    i = pl.program_id(1)
    tm = h_ref.shape[0]
    halo = 8
    x = h_ref[...]
    xn = _rms(x, g_ref[...]).astype(BF16)
    c_gate = _dot(xn, win_ref[:, D_MODEL:2 * D_MODEL])
    u = _dot(xn, win_ref[:, 2 * D_MODEL:3 * D_MODEL])
    z = c_gate * u

    @pl.when(i == 0)
    def _():
        z_scr[0:halo, :] = jnp.zeros((halo, D_MODEL), F32)

    @pl.when(i > 0)
    def _():
        z_scr[0:halo, :] = z_scr[tm:tm + halo, :]

    z_scr[halo:halo + tm, :] = z
    cw = cw_ref[...]
    y = (cw[0:1, :] * z_scr[halo - 2:halo - 2 + tm, :]
         + cw[1:2, :] * z_scr[halo - 1:halo - 1 + tm, :]
         + cw[2:3, :] * z)
    b_gate = _dot(xn, win_ref[:, 0:D_MODEL])
    o_ref[...] = x + _dot((b_gate * y).astype(BF16), wout_ref[...])


def _conv_call(h3, g, w_in, cw, w_out):
    b, lp, d = h3.shape
    tm = SEQ_TILE
    full = lambda a: pl.BlockSpec(a.shape, lambda bi, i: (0,) * a.ndim)
    return pl.pallas_call(
        _conv_kernel,
        grid=(b, lp // tm),
        in_specs=[pl.BlockSpec((None, tm, d), lambda bi, i: (bi, i, 0)), full(g), full(w_in),
                  full(cw), full(w_out)],
        out_specs=pl.BlockSpec((None, tm, d), lambda bi, i: (bi, i, 0)),
        out_shape=jax.ShapeDtypeStruct(h3.shape, F32),
        scratch_shapes=[pltpu.VMEM((tm + 8, d), F32)],
        compiler_params=_cparams(("arbitrary", "arbitrary")),
        name="short_conv_mixer",
    )(h3, g, w_in, cw, w_out)


def _as_tiles(x):
    return x.reshape(x.shape[0], D_MODEL // LANES, LANES)


def _as_rows(x3):
    return x3.reshape(x3.shape[0], D_MODEL)


def _route_kernel(h_ref, g_ref, wr_ref, xg_ref, didx_ref, gate_ref, cnt_ref,
                  xs_scr, base_scr, dvm_scr, dsm_scr, sem_idx, sem_rows, *, stride):
    i = pl.program_id(0)
    tm = h_ref.shape[0]
    lane = lax.broadcasted_iota(I32, (tm, LANES), 1)
    lane_f = lane.astype(F32)

    @pl.when(i == 0)
    def _():
        base_scr[...] = jnp.zeros(base_scr.shape, F32)

    xnf = _rms(h_ref[...], g_ref[...])
    logits = jnp.where(lane < N_EXPERTS, _dot(xnf.astype(BF16), wr_ref[...]), -jnp.inf)
    v1 = jnp.max(logits, axis=-1, keepdims=True)
    i1 = jnp.min(jnp.where(logits == v1, lane_f, float(LANES)), axis=-1, keepdims=True)
    rest = jnp.where(lane_f == i1, -jnp.inf, logits)
    v2 = jnp.max(rest, axis=-1, keepdims=True)
    i2 = jnp.min(jnp.where(rest == v2, lane_f, float(LANES)), axis=-1, keepdims=True)
    e2 = jnp.exp(v2 - v1)
    den = 1.0 + e2
    gates = (1.0 / den, e2 / den)

    oh1 = jnp.where(lane_f == i1, 1.0, 0.0)
    oh2 = jnp.where(lane_f == i2, 1.0, 0.0)
    oh = oh1 + oh2
    earlier = (lax.broadcasted_iota(I32, (tm, tm), 1) < lax.broadcasted_iota(I32, (tm, tm), 0))
    prefix = _dot(jnp.where(earlier, 1.0, 0.0).astype(BF16), oh.astype(BF16))
    pos = base_scr[0:1, :] + prefix
    d1 = jnp.sum(oh1 * pos, axis=-1, keepdims=True) + i1 * float(stride)
    d2 = jnp.sum(oh2 * pos, axis=-1, keepdims=True) + i2 * float(stride)
    base_scr[...] = base_scr[...] + jnp.sum(oh, axis=0, keepdims=True)
    cnt_ref[...] = base_scr[...]

    gate_ref[...] = jnp.where(lane == 0, gates[0], jnp.where(lane == 1, gates[1], 0.0))
    slot = lax.rem(i, 2)
    xs_scr[slot] = _as_tiles(xnf)

    dmat = jnp.where(lane == 0, d1, jnp.where(lane == 1, d2, 0.0))
    dvm_scr[...] = jnp.transpose(dmat)[0:8, :].astype(I32)
    didx_ref[...] = dvm_scr[...]
    cp = pltpu.make_async_copy(dvm_scr, dsm_scr, sem_idx)
    cp.start()
    cp.wait()

    for r in range(tm):
        for s in range(TOP_K):
            pltpu.make_async_copy(xs_scr.at[slot, r], xg_ref.at[dsm_scr[s, r]],
                                  sem_rows.at[slot]).start(priority=s)

    def wait_rows(sl):
        for s in range(TOP_K):
            pltpu.make_async_copy(xs_scr.at[sl], xg_ref.at[pl.ds(0, tm)], sem_rows.at[sl]).wait()

    @pl.when(i > 0)
    def _():
        wait_rows(1 - slot)

    @pl.when(i == pl.num_programs(0) - 1)
    def _():
        wait_rows(slot)
        xs_scr[0] = jnp.zeros(xs_scr.shape[1:], F32)
        dvm_scr[:, 0:LANES] = base_scr[...].astype(I32)
        cp2 = pltpu.make_async_copy(dvm_scr, dsm_scr, sem_idx)
        cp2.start()
        cp2.wait()
        for phase in ("start", "wait"):
            for e in range(N_EXPERTS):
                for k in range(EXPERT_TILE // tm):
                    blk = pltpu.make_async_copy(
                        xs_scr.at[0], xg_ref.at[pl.ds(e * stride + dsm_scr[0, e] + k * tm, tm)],
                        sem_rows.at[0])
                    blk.start() if phase == "start" else blk.wait()


def _route_call(h, g, wr, stride):
    t = h.shape[0]
    tm = MOE_TILE
    return pl.pallas_call(
        functools.partial(_route_kernel, stride=stride),
        grid=(t // tm,),
        in_specs=[pl.BlockSpec((tm, D_MODEL), lambda i: (i, 0)),
                  pl.BlockSpec(g.shape, lambda i: (0, 0)),
                  pl.BlockSpec(wr.shape, lambda i: (0, 0))],
        out_specs=[pl.BlockSpec(memory_space=pl.ANY),
                   pl.BlockSpec((8, tm), lambda i: (0, i)),
                   pl.BlockSpec((tm, LANES), lambda i: (i, 0)),
                   pl.BlockSpec((8, LANES), lambda i: (0, 0))],
        out_shape=[jax.ShapeDtypeStruct((N_EXPERTS * stride, D_MODEL // LANES, LANES), F32),
                   jax.ShapeDtypeStruct((8, t), I32),
                   jax.ShapeDtypeStruct((t, LANES), F32),
                   jax.ShapeDtypeStruct((8, LANES), F32)],
        scratch_shapes=[pltpu.VMEM((2, tm, D_MODEL // LANES, LANES), F32), pltpu.VMEM((8, LANES), F32),
                        pltpu.VMEM((8, tm), I32), pltpu.SMEM((8, tm), I32),
                        pltpu.SemaphoreType.DMA, pltpu.SemaphoreType.DMA((2,))],
        compiler_params=_cparams(("arbitrary",)),
        name="moe_route",
    )(h, g, wr)


def _expert_kernel(blk_ref, exp_ref, nu_ref, x_ref, w1_ref, w3_ref, w2_ref, o_ref, xb_scr, acc_scr):
    j = pl.program_id(0)
    f = pl.program_id(1)

    @pl.when(j < nu_ref[0])
    def _():
        @pl.when(f == 0)
        def _():
            xb_scr[...] = _as_rows(x_ref[...]).astype(BF16)
            acc_scr[...] = jnp.zeros(acc_scr.shape, F32)

        acc_scr[...] += _swiglu_chunk(xb_scr[...], w1_ref, w3_ref, w2_ref)

        @pl.when(f == pl.num_programs(1) - 1)
        def _():
            o_ref[...] = _as_tiles(acc_scr[...])


def _expert_call(tile_blk, tile_exp, n_used, xg, w1, w3, w2):
    d_ff = w1.shape[2]
    tm = EXPERT_TILE
    tf = 512 if d_ff % 512 == 0 else d_ff
    nf = d_ff // tf
    n_tiles = tile_blk.shape[0]
    fe = lambda j, f, nu: jnp.where(j < nu[0], f, nf - 1)
    tile_block = pl.BlockSpec((tm, D_MODEL // LANES, LANES), lambda j, f, blk, ex, nu: (blk[j], 0, 0))
    grid_spec = pltpu.PrefetchScalarGridSpec(
        num_scalar_prefetch=3,
        grid=(n_tiles, nf),
        in_specs=[tile_block,
                  pl.BlockSpec((None, D_MODEL, tf), lambda j, f, blk, ex, nu: (ex[j], 0, fe(j, f, nu))),
                  pl.BlockSpec((None, D_MODEL, tf), lambda j, f, blk, ex, nu: (ex[j], 0, fe(j, f, nu))),
                  pl.BlockSpec((None, tf, D_MODEL), lambda j, f, blk, ex, nu: (ex[j], fe(j, f, nu), 0))],
        out_specs=tile_block,
        scratch_shapes=[pltpu.VMEM((tm, D_MODEL), BF16), pltpu.VMEM((tm, D_MODEL), F32)])
    return pl.pallas_call(
        _expert_kernel,
        grid_spec=grid_spec,
        out_shape=jax.ShapeDtypeStruct(xg.shape, F32),
        compiler_params=_cparams(("arbitrary", "arbitrary")),
        name="expert_swiglu",
    )(tile_blk, tile_exp, n_used, xg, w1, w3, w2)


def _combine_kernel(h_ref, gate_ref, didx_ref, yg_ref, o_ref, y_scr, dsm_scr, sem_idx, sem_rows):
    i = pl.program_id(0)
    tm = h_ref.shape[0]
    slot = lax.rem(i, 2)

    def fetch(j, sl):
        cp = pltpu.make_async_copy(didx_ref.at[:, pl.ds(pl.multiple_of(j * tm, tm), tm)], dsm_scr,
                                   sem_idx)
        cp.start()
        cp.wait()
        for r in range(tm):
            for s in range(TOP_K):
                pltpu.make_async_copy(yg_ref.at[dsm_scr[s, r]], y_scr.at[sl, s, r],
                                      sem_rows.at[sl]).start(priority=s)

    @pl.when(i == 0)
    def _():
        fetch(0, 0)

    @pl.when(i + 1 < pl.num_programs(0))
    def _():
        fetch(i + 1, 1 - slot)

    for s in range(TOP_K):
        pltpu.make_async_copy(yg_ref.at[pl.ds(0, tm)], y_scr.at[slot, s], sem_rows.at[slot]).wait()
    gate = gate_ref[...]
    o_ref[...] = (h_ref[...] + gate[:, 0:1] * _as_rows(y_scr[slot, 0])
                  + gate[:, 1:2] * _as_rows(y_scr[slot, 1]))


def _combine_call(h, gate, didx, yg, rows=None):
    t = h.shape[0]
    tm = MOE_TILE
    n_seq, seq_stride, first, n = rows if rows is not None else (1, 0, 0, t)
    assert n % tm == 0 and first % 8 == 0 and seq_stride % 8 == 0
    tiles = n // tm
    if rows is not None:
        didx = didx.reshape(8, n_seq, seq_stride)[:, :, first:first + n].reshape(8, n_seq * n)

    def token_rows(width):
        start = lambda i: pl.multiple_of((i // tiles) * seq_stride + first + (i % tiles) * tm, 8)
        return pl.BlockSpec((pl.Element(tm), pl.Element(width)), lambda i: (start(i), 0))

    return pl.pallas_call(
        _combine_kernel,
        grid=(n_seq * tiles,),
        in_specs=[token_rows(D_MODEL), token_rows(LANES),
                  pl.BlockSpec(memory_space=pl.ANY), pl.BlockSpec(memory_space=pl.ANY)],
        out_specs=pl.BlockSpec((tm, D_MODEL), lambda i: (i, 0)),
        out_shape=jax.ShapeDtypeStruct((n_seq * n, D_MODEL), F32),
        scratch_shapes=[pltpu.VMEM((2, TOP_K, tm, D_MODEL // LANES, LANES), F32),
                        pltpu.SMEM((8, tm), I32),
                        pltpu.SemaphoreType.DMA, pltpu.SemaphoreType.DMA((2,))],
        compiler_params=_cparams(("arbitrary",)),
        name="moe_combine",
    )(h, gate, didx, yg)


def _moe_call(h, g, wr, w1, w3, w2, rows=None):
    t = h.shape[0]
    tm = EXPERT_TILE
    assert t % MOE_TILE == 0 and tm % MOE_TILE == 0
    stride = -(-t // tm) * tm + tm
    xg, didx, gate, cnt = _route_call(h, g, wr, stride)
    counts = cnt[0, :N_EXPERTS].astype(I32)
    tiles_e = (counts + tm - 1) // tm
    cum = jnp.cumsum(tiles_e)
    n_used = cum[-1]
    n_tiles = -(-TOP_K * t // tm) + N_EXPERTS
    jj = jnp.minimum(jnp.arange(n_tiles, dtype=I32), n_used - 1)
    tile_exp = jnp.sum((jj[:, None] >= cum[None, :]).astype(I32), axis=1)
    tile_blk = tile_exp * (stride // tm) + jj - (cum - tiles_e)[tile_exp]
    yg = _expert_call(tile_blk, tile_exp, n_used[None], xg, w1, w3, w2)
    return _combine_call(h, gate, didx, yg, rows)


def _mix_in_weights(w):
    offs = np.concatenate([[0], np.cumsum(MIX_IN_SIZES)])
    cq, ckv, kr, dq, dk, dv, iq, ik, iw = [w[:, offs[j]:offs[j + 1]] for j in range(9)]
    z = lambda n: jnp.zeros((w.shape[0], n), w.dtype)
    half = MLA_ROPE // 2
    kr_main = jnp.concatenate([z(MLA_NOPE), kr, z(LANES - MLA_QK)], axis=1)
    kr_swap = jnp.concatenate([z(MLA_NOPE), kr[:, half:], kr[:, :half], z(LANES - MLA_QK)], axis=1)
    ik4 = jnp.concatenate([ik] * (LANES // IDX_DIM), axis=1)
    iw_p = jnp.concatenate([iw, z(LANES - IDX_HEADS)], axis=1)
    return jnp.concatenate([cq, ckv, kr_main, kr_swap, dq, dk, dv, iq, ik4, iw_p],
                           axis=1).astype(BF16)


def _mla_q_weights(w_uq):
    r = w_uq.shape[0]
    w = w_uq.reshape(r, MLA_HEADS, MLA_QK)
    nope, rope = w[..., :MLA_NOPE], w[..., MLA_NOPE:]
    half = MLA_ROPE // 2
    z = lambda n: jnp.zeros((r, MLA_HEADS, n), w.dtype)
    main = jnp.concatenate([nope, rope, z(LANES - MLA_QK)], axis=-1)
    swap = jnp.concatenate([z(MLA_NOPE), rope[..., half:], rope[..., :half], z(LANES - MLA_QK)],
                           axis=-1)
    return (main.reshape(r, MLA_HEADS * LANES).astype(BF16),
            swap.reshape(r, MLA_HEADS * LANES).astype(BF16))


def _mla_kv_weights(w_ukv):
    r = w_ukv.shape[0]
    w = w_ukv.reshape(r, MLA_HEADS, MLA_NOPE + MLA_V)
    k_nope = jnp.concatenate([w[..., :MLA_NOPE], jnp.zeros((r, MLA_HEADS, LANES - MLA_NOPE), w.dtype)],
                             axis=-1)
    return (k_nope.reshape(r, MLA_HEADS * LANES).astype(BF16),
            w[..., MLA_NOPE:].reshape(r, MLA_HEADS * MLA_V).astype(BF16))


def _qk_gains(g):
    half = MLA_ROPE // 2
    z = lambda n: jnp.zeros((n,), g.dtype)
    main = jnp.concatenate([g, z(LANES - MLA_QK)])
    swap = jnp.concatenate([z(MLA_NOPE), g[MLA_NOPE + half:], g[MLA_NOPE:MLA_NOPE + half],
                            z(LANES - MLA_QK)])
    return main[None, :], swap[None, :]


def _rope_tables(lp):
    half = MLA_ROPE // 2
    inv = ROPE_BASE ** (-jnp.arange(half, dtype=F32) / half)
    ang = jnp.arange(lp, dtype=jnp.int32).astype(F32)[:, None] * inv[None, :]
    cos, sin = jnp.cos(ang), jnp.sin(ang)
    ones = jnp.ones((lp, MLA_NOPE), F32)
    pad1 = jnp.ones((lp, LANES - MLA_QK), F32)
    zeros = jnp.zeros((lp, MLA_NOPE), F32)
    pad0 = jnp.zeros((lp, LANES - MLA_QK), F32)
    return (jnp.concatenate([ones, cos, cos, pad1], axis=1),
            jnp.concatenate([zeros, -sin, sin, pad0], axis=1))


def _rel_buckets(n):
    max_exact = REL_BUCKETS // 2
    d = np.arange(n)
    df = np.maximum(d, 1).astype(np.float32)
    large = max_exact + (np.log(df / np.float32(max_exact))
                         / np.float32(math.log(REL_MAX_DIST / max_exact))
                         * np.float32(REL_BUCKETS - max_exact)).astype(np.int32)
    large = np.minimum(large, REL_BUCKETS - 1)
    return np.where(d < max_exact, d, large)


def _bias_tiles(rel_bias):
    buckets = _rel_buckets(2 * LANES)
    assert np.all(buckets[LANES - 1:] == REL_BUCKETS - 1)
    n_heads = rel_bias.shape[1]
    shifted = (rel_bias - rel_bias[REL_BUCKETS - 1:REL_BUCKETS, :]) * LOG2E
    period = 3 * LANES
    w = jnp.concatenate([jnp.take(shifted, jnp.asarray(buckets), axis=0),
                         jnp.zeros((period - 2 * LANES, n_heads), F32)], axis=0)
    u = jnp.roll(jnp.flip(w, axis=0), -(2 * LANES - 1), axis=0).T
    flat = jnp.tile(u, (1, LANES))[:, :LANES * (period - 1)]
    return flat.reshape(n_heads, LANES, period - 1)[:, :, :2 * LANES].astype(F32)


def kernel(x, meta_tokens, rel_bias, ev_norm_mix, ev_w_mix_in, ev_g_q_lat, ev_g_kv_lat, ev_w_uq,
           ev_w_ukv, ev_mla_q_norm, ev_mla_k_norm, ev_dsa_q_norm, ev_dsa_k_norm, ev_w_mix_out,
           ev_norm_ffn, ev_w1, ev_w3, ev_w2, od_norm_mix, od_w_in, od_conv_w, od_w_out,
           od_norm_ffn, od_w_router, od_w1, od_w3, od_w2):
    b, seq, d = x.shape
    assert d == D_MODEL
    l_tot = seq + N_META
    lp = -(-l_tot // BLOCK_Q) * BLOCK_Q
    assert lp % SEQ_TILE == 0, "sequence tiling assumes the padded length is a multiple of 384"
    top_k = min(DSA_TOPK_MAX, l_tot // 4)
    depth = ev_norm_mix.shape[0] + od_norm_mix.shape[0]

    meta = jnp.broadcast_to(meta_tokens[None].astype(x.dtype), (b, N_META, d))
    h = jnp.concatenate([meta, x, jnp.zeros((b, lp - l_tot, d), x.dtype)], axis=1)
    h = h.reshape(b * lp, d)

    cos_t, sin_t = _rope_tables(lp)
    tz = _bias_tiles(rel_bias)
    row2 = lambda v: v[None, :]

    for layer in range(depth):
        i = layer // 2
        if layer % 2 == 0:
            wqm, wqs = _mla_q_weights(ev_w_uq[i])
            wkk, wkv = _mla_kv_weights(ev_w_ukv[i])
            gqm, gqs = _qk_gains(ev_mla_q_norm[i])
            gkm, gks = _qk_gains(ev_mla_k_norm[i])
            gdq = row2(jnp.concatenate([ev_dsa_q_norm[i]] * 2))
            gdk = row2(jnp.concatenate([ev_dsa_k_norm[i]] * 2))
            qm, km, vm, qd, kd, vd, iq, ik, iw = _prep_call(
                h, row2(ev_norm_mix[i]), _mix_in_weights(ev_w_mix_in[i]), row2(ev_g_q_lat[i]),
                row2(ev_g_kv_lat[i]), wqm, wqs, wkk, wkv, gqm, gqs, gkm, gks, gdq, gdk,
                cos_t, sin_t, lp)
            seq3 = lambda a: a.reshape(b, lp, a.shape[1])
            o_mla = _mla_call(seq3(qm), seq3(km), seq3(vm))
            o_dsa = _dsa_call(seq3(iq), seq3(iw), seq3(ik), seq3(qd), seq3(kd), seq3(vd), tz, top_k)
            w_o = ev_w_mix_out[i].astype(BF16)
            n_mla = MLA_HEADS * MLA_V
            h = _mix_out_ffn_call(h, o_mla.reshape(b * lp, -1), o_dsa.reshape(b * lp, -1),
                                  w_o[:n_mla], w_o[n_mla:], row2(ev_norm_ffn[i]),
                                  ev_w1[i], ev_w3[i], ev_w2[i])
        else:
            h = _conv_call(h.reshape(b, lp, d), row2(od_norm_mix[i]), od_w_in[i].astype(BF16),
                           od_conv_w[i].reshape(CONV_WIDTH, d), od_w_out[i].astype(BF16))
            h = h.reshape(b * lp, d)
            wr = jnp.concatenate(
                [od_w_router[i], jnp.zeros((d, LANES - N_EXPERTS), od_w_router.dtype)], axis=1)
            rows = (b, lp, N_META, seq) if layer == depth - 1 and seq % MOE_TILE == 0 else None
            h = _moe_call(h, row2(od_norm_ffn[i]), wr.astype(BF16), od_w1[i], od_w3[i], od_w2[i],
                          rows)
            if rows is not None:
                return h.reshape(b, seq, d)
    return h.reshape(b, lp, d)[:, N_META:l_tot]
```

```python
import functools
import math

import numpy as np
import jax
import jax.numpy as jnp
from jax import lax
from jax.experimental import pallas as pl
from jax.experimental.pallas import tpu as pltpu

F32 = jnp.float32
BF16 = jnp.bfloat16
I32 = jnp.int32
I16 = jnp.int16

D_MODEL = 1024
N_META = 16
BLOCK_Q = 128
EPS = 1e-6
MLA_HEADS = 8
MLA_Q_RANK = 384
MLA_KV_RANK = 256
MLA_NOPE = 64
MLA_ROPE = 32
MLA_V = 64
MLA_QK = MLA_NOPE + MLA_ROPE
MLA_SCALE = MLA_QK ** -0.5
ROPE_BASE = 10000.0
DSA_HEADS = 8
DSA_HEAD_DIM = 64
DSA_WIDTH = DSA_HEADS * DSA_HEAD_DIM
DSA_SCALE = DSA_HEAD_DIM ** -0.5
IDX_HEADS = 8
IDX_DIM = 32
DSA_TOPK_MAX = 256
REL_BUCKETS = 32
REL_MAX_DIST = 128
MIX_IN_SIZES = (MLA_Q_RANK, MLA_KV_RANK, MLA_ROPE, DSA_WIDTH, DSA_WIDTH, DSA_WIDTH,
                IDX_HEADS * IDX_DIM, IDX_DIM, IDX_HEADS)
CONV_WIDTH = 3
N_EXPERTS = 8
TOP_K = 2

LANES = 128
VMEM_LIMIT_BYTES = 56 * 1024 * 1024

SEQ_TILE = 3 * LANES
PREP_TILE = 2 * SEQ_TILE
MOE_TILE = 512
EXPERT_TILE = 2 * MOE_TILE
QK_LOOKAHEAD = 2
HALF_BITS = 16
BITS_PER_EXIT_TEST = 8
LOG2E = math.log2(math.e)
NEG_BIG = -1e30
INT_MIN = -2 ** 31

_C_CQ = 0
_C_CKV = _C_CQ + MLA_Q_RANK
_C_KRM = _C_CKV + MLA_KV_RANK
_C_KRS = _C_KRM + LANES
_C_DQ = _C_KRS + LANES
_C_DK = _C_DQ + DSA_WIDTH
_C_DV = _C_DK + DSA_WIDTH
_C_IQ = _C_DV + DSA_WIDTH
_C_IK = _C_IQ + IDX_HEADS * IDX_DIM
_C_IW = _C_IK + LANES
_C_END = _C_IW + LANES


def _cparams(sem):
    return pltpu.CompilerParams(dimension_semantics=sem, vmem_limit_bytes=VMEM_LIMIT_BYTES)


def _row_tile(n_rows, candidates=(1024, 768, 512, 384, 256, 128)):
    for c in candidates:
        if n_rows % c == 0:
            return c
    raise ValueError(f"no row tile for {n_rows}")


def _rms(x, g):
    ms = jnp.mean(x * x, axis=-1, keepdims=True)
    return x * lax.rsqrt(ms + EPS) * g


def _dot(a, b):
    return jnp.dot(a, b, preferred_element_type=F32)


def _dot_nt(a, b):
    return lax.dot_general(a, b, (((1,), (1,)), ((), ())), preferred_element_type=F32)


def _prep_kernel(h_ref, g_ref, wext_ref, gql_ref, gkvl_ref, wqm_ref, wqs_ref, wkk_ref, wkv_ref,
                 gqm_ref, gqs_ref, gkm_ref, gks_ref, gdq_ref, gdk_ref, cos_ref, sin_ref,
                 qm_o, km_o, vm_o, qd_o, kd_o, vd_o, iq_o, ik_o, iw_o):
    xn = _rms(h_ref[...], g_ref[...]).astype(BF16)

    def proj(lo, hi):
        return _dot(xn, wext_ref[:, lo:hi])

    cos = cos_ref[...]
    sin = sin_ref[...]
    lane = lax.broadcasted_iota(I32, (xn.shape[0], LANES), 1)

    cqn = _rms(proj(_C_CQ, _C_CKV), gql_ref[...]).astype(BF16)
    q_main = _dot(cqn, wqm_ref[...])
    q_swap = _dot(cqn, wqs_ref[...])
    for hd in range(MLA_HEADS):
        sl = slice(hd * LANES, (hd + 1) * LANES)
        a = q_main[:, sl]
        r = lax.rsqrt(jnp.sum(a * a, axis=-1, keepdims=True) * (1.0 / MLA_QK) + EPS)
        out = (a * r * gqm_ref[...]) * cos + (q_swap[:, sl] * r * gqs_ref[...]) * sin
        qm_o[:, sl] = (out * (MLA_SCALE * LOG2E)).astype(BF16)

    ckvn = _rms(proj(_C_CKV, _C_KRM), gkvl_ref[...]).astype(BF16)
    k_nope = _dot(ckvn, wkk_ref[...])
    vm_o[...] = _dot(ckvn, wkv_ref[...]).astype(BF16)
    kr_main = proj(_C_KRM, _C_KRS)
    kr_swap = proj(_C_KRS, _C_DQ)
    for hd in range(MLA_HEADS):
        sl = slice(hd * LANES, (hd + 1) * LANES)
        a = k_nope[:, sl] + kr_main
        r = lax.rsqrt(jnp.sum(a * a, axis=-1, keepdims=True) * (1.0 / MLA_QK) + EPS)
        out = (a * r * gkm_ref[...]) * cos + (kr_swap * r * gks_ref[...]) * sin
        km_o[:, sl] = out.astype(BF16)

    first = lane < DSA_HEAD_DIM
    for (lo, g2_ref, o_ref, post) in ((_C_DQ, gdq_ref, qd_o, DSA_SCALE * LOG2E),
                                      (_C_DK, gdk_ref, kd_o, None)):
        for pr in range(DSA_HEADS // 2):
            x = proj(lo + pr * LANES, lo + (pr + 1) * LANES)
            sq = x * x
            s0 = jnp.sum(jnp.where(first, sq, 0.0), axis=-1, keepdims=True)
            s1 = jnp.sum(jnp.where(first, 0.0, sq), axis=-1, keepdims=True)
            r0 = lax.rsqrt(s0 * (1.0 / DSA_HEAD_DIM) + EPS)
            r1 = lax.rsqrt(s1 * (1.0 / DSA_HEAD_DIM) + EPS)
            out = x * jnp.where(first, r0, r1) * g2_ref[...]
            if post is None:
                o_ref[:, pr * LANES:(pr + 1) * LANES] = out.astype(BF16)
            else:
                out = out * post
                o_ref[:, (2 * pr) * LANES:(2 * pr + 1) * LANES] = jnp.where(first, out, 0.0).astype(BF16)
                o_ref[:, (2 * pr + 1) * LANES:(2 * pr + 2) * LANES] = jnp.where(first, 0.0, out).astype(BF16)

    vd_o[...] = proj(_C_DV, _C_IQ).astype(BF16)
    lane_group = lax.shift_right_logical(lane, int(math.log2(IDX_DIM)))
    for quad in range(IDX_HEADS * IDX_DIM // LANES):
        x = proj(_C_IQ + quad * LANES, _C_IQ + (quad + 1) * LANES)
        for j in range(LANES // IDX_DIM):
            hd = quad * (LANES // IDX_DIM) + j
            iq_o[:, hd * LANES:(hd + 1) * LANES] = jnp.where(lane_group == j, x, 0.0).astype(BF16)
    ik_o[...] = proj(_C_IK, _C_IW).astype(BF16)
    iw_o[...] = proj(_C_IW, _C_END)


def _prep_call(h, g_mix, wext, gql, gkvl, wqm, wqs, wkk, wkv, gqm, gqs, gkm, gks, gdq, gdk,
               cos_t, sin_t, lp):
    t = h.shape[0]
    tm = PREP_TILE
    assert t % tm == 0
    reps = tm // math.gcd(lp, tm)
    nt = reps * lp // tm
    cos_t, sin_t = jnp.tile(cos_t, (reps, 1)), jnp.tile(sin_t, (reps, 1))
    row = lambda w: pl.BlockSpec((tm, w), lambda i: (i, 0))
    full = lambda a: pl.BlockSpec(a.shape, lambda i: (0, 0), pipeline_mode=pl.Buffered(1))
    tab = pl.BlockSpec((tm, LANES), lambda i: (i % nt, 0))
    hw = MLA_HEADS * LANES
    out_shape = [
        jax.ShapeDtypeStruct((t, hw), BF16), jax.ShapeDtypeStruct((t, hw), BF16),
        jax.ShapeDtypeStruct((t, MLA_HEADS * MLA_V), BF16),
        jax.ShapeDtypeStruct((t, DSA_HEADS * LANES), BF16), jax.ShapeDtypeStruct((t, DSA_WIDTH), BF16),
        jax.ShapeDtypeStruct((t, DSA_WIDTH), BF16),
        jax.ShapeDtypeStruct((t, IDX_HEADS * LANES), BF16),
        jax.ShapeDtypeStruct((t, LANES), BF16), jax.ShapeDtypeStruct((t, LANES), F32),
    ]
    return pl.pallas_call(
        _prep_kernel,
        grid=(t // tm,),
        in_specs=[row(D_MODEL), full(g_mix), full(wext), full(gql), full(gkvl), full(wqm),
                  full(wqs), full(wkk), full(wkv), full(gqm), full(gqs), full(gkm), full(gks),
                  full(gdq), full(gdk), tab, tab],
        out_specs=[row(s.shape[1]) for s in out_shape],
        out_shape=out_shape,
        compiler_params=_cparams(("parallel",)),
        name="prep_mix_in",
    )(h, g_mix, wext, gql, gkvl, wqm, wqs, wkk, wkv, gqm, gqs, gkm, gks, gdq, gdk, cos_t, sin_t)


def _flash_init(m_scr, l_scr, acc_scr):
    m_scr[...] = jnp.full(m_scr.shape, -jnp.inf, F32)
    l_scr[...] = jnp.zeros(l_scr.shape, F32)
    acc_scr[...] = jnp.zeros(acc_scr.shape, F32)


def _lane_fold(x, op):
    return functools.reduce(op, [x[:, j * LANES:(j + 1) * LANES] for j in range(x.shape[1] // LANES)])


def _flash_update(hd, s, vc, m_scr, l_scr, acc_scr):
    m_old = m_scr[hd]
    m_new = jnp.maximum(m_old, jnp.max(_lane_fold(s, jnp.maximum), axis=-1, keepdims=True))
    alpha = jnp.exp2(m_old - m_new)
    p = jnp.exp2(s - jnp.concatenate([m_new] * (s.shape[1] // LANES), axis=1))
    m_scr[hd] = m_new
    l_scr[hd] = alpha * l_scr[hd] + _lane_fold(p, jnp.add)
    acc_scr[hd] = alpha * acc_scr[hd] + _dot(p.astype(BF16), vc)


def _flash_store(o_ref, l_scr, acc_scr, head_dim):
    lane = lax.broadcasted_iota(I32, acc_scr.shape[1:], 1)
    for pr in range(acc_scr.shape[0] // 2):
        o0, o1 = [acc_scr[hd] / jnp.sum(l_scr[hd], axis=-1, keepdims=True)
                  for hd in (2 * pr, 2 * pr + 1)]
        o_ref[:, pr * LANES:(pr + 1) * LANES] = jnp.where(lane < head_dim, o0, o1).astype(o_ref.dtype)


def _mla_kernel(q_ref, k_ref, v_ref, o_ref, m_scr, l_scr, acc_scr):
    qi = pl.program_id(1)
    tq = q_ref.shape[0]
    _flash_init(m_scr, l_scr, acc_scr)

    def chunk(c, diagonal):
        ks = pl.ds(pl.multiple_of(c * tq, tq), tq)

        def scores(hd):
            sl = slice(hd * LANES, (hd + 1) * LANES)
            return _dot_nt(q_ref[:, sl], k_ref[ks, sl])

        pending = [scores(hd) for hd in range(QK_LOOKAHEAD)]
        for hd in range(MLA_HEADS):
            if hd + QK_LOOKAHEAD < MLA_HEADS:
                pending.append(scores(hd + QK_LOOKAHEAD))
            s = pending.pop(0)
            if diagonal:
                row = lax.broadcasted_iota(I32, (tq, tq), 0)
                col = lax.broadcasted_iota(I32, (tq, tq), 1)
                s = jnp.where(col <= row, s, NEG_BIG)
            pr = hd // 2
            _flash_update(hd, s, v_ref[ks, pr * LANES:(pr + 1) * LANES], m_scr, l_scr, acc_scr)

    def body(c, carry):
        chunk(c, False)
        return carry

    lax.fori_loop(0, qi, body, 0)
    chunk(qi, True)
    _flash_store(o_ref, l_scr, acc_scr, MLA_V)


def _mla_call(q, k, v):
    b, lp, _ = q.shape
    tq = SEQ_TILE
    nq = lp // tq
    state = pltpu.VMEM((MLA_HEADS, tq, LANES), F32)
    return pl.pallas_call(
        _mla_kernel,
        grid=(b, nq),
        in_specs=[pl.BlockSpec((None, tq, q.shape[2]), lambda bi, i: (bi, i, 0)),
                  pl.BlockSpec((None, lp, k.shape[2]), lambda bi, i: (bi, 0, 0)),
                  pl.BlockSpec((None, lp, v.shape[2]), lambda bi, i: (bi, 0, 0))],
        out_specs=pl.BlockSpec((None, tq, v.shape[2]), lambda bi, i: (bi, i, 0)),
        out_shape=jax.ShapeDtypeStruct((b, lp, v.shape[2]), BF16),
        scratch_shapes=[state, state, state],
        compiler_params=_cparams(("parallel", "arbitrary")),
        name="mla_attention",
    )(q, k, v)


def _dsa_kernel(iq_ref, iw_ref, ik_ref, q_ref, k_ref, v_ref, tz_ref, o_ref,
                key_scr, cap_scr, hi_scr, lo_scr, wt_scr, ans_scr, jst_scr, m_scr, l_scr, acc_scr,
                *, top_k):
    qi = pl.program_id(1)
    tq = q_ref.shape[0]
    nsub = tq // LANES
    sublanes = 8
    n_chunks = qi + 1
    key_i = lax.broadcasted_iota(I32, (tq, tq), 0)
    qry_i = lax.broadcasted_iota(I32, (tq, tq), 1)
    chunk_rows = lambda c: pl.ds(pl.multiple_of(c * tq, tq), tq)
    head_slot = lambda hd: slice(hd * LANES, (hd + 1) * LANES)
    wt_scr[...] = jnp.transpose(iw_ref[...])[0:IDX_HEADS, :]

    def index_chunk(c, diagonal):
        ikc = ik_ref[chunk_rows(c), :]
        sc = jnp.zeros((tq, tq), F32)
        for hd in range(IDX_HEADS):
            act = jnp.maximum(_dot_nt(ikc, iq_ref[:, head_slot(hd)]), 0.0)
            sc = sc + wt_scr[hd:hd + 1, :] * act
        sc = jnp.where(sc == 0.0, 0.0, sc)
        if diagonal:
            sc = jnp.where(key_i <= qry_i, sc, -jnp.inf)
        bits = pltpu.bitcast(sc, I32)
        key = bits ^ (lax.shift_right_arithmetic(bits, 31) & 0x7FFFFFFF)
        key_scr[c] = key
        hi_scr[c] = lax.shift_right_arithmetic(key, HALF_BITS).astype(I16)

    def index_body(c, carry):
        index_chunk(c, False)
        return carry

    lax.fori_loop(0, qi, index_body, 0)
    index_chunk(qi, True)

    kf = float(top_k)
    vec = (sublanes, tq)
    sub_i = lax.broadcasted_iota(I32, vec, 0)

    def count(pred):
        def body(c, accs):
            accs = list(accs)
            kk = key_scr[c]
            for g in range(tq // sublanes):
                hit = jnp.where(pred(kk[g * sublanes:(g + 1) * sublanes, :], c * tq + g * sublanes),
                                1.0, 0.0)
                accs[g % 2] = accs[g % 2] + hit
            return tuple(accs)
        zero = jnp.zeros(vec, F32)
        a0, a1 = lax.fori_loop(0, n_chunks, body, (zero, zero))
        return jnp.broadcast_to(jnp.sum(a0 + a1, axis=0, keepdims=True), vec)

    packed = 16

    def count_half(half_scr, cand):
        c16 = jnp.broadcast_to(cand[0:1, :], (packed, tq)).astype(I16)

        def one_chunk(c, accs):
            accs = list(accs)
            for g in range(tq // packed):
                kk = half_scr[c, g * packed:(g + 1) * packed, :]
                accs[g % 2] = accs[g % 2] + jnp.where(kk >= c16, jnp.int16(1), jnp.int16(0))
            return tuple(accs)

        def two_chunks(j, accs):
            return one_chunk(2 * j + 1, one_chunk(2 * j, accs))

        zero = jnp.zeros((packed, tq), I16)
        n_pairs = lax.shift_right_logical(n_chunks, 1)
        accs = lax.fori_loop(0, n_pairs, two_chunks, (zero, zero))
        a0, a1 = lax.fori_loop(2 * n_pairs, n_chunks, one_chunk, accs)
        tot = a0.astype(I32) + a1.astype(I32)
        return jnp.broadcast_to(jnp.sum(tot, axis=0, keepdims=True).astype(F32), vec)

    def n_open_of(done):
        return jnp.sum(jnp.where(done == 0, 1.0, 0.0))

    def bisect(count_ge, ans, done):
        def step(bit, ans, done):
            cand = ans + lax.shift_left(jnp.int32(1), bit)
            cnt = count_ge(cand)
            open_ = done == 0
            ans = jnp.where(jnp.logical_and(open_, cnt >= kf), cand, ans)
            done = jnp.where(jnp.logical_and(open_, cnt == kf), 1, done)
            return ans, done

        def cond(st):
            bit, _, _, n_open = st
            return jnp.logical_and(bit >= 0, n_open > 0.0)

        def body(st):
            bit, ans, done, _ = st
            for k in range(BITS_PER_EXIT_TEST):
                ans, done = step(bit - k, ans, done)
            return bit - BITS_PER_EXIT_TEST, ans, done, n_open_of(done)

        _, ans, done, _ = lax.while_loop(cond, body,
                                         (jnp.int32(HALF_BITS - 1), ans, done, n_open_of(done)))
        return ans, done

    t_pos = qi * tq + lax.broadcasted_iota(I32, vec, 1)
    done0 = (t_pos + 1 <= top_k).astype(I32)
    half_min = -(1 << (HALF_BITS - 1))

    hi_ans, done = bisect(lambda cand: count_half(hi_scr, cand), jnp.full(vec, half_min, I32), done0)

    hi_row = hi_ans[0:1, :]

    def low_chunk(c, carry):
        key = key_scr[c]
        low = (key & ((1 << HALF_BITS) - 1)) + half_min
        inside = lax.shift_right_arithmetic(key, HALF_BITS) == hi_row
        lo_scr[c] = jnp.where(inside, low, half_min).astype(I16)
        return carry

    lax.fori_loop(0, n_chunks, low_chunk, 0)
    above = count_half(hi_scr, hi_ans + 1)
    lo_ans, done = bisect(lambda cand: above + count_half(lo_scr, cand + half_min),
                          jnp.zeros(vec, I32), done)
    ans = lax.shift_left(hi_ans, HALF_BITS) + lo_ans
    n_open = n_open_of(done)
    ans_scr[...] = ans
    jst_scr[...] = jnp.full(vec, 2 ** 31 - 1, I32)

    @pl.when(n_open > 0.0)
    def _():
        need = kf - count(lambda kk, _i: kk > ans)

        def tie_body(i, jst):
            cand = jst + lax.shift_left(jnp.int32(1), 13 - i)
            cnt = count(lambda kk, i0: jnp.logical_and(kk == ans, sub_i + i0 < cand))
            return jnp.where(cnt < need, cand, jst)

        jst = lax.fori_loop(0, 14, tie_body, jnp.zeros(vec, I32))
        jst_scr[...] = jnp.where(done == 0, jst, 2 ** 31 - 1)

    ans_row = ans_scr[0:1, :]
    jst_row = jst_scr[0:1, :]

    def mask_chunk(c, diagonal):
        kk = key_scr[c]
        sel = jnp.logical_or(kk > ans_row,
                             jnp.logical_and(kk == ans_row, key_i + c * tq <= jst_row))
        if diagonal:
            sel = jnp.logical_and(sel, key_i <= qry_i)
        cap_scr[c] = jnp.transpose(jnp.where(sel, -NEG_BIG, NEG_BIG))

    def mask_body(c, carry):
        mask_chunk(c, False)
        return carry

    lax.fori_loop(0, qi, mask_body, 0)
    mask_chunk(qi, True)

    _flash_init(m_scr, l_scr, acc_scr)

    def bias_of(hd, where_):
        near = tz_ref[hd, :, LANES:2 * LANES]
        far = tz_ref[hd, :, 0:LANES]
        z = jnp.zeros((LANES, LANES), F32)
        if where_ == "previous":
            blocks = [[far if (a == 0 and b == nsub - 1) else z for b in range(nsub)]
                      for a in range(nsub)]
        else:
            blocks = [[near if b == a else far if b == a - 1 else z for b in range(nsub)]
                      for a in range(nsub)]
        return jnp.concatenate([jnp.concatenate(r, axis=1) for r in blocks], axis=0)

    def chunk(c, where_):
        ks = chunk_rows(c)
        cap = cap_scr[c]
        pair = lambda hd: slice((hd // 2) * LANES, (hd // 2 + 1) * LANES)
        scores = lambda hd: _dot_nt(q_ref[:, head_slot(hd)], k_ref[ks, pair(hd)])

        pending = [scores(hd) for hd in range(QK_LOOKAHEAD)]
        for hd in range(DSA_HEADS):
            if hd + QK_LOOKAHEAD < DSA_HEADS:
                pending.append(scores(hd + QK_LOOKAHEAD))
            s = jnp.minimum(pending.pop(0), cap)
            if where_ is not None:
                s = s + bias_of(hd, where_)
            _flash_update(hd, s, v_ref[ks, pair(hd)], m_scr, l_scr, acc_scr)

    def body(c, carry):
        chunk(c, None)
        return carry

    lax.fori_loop(0, jnp.maximum(qi - 1, 0), body, 0)

    @pl.when(qi >= 1)
    def _():
        chunk(qi - 1, "previous")

    chunk(qi, "diagonal")
    _flash_store(o_ref, l_scr, acc_scr, DSA_HEAD_DIM)


def _dsa_call(iq, iw, ik, q, k, v, tz, top_k):
    b, lp, _ = q.shape
    tq = SEQ_TILE
    nq = lp // tq
    qspec = lambda w: pl.BlockSpec((None, tq, w), lambda bi, i: (bi, i, 0))
    kspec = lambda w: pl.BlockSpec((None, lp, w), lambda bi, i: (bi, 0, 0))
    state = pltpu.VMEM((DSA_HEADS, tq, LANES), F32)
    return pl.pallas_call(
        functools.partial(_dsa_kernel, top_k=top_k),
        grid=(b, nq),
        in_specs=[qspec(iq.shape[2]), qspec(LANES), kspec(LANES), qspec(q.shape[2]),
                  kspec(DSA_WIDTH), kspec(DSA_WIDTH),
                  pl.BlockSpec(tz.shape, lambda bi, i: (0, 0, 0))],
        out_specs=qspec(DSA_WIDTH),
        out_shape=jax.ShapeDtypeStruct((b, lp, DSA_WIDTH), BF16),
        scratch_shapes=[pltpu.VMEM((nq, tq, tq), I32), pltpu.VMEM((nq, tq, tq), F32),
                        pltpu.VMEM((nq, tq, tq), I16), pltpu.VMEM((nq, tq, tq), I16),
                        pltpu.VMEM((IDX_HEADS, tq), F32),
                        pltpu.VMEM((8, tq), I32), pltpu.VMEM((8, tq), I32),
                        state, state, state],
        compiler_params=_cparams(("parallel", "arbitrary")),
        name="dsa_attention",
    )(iq, iw, ik, q, k, v, tz)


def _swiglu_chunk(xb, w1_ref, w3_ref, w2_ref):
    a = _dot(xb, w1_ref[...].astype(BF16))
    act = (a * jax.nn.sigmoid(a)) * _dot(xb, w3_ref[...].astype(BF16))
    return _dot(act.astype(BF16), w2_ref[...].astype(BF16))


def _mix_out_ffn_kernel(h_ref, a_ref, b_ref, wa_ref, wb_ref, g_ref, w1_ref, w3_ref, w2_ref, o_ref,
                        h1_scr, xn_scr, acc_scr):
    f = pl.program_id(1)

    @pl.when(f == 0)
    def _():
        h1 = h_ref[...] + _dot(a_ref[...], wa_ref[...]) + _dot(b_ref[...], wb_ref[...])
        h1_scr[...] = h1
        xn_scr[...] = _rms(h1, g_ref[...]).astype(BF16)
        acc_scr[...] = jnp.zeros(acc_scr.shape, F32)

    acc_scr[...] += _swiglu_chunk(xn_scr[...], w1_ref, w3_ref, w2_ref)

    @pl.when(f == pl.num_programs(1) - 1)
    def _():
        o_ref[...] = h1_scr[...] + acc_scr[...]


def _mix_out_ffn_call(h, a, b, wa, wb, g, w1, w3, w2):
    t = h.shape[0]
    d_ff = w1.shape[1]
    tm = _row_tile(t)
    tf = 512 if d_ff % 512 == 0 else d_ff
    row = lambda w: pl.BlockSpec((tm, w), lambda i, f: (i, 0))
    full = lambda x: pl.BlockSpec(x.shape, lambda i, f: (0, 0), pipeline_mode=pl.Buffered(1))
    return pl.pallas_call(
        _mix_out_ffn_kernel,
        grid=(t // tm, d_ff // tf),
        in_specs=[row(D_MODEL), row(a.shape[1]), row(b.shape[1]), full(wa), full(wb), full(g),
                  pl.BlockSpec((D_MODEL, tf), lambda i, f: (0, f)),
                  pl.BlockSpec((D_MODEL, tf), lambda i, f: (0, f)),
                  pl.BlockSpec((tf, D_MODEL), lambda i, f: (f, 0))],
        out_specs=row(D_MODEL),
        out_shape=jax.ShapeDtypeStruct(h.shape, F32),
        scratch_shapes=[pltpu.VMEM((tm, D_MODEL), F32), pltpu.VMEM((tm, D_MODEL), BF16),
                        pltpu.VMEM((tm, D_MODEL), F32)],
        compiler_params=_cparams(("parallel", "arbitrary")),
        name="mix_out_swiglu",
    )(h, a, b, wa, wb, g, w1, w3, w2)


def _conv_kernel(h_ref, g_ref, win_ref, cw_ref, wout_ref, o_ref, z_scr):
    i = pl.program_id(1)
    tm = h_ref.shape[0]
    halo = 8
    x = h_ref[...]
    xn = _rms(x, g_ref[...]).astype(BF16)
    c_gate = _dot(xn, win_ref[:, D_MODEL:2 * D_MODEL])
    u = _dot(xn, win_ref[:, 2 * D_MODEL:3 * D_MODEL])
    z = c_gate * u

    @pl.when(i == 0)
    def _():
        z_scr[0:halo, :] = jnp.zeros((halo, D_MODEL), F32)

    @pl.when(i > 0)
    def _():
        z_scr[0:halo, :] = z_scr[tm:tm + halo, :]

    z_scr[halo:halo + tm, :] = z
    cw = cw_ref[...]
    y = (cw[0:1, :] * z_scr[halo - 2:halo - 2 + tm, :]
         + cw[1:2, :] * z_scr[halo - 1:halo - 1 + tm, :]
         + cw[2:3, :] * z)
    b_gate = _dot(xn, win_ref[:, 0:D_MODEL])
    o_ref[...] = x + _dot((b_gate * y).astype(BF16), wout_ref[...])


def _conv_call(h3, g, w_in, cw, w_out):
    b, lp, d = h3.shape
    tm = SEQ_TILE
    full = lambda a: pl.BlockSpec(a.shape, lambda bi, i: (0,) * a.ndim)
    return pl.pallas_call(
        _conv_kernel,
        grid=(b, lp // tm),
        in_specs=[pl.BlockSpec((None, tm, d), lambda bi, i: (bi, i, 0)), full(g), full(w_in),
                  full(cw), full(w_out)],
        out_specs=pl.BlockSpec((None, tm, d), lambda bi, i: (bi, i, 0)),
        out_shape=jax.ShapeDtypeStruct(h3.shape, F32),
        scratch_shapes=[pltpu.VMEM((tm + 8, d), F32)],
        compiler_params=_cparams(("arbitrary", "arbitrary")),
        name="short_conv_mixer",
    )(h3, g, w_in, cw, w_out)


def _as_tiles(x):
    return x.reshape(x.shape[0], D_MODEL // LANES, LANES)


def _as_rows(x3):
    return x3.reshape(x3.shape[0], D_MODEL)


def _route_kernel(h_ref, g_ref, wr_ref, xg_ref, didx_ref, gate_ref, cnt_ref,
                  xs_scr, base_scr, dvm_scr, dsm_scr, sem_idx, sem_rows, *, stride):
    i = pl.program_id(0)
    tm = h_ref.shape[0]
    lane = lax.broadcasted_iota(I32, (tm, LANES), 1)
    lane_f = lane.astype(F32)

    @pl.when(i == 0)
    def _():
        base_scr[...] = jnp.zeros(base_scr.shape, F32)

    xnf = _rms(h_ref[...], g_ref[...])
    logits = jnp.where(lane < N_EXPERTS, _dot(xnf.astype(BF16), wr_ref[...]), -jnp.inf)
    v1 = jnp.max(logits, axis=-1, keepdims=True)
    i1 = jnp.min(jnp.where(logits == v1, lane_f, float(LANES)), axis=-1, keepdims=True)
    rest = jnp.where(lane_f == i1, -jnp.inf, logits)
    v2 = jnp.max(rest, axis=-1, keepdims=True)
    i2 = jnp.min(jnp.where(rest == v2, lane_f, float(LANES)), axis=-1, keepdims=True)
    e2 = jnp.exp(v2 - v1)
    den = 1.0 + e2
    gates = (1.0 / den, e2 / den)

    oh1 = jnp.where(lane_f == i1, 1.0, 0.0)
    oh2 = jnp.where(lane_f == i2, 1.0, 0.0)
    oh = oh1 + oh2
    earlier = (lax.broadcasted_iota(I32, (tm, tm), 1) < lax.broadcasted_iota(I32, (tm, tm), 0))
    prefix = _dot(jnp.where(earlier, 1.0, 0.0).astype(BF16), oh.astype(BF16))
    pos = base_scr[0:1, :] + prefix
    d1 = jnp.sum(oh1 * pos, axis=-1, keepdims=True) + i1 * float(stride)
    d2 = jnp.sum(oh2 * pos, axis=-1, keepdims=True) + i2 * float(stride)
    base_scr[...] = base_scr[...] + jnp.sum(oh, axis=0, keepdims=True)
    cnt_ref[...] = base_scr[...]

    gate_ref[...] = jnp.where(lane == 0, gates[0], jnp.where(lane == 1, gates[1], 0.0))
    slot = lax.rem(i, 2)
    xs_scr[slot] = _as_tiles(xnf)

    dmat = jnp.where(lane == 0, d1, jnp.where(lane == 1, d2, 0.0))
    dvm_scr[...] = jnp.transpose(dmat)[0:8, :].astype(I32)
    didx_ref[...] = dvm_scr[...]
    cp = pltpu.make_async_copy(dvm_scr, dsm_scr, sem_idx)
    cp.start()
    cp.wait()

    for r in range(tm):
        for s in range(TOP_K):
            pltpu.make_async_copy(xs_scr.at[slot, r], xg_ref.at[dsm_scr[s, r]],
                                  sem_rows.at[slot]).start(priority=s)

    def wait_rows(sl):
        for s in range(TOP_K):
            pltpu.make_async_copy(xs_scr.at[sl], xg_ref.at[pl.ds(0, tm)], sem_rows.at[sl]).wait()

    @pl.when(i > 0)
    def _():
        wait_rows(1 - slot)

    @pl.when(i == pl.num_programs(0) - 1)
    def _():
        wait_rows(slot)
        xs_scr[0] = jnp.zeros(xs_scr.shape[1:], F32)
        dvm_scr[:, 0:LANES] = base_scr[...].astype(I32)
        cp2 = pltpu.make_async_copy(dvm_scr, dsm_scr, sem_idx)
        cp2.start()
        cp2.wait()
        for phase in ("start", "wait"):
            for e in range(N_EXPERTS):
                for k in range(EXPERT_TILE // tm):
                    blk = pltpu.make_async_copy(
                        xs_scr.at[0], xg_ref.at[pl.ds(e * stride + dsm_scr[0, e] + k * tm, tm)],
                        sem_rows.at[0])
                    blk.start() if phase == "start" else blk.wait()


def _route_call(h, g, wr, stride):
    t = h.shape[0]
    tm = MOE_TILE
    return pl.pallas_call(
        functools.partial(_route_kernel, stride=stride),
        grid=(t // tm,),
        in_specs=[pl.BlockSpec((tm, D_MODEL), lambda i: (i, 0)),
                  pl.BlockSpec(g.shape, lambda i: (0, 0)),
                  pl.BlockSpec(wr.shape, lambda i: (0, 0))],
        out_specs=[pl.BlockSpec(memory_space=pl.ANY),
                   pl.BlockSpec((8, tm), lambda i: (0, i)),
                   pl.BlockSpec((tm, LANES), lambda i: (i, 0)),
                   pl.BlockSpec((8, LANES), lambda i: (0, 0))],
        out_shape=[jax.ShapeDtypeStruct((N_EXPERTS * stride, D_MODEL // LANES, LANES), F32),
                   jax.ShapeDtypeStruct((8, t), I32),
                   jax.ShapeDtypeStruct((t, LANES), F32),
                   jax.ShapeDtypeStruct((8, LANES), F32)],
        scratch_shapes=[pltpu.VMEM((2, tm, D_MODEL // LANES, LANES), F32), pltpu.VMEM((8, LANES), F32),
                        pltpu.VMEM((8, tm), I32), pltpu.SMEM((8, tm), I32),
                        pltpu.SemaphoreType.DMA, pltpu.SemaphoreType.DMA((2,))],
        compiler_params=_cparams(("arbitrary",)),
        name="moe_route",
    )(h, g, wr)


def _expert_kernel(blk_ref, exp_ref, nu_ref, x_ref, w1_ref, w3_ref, w2_ref, o_ref, xb_scr, acc_scr):
    j = pl.program_id(0)
    f = pl.program_id(1)

    @pl.when(j < nu_ref[0])
    def _():
        @pl.when(f == 0)
        def _():
            xb_scr[...] = _as_rows(x_ref[...]).astype(BF16)
            acc_scr[...] = jnp.zeros(acc_scr.shape, F32)

        acc_scr[...] += _swiglu_chunk(xb_scr[...], w1_ref, w3_ref, w2_ref)

        @pl.when(f == pl.num_programs(1) - 1)
        def _():
            o_ref[...] = _as_tiles(acc_scr[...])


def _expert_call(tile_blk, tile_exp, n_used, xg, w1, w3, w2):
    d_ff = w1.shape[2]
    tm = EXPERT_TILE
    tf = 512 if d_ff % 512 == 0 else d_ff
    nf = d_ff // tf
    n_tiles = tile_blk.shape[0]
    fe = lambda j, f, nu: jnp.where(j < nu[0], f, nf - 1)
    tile_block = pl.BlockSpec((tm, D_MODEL // LANES, LANES), lambda j, f, blk, ex, nu: (blk[j], 0, 0))
    grid_spec = pltpu.PrefetchScalarGridSpec(
        num_scalar_prefetch=3,
        grid=(n_tiles, nf),
        in_specs=[tile_block,
                  pl.BlockSpec((None, D_MODEL, tf), lambda j, f, blk, ex, nu: (ex[j], 0, fe(j, f, nu))),
                  pl.BlockSpec((None, D_MODEL, tf), lambda j, f, blk, ex, nu: (ex[j], 0, fe(j, f, nu))),
                  pl.BlockSpec((None, tf, D_MODEL), lambda j, f, blk, ex, nu: (ex[j], fe(j, f, nu), 0))],
        out_specs=tile_block,
        scratch_shapes=[pltpu.VMEM((tm, D_MODEL), BF16), pltpu.VMEM((tm, D_MODEL), F32)])
    return pl.pallas_call(
        _expert_kernel,
        grid_spec=grid_spec,
        out_shape=jax.ShapeDtypeStruct(xg.shape, F32),
        compiler_params=_cparams(("arbitrary", "arbitrary")),
        name="expert_swiglu",
    )(tile_blk, tile_exp, n_used, xg, w1, w3, w2)


def _combine_kernel(h_ref, gate_ref, didx_ref, yg_ref, o_ref, y_scr, dsm_scr, sem_idx, sem_rows):
    i = pl.program_id(0)
    tm = h_ref.shape[0]
    slot = lax.rem(i, 2)

    def fetch(j, sl):
        cp = pltpu.make_async_copy(didx_ref.at[:, pl.ds(pl.multiple_of(j * tm, tm), tm)], dsm_scr,
                                   sem_idx)
        cp.start()
        cp.wait()
        for r in range(tm):
            for s in range(TOP_K):
                pltpu.make_async_copy(yg_ref.at[dsm_scr[s, r]], y_scr.at[sl, s, r],
                                      sem_rows.at[sl]).start(priority=s)

    @pl.when(i == 0)
    def _():
        fetch(0, 0)

    @pl.when(i + 1 < pl.num_programs(0))
    def _():
        fetch(i + 1, 1 - slot)

    for s in range(TOP_K):
        pltpu.make_async_copy(yg_ref.at[pl.ds(0, tm)], y_scr.at[slot, s], sem_rows.at[slot]).wait()
    gate = gate_ref[...]
    o_ref[...] = (h_ref[...] + gate[:, 0:1] * _as_rows(y_scr[slot, 0])
                  + gate[:, 1:2] * _as_rows(y_scr[slot, 1]))


def _combine_call(h, gate, didx, yg, rows=None):
    t = h.shape[0]
    tm = MOE_TILE
    n_seq, seq_stride, first, n = rows if rows is not None else (1, 0, 0, t)
    assert n % tm == 0 and first % 8 == 0 and seq_stride % 8 == 0
    tiles = n // tm
    if rows is not None:
        didx = didx.reshape(8, n_seq, seq_stride)[:, :, first:first + n].reshape(8, n_seq * n)

    def token_rows(width):
        start = lambda i: pl.multiple_of((i // tiles) * seq_stride + first + (i % tiles) * tm, 8)
        return pl.BlockSpec((pl.Element(tm), pl.Element(width)), lambda i: (start(i), 0))

    return pl.pallas_call(
        _combine_kernel,
        grid=(n_seq * tiles,),
        in_specs=[token_rows(D_MODEL), token_rows(LANES),
                  pl.BlockSpec(memory_space=pl.ANY), pl.BlockSpec(memory_space=pl.ANY)],
        out_specs=pl.BlockSpec((tm, D_MODEL), lambda i: (i, 0)),
        out_shape=jax.ShapeDtypeStruct((n_seq * n, D_MODEL), F32),
        scratch_shapes=[pltpu.VMEM((2, TOP_K, tm, D_MODEL // LANES, LANES), F32),
                        pltpu.SMEM((8, tm), I32),
                        pltpu.SemaphoreType.DMA, pltpu.SemaphoreType.DMA((2,))],
        compiler_params=_cparams(("arbitrary",)),
        name="moe_combine",
    )(h, gate, didx, yg)


def _moe_call(h, g, wr, w1, w3, w2, rows=None):
    t = h.shape[0]
    tm = EXPERT_TILE
    assert t % MOE_TILE == 0 and tm % MOE_TILE == 0
    stride = -(-t // tm) * tm + tm
    xg, didx, gate, cnt = _route_call(h, g, wr, stride)
    counts = cnt[0, :N_EXPERTS].astype(I32)
    tiles_e = (counts + tm - 1) // tm
    cum = jnp.cumsum(tiles_e)
    n_used = cum[-1]
    n_tiles = -(-TOP_K * t // tm) + N_EXPERTS
    jj = jnp.minimum(jnp.arange(n_tiles, dtype=I32), n_used - 1)
    tile_exp = jnp.sum((jj[:, None] >= cum[None, :]).astype(I32), axis=1)
    tile_blk = tile_exp * (stride // tm) + jj - (cum - tiles_e)[tile_exp]
    yg = _expert_call(tile_blk, tile_exp, n_used[None], xg, w1, w3, w2)
    return _combine_call(h, gate, didx, yg, rows)


def _mix_in_weights(w):
    offs = np.concatenate([[0], np.cumsum(MIX_IN_SIZES)])
    cq, ckv, kr, dq, dk, dv, iq, ik, iw = [w[:, offs[j]:offs[j + 1]] for j in range(9)]
    z = lambda n: jnp.zeros((w.shape[0], n), w.dtype)
    half = MLA_ROPE // 2
    kr_main = jnp.concatenate([z(MLA_NOPE), kr, z(LANES - MLA_QK)], axis=1)
    kr_swap = jnp.concatenate([z(MLA_NOPE), kr[:, half:], kr[:, :half], z(LANES - MLA_QK)], axis=1)
    ik4 = jnp.concatenate([ik] * (LANES // IDX_DIM), axis=1)
    iw_p = jnp.concatenate([iw, z(LANES - IDX_HEADS)], axis=1)
    return jnp.concatenate([cq, ckv, kr_main, kr_swap, dq, dk, dv, iq, ik4, iw_p],
                           axis=1).astype(BF16)


def _mla_q_weights(w_uq):
    r = w_uq.shape[0]
    w = w_uq.reshape(r, MLA_HEADS, MLA_QK)
    nope, rope = w[..., :MLA_NOPE], w[..., MLA_NOPE:]
    half = MLA_ROPE // 2
    z = lambda n: jnp.zeros((r, MLA_HEADS, n), w.dtype)
    main = jnp.concatenate([nope, rope, z(LANES - MLA_QK)], axis=-1)
    swap = jnp.concatenate([z(MLA_NOPE), rope[..., half:], rope[..., :half], z(LANES - MLA_QK)],
                           axis=-1)
    return (main.reshape(r, MLA_HEADS * LANES).astype(BF16),
            swap.reshape(r, MLA_HEADS * LANES).astype(BF16))


def _mla_kv_weights(w_ukv):
    r = w_ukv.shape[0]
    w = w_ukv.reshape(r, MLA_HEADS, MLA_NOPE + MLA_V)
    k_nope = jnp.concatenate([w[..., :MLA_NOPE], jnp.zeros((r, MLA_HEADS, LANES - MLA_NOPE), w.dtype)],
                             axis=-1)
    return (k_nope.reshape(r, MLA_HEADS * LANES).astype(BF16),
            w[..., MLA_NOPE:].reshape(r, MLA_HEADS * MLA_V).astype(BF16))


def _qk_gains(g):
    half = MLA_ROPE // 2
    z = lambda n: jnp.zeros((n,), g.dtype)
    main = jnp.concatenate([g, z(LANES - MLA_QK)])
    swap = jnp.concatenate([z(MLA_NOPE), g[MLA_NOPE + half:], g[MLA_NOPE:MLA_NOPE + half],
                            z(LANES - MLA_QK)])
    return main[None, :], swap[None, :]


def _rope_tables(lp):
    half = MLA_ROPE // 2
    inv = ROPE_BASE ** (-jnp.arange(half, dtype=F32) / half)
    ang = jnp.arange(lp, dtype=jnp.int32).astype(F32)[:, None] * inv[None, :]
    cos, sin = jnp.cos(ang), jnp.sin(ang)
    ones = jnp.ones((lp, MLA_NOPE), F32)
    pad1 = jnp.ones((lp, LANES - MLA_QK), F32)
    zeros = jnp.zeros((lp, MLA_NOPE), F32)
    pad0 = jnp.zeros((lp, LANES - MLA_QK), F32)
    return (jnp.concatenate([ones, cos, cos, pad1], axis=1),
            jnp.concatenate([zeros, -sin, sin, pad0], axis=1))


def _rel_buckets(n):
    max_exact = REL_BUCKETS // 2
    d = np.arange(n)
    df = np.maximum(d, 1).astype(np.float32)
    large = max_exact + (np.log(df / np.float32(max_exact))
                         / np.float32(math.log(REL_MAX_DIST / max_exact))
                         * np.float32(REL_BUCKETS - max_exact)).astype(np.int32)
    large = np.minimum(large, REL_BUCKETS - 1)
    return np.where(d < max_exact, d, large)


def _bias_tiles(rel_bias):
    buckets = _rel_buckets(2 * LANES)
    assert np.all(buckets[LANES - 1:] == REL_BUCKETS - 1)
    n_heads = rel_bias.shape[1]
    shifted = (rel_bias - rel_bias[REL_BUCKETS - 1:REL_BUCKETS, :]) * LOG2E
    period = 3 * LANES
    w = jnp.concatenate([jnp.take(shifted, jnp.asarray(buckets), axis=0),
                         jnp.zeros((period - 2 * LANES, n_heads), F32)], axis=0)
    u = jnp.roll(jnp.flip(w, axis=0), -(2 * LANES - 1), axis=0).T
    flat = jnp.tile(u, (1, LANES))[:, :LANES * (period - 1)]
    return flat.reshape(n_heads, LANES, period - 1)[:, :, :2 * LANES].astype(F32)


def kernel(x, meta_tokens, rel_bias, ev_norm_mix, ev_w_mix_in, ev_g_q_lat, ev_g_kv_lat, ev_w_uq,
           ev_w_ukv, ev_mla_q_norm, ev_mla_k_norm, ev_dsa_q_norm, ev_dsa_k_norm, ev_w_mix_out,
           ev_norm_ffn, ev_w1, ev_w3, ev_w2, od_norm_mix, od_w_in, od_conv_w, od_w_out,
           od_norm_ffn, od_w_router, od_w1, od_w3, od_w2):
    b, seq, d = x.shape
    assert d == D_MODEL
    l_tot = seq + N_META
    lp = -(-l_tot // BLOCK_Q) * BLOCK_Q
    assert lp % SEQ_TILE == 0, "sequence tiling assumes the padded length is a multiple of 384"
    top_k = min(DSA_TOPK_MAX, l_tot // 4)
    depth = ev_norm_mix.shape[0] + od_norm_mix.shape[0]

    meta = jnp.broadcast_to(meta_tokens[None].astype(x.dtype), (b, N_META, d))
    h = jnp.concatenate([meta, x, jnp.zeros((b, lp - l_tot, d), x.dtype)], axis=1)
    h = h.reshape(b * lp, d)

    cos_t, sin_t = _rope_tables(lp)
    tz = _bias_tiles(rel_bias)
    row2 = lambda v: v[None, :]

    for layer in range(depth):
        i = layer // 2
        if layer % 2 == 0:
            wqm, wqs = _mla_q_weights(ev_w_uq[i])
            wkk, wkv = _mla_kv_weights(ev_w_ukv[i])
            gqm, gqs = _qk_gains(ev_mla_q_norm[i])
            gkm, gks = _qk_gains(ev_mla_k_norm[i])
            gdq = row2(jnp.concatenate([ev_dsa_q_norm[i]] * 2))
            gdk = row2(jnp.concatenate([ev_dsa_k_norm[i]] * 2))
            qm, km, vm, qd, kd, vd, iq, ik, iw = _prep_call(
                h, row2(ev_norm_mix[i]), _mix_in_weights(ev_w_mix_in[i]), row2(ev_g_q_lat[i]),
                row2(ev_g_kv_lat[i]), wqm, wqs, wkk, wkv, gqm, gqs, gkm, gks, gdq, gdk,
                cos_t, sin_t, lp)
            seq3 = lambda a: a.reshape(b, lp, a.shape[1])
            o_mla = _mla_call(seq3(qm), seq3(km), seq3(vm))
            o_dsa = _dsa_call(seq3(iq), seq3(iw), seq3(ik), seq3(qd), seq3(kd), seq3(vd), tz, top_k)
            w_o = ev_w_mix_out[i].astype(BF16)
            n_mla = MLA_HEADS * MLA_V
            h = _mix_out_ffn_call(h, o_mla.reshape(b * lp, -1), o_dsa.reshape(b * lp, -1),
                                  w_o[:n_mla], w_o[n_mla:], row2(ev_norm_ffn[i]),
                                  ev_w1[i], ev_w3[i], ev_w2[i])
        else:
            h = _conv_call(h.reshape(b, lp, d), row2(od_norm_mix[i]), od_w_in[i].astype(BF16),
                           od_conv_w[i].reshape(CONV_WIDTH, d), od_w_out[i].astype(BF16))
            h = h.reshape(b * lp, d)
            wr = jnp.concatenate(
                [od_w_router[i], jnp.zeros((d, LANES - N_EXPERTS), od_w_router.dtype)], axis=1)
            rows = (b, lp, N_META, seq) if layer == depth - 1 and seq % MOE_TILE == 0 else None
            h = _moe_call(h, row2(od_norm_ffn[i]), wr.astype(BF16), od_w1[i], od_w3[i], od_w2[i],
                          rows)
            if rows is not None:
                return h.reshape(b, seq, d)
    return h.reshape(b, lp, d)[:, N_META:l_tot]
```
